```python
import math
import numpy as np
import jax
import jax.numpy as jnp
from jax import lax

D_MODEL = 2048
BATCH = 2
SEQ = 4096
DEPTH = 4

N_META = 16
BLOCK = 128
ROPE_THETA = 500000.0
LN_EPS = 1e-5
RMS_EPS = 1e-6
ALPHA = (2 * DEPTH) ** 0.25
BETA = (8 * DEPTH) ** -0.25

A_HEADS = 8
A_KV_HEADS = 2
A_HEAD_DIM = 128
A_ROT = A_HEAD_DIM // 4
IDX_HEADS = 8
IDX_DIM = 64
IDX_ROT = IDX_DIM // 4
TOPK_MAX = 256

MLA_HEADS = 8
MLA_Q_LORA = 512
MLA_KV_LORA = 512
MLA_NOPE = 128
MLA_ROPE = 64
MLA_V = 128

SSM_WIDTH = 1024
SSM_GROUP = 16
SSM_GROUPS = SSM_WIDTH // SSM_GROUP
SSM_STATE = 64

N_BRANCH = 3
A_WIDTH = A_HEADS * A_HEAD_DIM
MLA_WIDTH = MLA_HEADS * MLA_V
IN_SIZES = (A_HEADS * A_HEAD_DIM, A_KV_HEADS * A_HEAD_DIM, A_KV_HEADS * A_HEAD_DIM,
            IDX_HEADS * IDX_DIM, IDX_DIM, IDX_HEADS,
            MLA_Q_LORA, MLA_KV_LORA, MLA_ROPE,
            SSM_WIDTH, N_BRANCH * D_MODEL)
IN_WIDTH = sum(IN_SIZES)

D_FF = 5632
N_EXPERTS = 8
TOP_K_EXPERTS = 2
MOE_BLOCK = 128
N_DENSE = (DEPTH + 1) // 2
N_MOE = DEPTH // 2

kernel_name = 'hybrid_dsa_mla_s5_moe_trunk'


def _layernorm(x, g, b):
    xf = x.astype(jnp.float32)
    mu = jnp.mean(xf, axis=-1, keepdims=True)
    var = jnp.mean(jnp.square(xf - mu), axis=-1, keepdims=True)
    return ((xf - mu) * lax.rsqrt(var + LN_EPS) * g.astype(jnp.float32) + b.astype(jnp.float32)).astype(x.dtype)


def _rmsnorm(x, g):
    xf = x.astype(jnp.float32)
    ms = jnp.mean(xf * xf, axis=-1, keepdims=True)
    return (xf * lax.rsqrt(ms + RMS_EPS) * g.astype(jnp.float32)).astype(x.dtype)


def _rope_tables(pos, dim):
    inv = ROPE_THETA ** (-jnp.arange(0, dim, 2, dtype=jnp.float32) / dim)
    ang = pos.astype(jnp.float32)[:, None] * inv[None, :]
    return (jnp.cos(ang), jnp.sin(ang))


def _rotate(x, rope):
    cos, sin = rope
    c = cos[None, :, None, :].astype(x.dtype)
    s = sin[None, :, None, :].astype(x.dtype)
    x1, x2 = jnp.split(x, 2, axis=-1)
    return jnp.concatenate([x1 * c - x2 * s, x2 * c + x1 * s], axis=-1)


def _partial_rope(x, rope, rot):
    return jnp.concatenate([_rotate(x[..., :rot], rope), x[..., rot:]], axis=-1)


def _to_blocks(a):
    b, l = a.shape[0], a.shape[1]
    return jnp.moveaxis(a.reshape((b, l // BLOCK, BLOCK) + a.shape[2:]), 1, 0)


def _from_blocks(a):
    a = jnp.moveaxis(a, 0, 1)
    return a.reshape((a.shape[0], a.shape[1] * a.shape[2]) + a.shape[3:])


def _dsa_attention(q, k, v, qi, ki, wi, n_sel):
    bsz, lp = q.shape[0], q.shape[1]
    rep = A_HEADS // A_KV_HEADS
    key_pos = jnp.arange(lp)
    bidx = jnp.arange(bsz)[:, None, None]
    scale = A_HEAD_DIM ** -0.5

    def block(args):
        blk, q_b, qi_b, wi_b = args
        qpos = blk * BLOCK + jnp.arange(BLOCK)
        rel = jax.nn.relu(jnp.einsum('bqhd,bsd->bqhs', qi_b, ki).astype(jnp.float32))
        score = jnp.einsum('bqhs,bqh->bqs', rel, wi_b.astype(jnp.float32))
        score = jnp.where(key_pos[None, None, :] < N_META, jnp.inf, score)
        score = jnp.where((key_pos[None, :] <= qpos[:, None])[None], score, -jnp.inf)
        _, sel = lax.top_k(score, n_sel)
        k_sel = k[bidx, sel]
        v_sel = v[bidx, sel]
        qg = q_b.reshape(bsz, BLOCK, A_KV_HEADS, rep, A_HEAD_DIM)
        logits = jnp.einsum('bqgrd,bqkgd->bqgrk', qg, k_sel).astype(jnp.float32) * scale
        valid = (sel <= qpos[None, :, None])[:, :, None, None, :]
        p = jax.nn.softmax(jnp.where(valid, logits, -jnp.inf), axis=-1).astype(v.dtype)
        o = jnp.einsum('bqgrk,bqkgd->bqgrd', p, v_sel)
        return o.reshape(bsz, BLOCK, A_WIDTH)

    nb = lp // BLOCK
    out = lax.map(block, (jnp.arange(nb), _to_blocks(q), _to_blocks(qi), _to_blocks(wi)))
    return _from_blocks(out)


def _mla_attention(q, k, v):
    bsz, lp = q.shape[0], q.shape[1]
    key_pos = jnp.arange(lp)
    scale = (MLA_NOPE + MLA_ROPE) ** -0.5

    def block(args):
        blk, q_b = args
        qpos = blk * BLOCK + jnp.arange(BLOCK)
        logits = jnp.einsum('bqhd,bkhd->bhqk', q_b, k).astype(jnp.float32) * scale
        mask = (key_pos[None, :] <= qpos[:, None])[None, None]
        p = jax.nn.softmax(jnp.where(mask, logits, -jnp.inf), axis=-1).astype(v.dtype)
        o = jnp.einsum('bhqk,bkhd->bqhd', p, v)
        return o.reshape(bsz, BLOCK, MLA_WIDTH)

    nb = lp // BLOCK
    out = lax.map(block, (jnp.arange(nb), _to_blocks(q)))
    return _from_blocks(out)


def _ssm_combine(e1, e2):
    a1r, a1i, b1r, b1i = e1
    a2r, a2i, b2r, b2i = e2
    return (a2r * a1r - a2i * a1i,
            a2r * a1i + a2i * a1r,
            a2r * b1r - a2i * b1i + b2r,
            a2r * b1i + a2i * b1r + b2i)


def _s5(u, a_re, a_im, log_dt, b_re, b_im, c_re, c_im, d_skip):
    f32 = jnp.float32
    bsz, lp, _ = u.shape
    uf = u.astype(f32)
    ug = uf.reshape(bsz, lp, SSM_GROUPS, SSM_GROUP)
    ar, ai = a_re.astype(f32), a_im.astype(f32)
    dt = jnp.exp(log_dt.astype(f32))[:, None]
    mag = jnp.exp(dt * ar)
    ab_re, ab_im = mag * jnp.cos(dt * ai), mag * jnp.sin(dt * ai)
    den = ar * ar + ai * ai
    f_re = ((ab_re - 1.0) * ar + ab_im * ai) / den
    f_im = (ab_im * ar - (ab_re - 1.0) * ai) / den
    br, bi = b_re.astype(f32), b_im.astype(f32)
    bb_re = f_re[..., None] * br - f_im[..., None] * bi
    bb_im = f_re[..., None] * bi + f_im[..., None] * br
    bu_re = jnp.einsum('gpc,blgc->blgp', bb_re, ug)
    bu_im = jnp.einsum('gpc,blgc->blgp', bb_im, ug)
    shp = bu_re.shape
    elems = (jnp.broadcast_to(ab_re, shp), jnp.broadcast_to(ab_im, shp), bu_re, bu_im)
    _, _, s_re, s_im = lax.associative_scan(_ssm_combine, elems, axis=1)
    y = (jnp.einsum('gcp,blgp->blgc', c_re.astype(f32), s_re)
         - jnp.einsum('gcp,blgp->blgc', c_im.astype(f32), s_im))
    y = y.reshape(bsz, lp, SSM_WIDTH) + d_skip.astype(f32) * uf
    return y.astype(u.dtype)


def _swiglu(h, w1, w3, w2):
    return (jax.nn.silu(h @ w1) * (h @ w3)) @ w2


def _moe(h, w_router, w1, w3, w2):
    t = h.shape[0]
    logits = h.astype(jnp.float32) @ w_router.astype(jnp.float32)
    top_val, top_idx = lax.top_k(logits, TOP_K_EXPERTS)
    gates = jax.nn.softmax(top_val, axis=-1)
    n_assign = t * TOP_K_EXPERTS
    expert = top_idx.reshape(-1)
    token = jnp.arange(n_assign) // TOP_K_EXPERTS
    order = jnp.argsort(expert)
    e_sorted = expert[order]
    counts = jnp.bincount(expert, length=N_EXPERTS)
    padded = (counts + MOE_BLOCK - 1) // MOE_BLOCK * MOE_BLOCK
    start = jnp.cumsum(counts) - counts
    pad_end = jnp.cumsum(padded)
    pad_start = pad_end - padded
    dest = pad_start[e_sorted] + (jnp.arange(n_assign) - start[e_sorted])
    n_rows = (n_assign // MOE_BLOCK + N_EXPERTS + 1) * MOE_BLOCK
    row_token = jnp.zeros((n_rows,), jnp.int32).at[dest].set(token[order])
    row_gate = jnp.zeros((n_rows,), jnp.float32).at[dest].set(gates.reshape(-1)[order])
    n_blk = n_rows // MOE_BLOCK
    blk_expert = jnp.minimum(
        jnp.searchsorted(pad_end, jnp.arange(n_blk) * MOE_BLOCK, side='right'), N_EXPERTS - 1)
    xs = h[row_token].reshape(n_blk, MOE_BLOCK, h.shape[1])

    def expert_block(args):
        e, xb = args
        return (jax.nn.silu(xb @ w1[e]) * (xb @ w3[e])) @ w2[e]

    ys = lax.map(expert_block, (blk_expert, xs)).reshape(n_rows, h.shape[1])
    return jax.ops.segment_sum(ys * row_gate[:, None].astype(ys.dtype), row_token, num_segments=t)


def _mixer(h, rope_a, rope_i, rope_m, n_sel, w_in, b_gate, q_norm, kv_norm, w_uq, w_ukv,
           a_re, a_im, log_dt, b_re, b_im, c_re, c_im, d_skip, w_glu, wb_a, wb_b, wb_c, w_o):
    bsz, lp, _ = h.shape
    split_at = np.cumsum(IN_SIZES)[:-1].tolist()
    aq, ak, av, iq, ik, iw, dq, dkv, kr, su, gl = jnp.split(h @ w_in, split_at, axis=-1)

    q_a = _partial_rope(aq.reshape(bsz, lp, A_HEADS, A_HEAD_DIM), rope_a, A_ROT)
    k_a = _partial_rope(ak.reshape(bsz, lp, A_KV_HEADS, A_HEAD_DIM), rope_a, A_ROT)
    v_a = av.reshape(bsz, lp, A_KV_HEADS, A_HEAD_DIM)
    q_i = _partial_rope(iq.reshape(bsz, lp, IDX_HEADS, IDX_DIM), rope_i, IDX_ROT)
    k_i = _partial_rope(ik[:, :, None, :], rope_i, IDX_ROT)[:, :, 0, :]
    w_i = iw * (IDX_HEADS * IDX_DIM) ** -0.5
    out_a = _dsa_attention(q_a, k_a, v_a, q_i, k_i, w_i, n_sel)

    q_m = (_rmsnorm(dq, q_norm) @ w_uq).reshape(bsz, lp, MLA_HEADS, MLA_NOPE + MLA_ROPE)
    q_m = jnp.concatenate([q_m[..., :MLA_NOPE], _rotate(q_m[..., MLA_NOPE:], rope_m)], axis=-1)
    kv = (_rmsnorm(dkv, kv_norm) @ w_ukv).reshape(bsz, lp, MLA_HEADS, MLA_NOPE + MLA_V)
    k_rope = jnp.broadcast_to(_rotate(kr[:, :, None, :], rope_m), (bsz, lp, MLA_HEADS, MLA_ROPE))
    k_m = jnp.concatenate([kv[..., :MLA_NOPE], k_rope], axis=-1)
    v_m = kv[..., MLA_NOPE:]
    out_b = _mla_attention(q_m, k_m, v_m)

    y = _s5(su, a_re, a_im, log_dt, b_re, b_im, c_re, c_im, d_skip)
    ga, gb = jnp.split(y @ w_glu, 2, axis=-1)
    out_c = ga * jax.nn.sigmoid(gb)

    gates = jax.nn.sigmoid((gl + b_gate).astype(jnp.float32)).astype(h.dtype)
    gates = gates.reshape(bsz, lp, N_BRANCH, D_MODEL)
    merged = (gates[:, :, 0] * (out_a @ wb_a) + gates[:, :, 1] * (out_b @ wb_b)
              + gates[:, :, 2] * (out_c @ wb_c))
    return merged @ w_o


def setup_inputs(seed: int = 0) -> dict:
    key = jax.random.key(seed)
    ks = iter(jax.random.split(key, 40))
    f32 = jnp.float32

    def nrm(shape, scale):
        return jax.random.normal(next(ks), shape, f32) * scale

    n_idx = jnp.arange(SSM_STATE, dtype=f32)
    return {
        'x': nrm((BATCH, SEQ, D_MODEL), 1.0),
        'meta': nrm((N_META, D_MODEL), 1.0),
        'ln_in_g': 1.0 + nrm((D_MODEL,), 0.02),
        'ln_in_b': nrm((D_MODEL,), 0.02),
        'w_in': nrm((DEPTH, D_MODEL, IN_WIDTH), D_MODEL ** -0.5),
        'b_gate': nrm((DEPTH, N_BRANCH * D_MODEL), 0.02),
        'mla_q_norm': 1.0 + nrm((DEPTH, MLA_Q_LORA), 0.02),
        'mla_kv_norm': 1.0 + nrm((DEPTH, MLA_KV_LORA), 0.02),
        'w_uq': nrm((DEPTH, MLA_Q_LORA, MLA_HEADS * (MLA_NOPE + MLA_ROPE)), MLA_Q_LORA ** -0.5),
        'w_ukv': nrm((DEPTH, MLA_KV_LORA, MLA_HEADS * (MLA_NOPE + MLA_V)), MLA_KV_LORA ** -0.5),
        'ssm_a_re': -0.5 + nrm((DEPTH, SSM_GROUPS, SSM_STATE), 0.01),
        'ssm_a_im': math.pi * n_idx + nrm((DEPTH, SSM_GROUPS, SSM_STATE), 0.01),
        'ssm_log_dt': jax.random.uniform(next(ks), (DEPTH, SSM_GROUPS), f32,
                                         math.log(1e-3), math.log(1e-1)),
        'ssm_b_re': nrm((DEPTH, SSM_GROUPS, SSM_STATE, SSM_GROUP), (2 * SSM_GROUP) ** -0.5),
        'ssm_b_im': nrm((DEPTH, SSM_GROUPS, SSM_STATE, SSM_GROUP), (2 * SSM_GROUP) ** -0.5),
        'ssm_c_re': nrm((DEPTH, SSM_GROUPS, SSM_GROUP, SSM_STATE), (2 * SSM_STATE) ** -0.5),
        'ssm_c_im': nrm((DEPTH, SSM_GROUPS, SSM_GROUP, SSM_STATE), (2 * SSM_STATE) ** -0.5),
        'ssm_d': nrm((DEPTH, SSM_WIDTH), 1.0),
        'w_glu': nrm((DEPTH, SSM_WIDTH, 2 * SSM_WIDTH), SSM_WIDTH ** -0.5),
        'w_branch_a': nrm((DEPTH, A_WIDTH, D_MODEL), A_WIDTH ** -0.5),
        'w_branch_b': nrm((DEPTH, MLA_WIDTH, D_MODEL), MLA_WIDTH ** -0.5),
        'w_branch_c': nrm((DEPTH, SSM_WIDTH, D_MODEL), SSM_WIDTH ** -0.5),
        'w_o': nrm((DEPTH, D_MODEL, D_MODEL), BETA * D_MODEL ** -0.5),
        'ln1_g': 1.0 + nrm((DEPTH, D_MODEL), 0.02),
        'ln1_b': nrm((DEPTH, D_MODEL), 0.02),
        'ffn_w1': nrm((N_DENSE, D_MODEL, D_FF), D_MODEL ** -0.5),
        'ffn_w3': nrm((N_DENSE, D_MODEL, D_FF), D_MODEL ** -0.5),
        'ffn_w2': nrm((N_DENSE, D_FF, D_MODEL), BETA * D_FF ** -0.5),
        'w_router': nrm((N_MOE, D_MODEL, N_EXPERTS), D_MODEL ** -0.5),
        'moe_w1': nrm((N_MOE, N_EXPERTS, D_MODEL, D_FF), D_MODEL ** -0.5),
        'moe_w3': nrm((N_MOE, N_EXPERTS, D_MODEL, D_FF), D_MODEL ** -0.5),
        'moe_w2': nrm((N_MOE, N_EXPERTS, D_FF, D_MODEL), BETA * D_FF ** -0.5),
        'ln2_g': 1.0 + nrm((DEPTH, D_MODEL), 0.02),
        'ln2_b': nrm((DEPTH, D_MODEL), 0.02),
    }


def reference(x, meta, ln_in_g, ln_in_b, w_in, b_gate, mla_q_norm, mla_kv_norm, w_uq, w_ukv,
              ssm_a_re, ssm_a_im, ssm_log_dt, ssm_b_re, ssm_b_im, ssm_c_re, ssm_c_im, ssm_d,
              w_glu, w_branch_a, w_branch_b, w_branch_c, w_o, ln1_g, ln1_b,
              ffn_w1, ffn_w3, ffn_w2, w_router, moe_w1, moe_w3, moe_w2, ln2_g, ln2_b):
    bsz, seq, _ = x.shape
    n_tok = seq + N_META
    lp = -(-n_tok // BLOCK) * BLOCK
    n_sel = min(TOPK_MAX, seq // 4)
    meta_b = jnp.broadcast_to(meta[None].astype(x.dtype), (bsz, N_META, D_MODEL))
    pad = jnp.zeros((bsz, lp - n_tok, D_MODEL), x.dtype)
    h = _layernorm(jnp.concatenate([meta_b, x, pad], axis=1), ln_in_g, ln_in_b)

    pos = jnp.arange(lp)
    rope_a = _rope_tables(pos, A_ROT)
    rope_i = _rope_tables(pos, IDX_ROT)
    rope_m = _rope_tables(pos, MLA_ROPE)

    for layer in range(DEPTH):
        mix = _mixer(h, rope_a, rope_i, rope_m, n_sel, w_in[layer], b_gate[layer],
                     mla_q_norm[layer], mla_kv_norm[layer], w_uq[layer], w_ukv[layer],
                     ssm_a_re[layer], ssm_a_im[layer], ssm_log_dt[layer], ssm_b_re[layer],
                     ssm_b_im[layer], ssm_c_re[layer], ssm_c_im[layer], ssm_d[layer],
                     w_glu[layer], w_branch_a[layer], w_branch_b[layer], w_branch_c[layer],
                     w_o[layer])
        h = _layernorm(ALPHA * h + mix, ln1_g[layer], ln1_b[layer])
        flat = h.reshape(bsz * lp, D_MODEL)
        i = layer // 2
        if layer % 2 == 0:
            f = _swiglu(flat, ffn_w1[i], ffn_w3[i], ffn_w2[i])
        else:
            f = _moe(flat, w_router[i], moe_w1[i], moe_w3[i], moe_w2[i])
        h = _layernorm(ALPHA * h + f.reshape(h.shape), ln2_g[layer], ln2_b[layer])

    return h[:, N_META:N_META + seq]
```

```python
import functools
import math

import jax
import jax.numpy as jnp
from jax import lax
from jax.experimental import pallas as pl
from jax.experimental.pallas import tpu as pltpu

F32 = jnp.float32
BF16 = jnp.bfloat16
I32 = jnp.int32

D_MODEL = 2048
DEPTH = 4
N_META = 16
BLOCK = 128
ROPE_THETA = 500000.0
LN_EPS = 1e-5
RMS_EPS = 1e-6
ALPHA = (2 * DEPTH) ** 0.25

A_HEADS = 8
A_KV_HEADS = 2
A_HEAD_DIM = 128
A_ROT = A_HEAD_DIM // 4
IDX_HEADS = 8
IDX_DIM = 64
IDX_ROT = IDX_DIM // 4
TOPK_MAX = 256

MLA_HEADS = 8
MLA_Q_LORA = 512
MLA_KV_LORA = 512
MLA_NOPE = 128
MLA_ROPE = 64
MLA_V = 128

SSM_WIDTH = 1024
SSM_GROUP = 16
SSM_GROUPS = SSM_WIDTH // SSM_GROUP
SSM_STATE = 64
SSM_CHUNK = 16
SSM_GROUPS_PER_STEP = 8

N_BRANCH = 3
A_WIDTH = A_HEADS * A_HEAD_DIM
MLA_WIDTH = MLA_HEADS * MLA_V
IN_SIZES = (A_HEADS * A_HEAD_DIM, A_KV_HEADS * A_HEAD_DIM, A_KV_HEADS * A_HEAD_DIM,
            IDX_HEADS * IDX_DIM, IDX_DIM, IDX_HEADS,
            MLA_Q_LORA, MLA_KV_LORA, MLA_ROPE,
            SSM_WIDTH, N_BRANCH * D_MODEL)

D_FF = 5632
N_EXPERTS = 8
TOP_K_EXPERTS = 2

LANES = 128
SUBLANES = 8
VMEM_LIMIT = 56 * 1024 * 1024
INT_MIN = -2 ** 31


def _pick(n, prefs):
    for p in prefs:
        if n % p == 0:
            return p
    raise ValueError(f"no tile in {prefs} divides {n}")


def _cparams(sem):
    return pltpu.CompilerParams(dimension_semantics=sem, vmem_limit_bytes=VMEM_LIMIT)


def _dot(a, b):
    return jnp.dot(a, b, preferred_element_type=F32)


def _dot_nt(a, b):
    return lax.dot_general(a, b, (((1,), (1,)), ((), ())), preferred_element_type=F32)


def _split_bf16(x):
    hi = x.astype(BF16)
    lo = (x - hi.astype(F32)).astype(BF16)
    return hi, lo


def _ln_math(x, g, b):
    mu = jnp.mean(x, axis=-1, keepdims=True)
    xc = x - mu
    var = jnp.mean(xc * xc, axis=-1, keepdims=True)
    return xc * lax.rsqrt(var + LN_EPS) * g + b


def _ln_kernel(x_ref, g_ref, b_ref, o_ref, ob_ref):
    y = _ln_math(x_ref[...], g_ref[...], b_ref[...])
    o_ref[...] = y
    ob_ref[...] = y.astype(BF16)


def _ln_res_kernel(h_ref, r_ref, g_ref, b_ref, o_ref, ob_ref):
    y = _ln_math(ALPHA * h_ref[...] + r_ref[...], g_ref[...], b_ref[...])
    o_ref[...] = y
    ob_ref[...] = y.astype(BF16)


def _moe_ln_kernel(h_ref, y0_ref, y1_ref, gt_ref, g_ref, b_ref, o_ref, ob_ref):
    gt = gt_ref[...]
    f = y0_ref[...] * gt[:, 0:1] + y1_ref[...] * gt[:, 1:2]
    y = _ln_math(ALPHA * h_ref[...] + f, g_ref[...], b_ref[...])
    o_ref[...] = y
    ob_ref[...] = y.astype(BF16)


def _layernorm(x, g, b, res=None):
    t, d = x.shape
    tm = _pick(t, (384, 256, 128))
    row = pl.BlockSpec((tm, d), lambda i: (i, 0))
    vec = pl.BlockSpec((1, d), lambda i: (0, 0))
    ins = [x] if res is None else [x, res]
    return pl.pallas_call(
        _ln_kernel if res is None else _ln_res_kernel,
        grid=(t // tm,),
        in_specs=[row] * len(ins) + [vec, vec],
        out_specs=[row, row],
        out_shape=[jax.ShapeDtypeStruct((t, d), F32), jax.ShapeDtypeStruct((t, d), BF16)],
        compiler_params=_cparams(("parallel",)),
        name="layernorm",
    )(*ins, g.reshape(1, d), b.reshape(1, d))


def _moe_combine_ln(h, y01, gates_t, g, b):
    t, d = h.shape
    tm = _pick(t, (384, 256, 128))
    nb = t // tm
    row = pl.BlockSpec((tm, d), lambda i: (i, 0))
    vec = pl.BlockSpec((1, d), lambda i: (0, 0))
    return pl.pallas_call(
        _moe_ln_kernel,
        grid=(nb,),
        in_specs=[row, row, pl.BlockSpec((tm, d), lambda i: (i + nb, 0)),
                  pl.BlockSpec((tm, TOP_K_EXPERTS), lambda i: (i, 0)), vec, vec],
        out_specs=[row, row],
        out_shape=[jax.ShapeDtypeStruct((t, d), F32), jax.ShapeDtypeStruct((t, d), BF16)],
        compiler_params=_cparams(("parallel",)),
        name="moe_combine_ln",
    )(h, y01, y01, gates_t, g.reshape(1, d), b.reshape(1, d))


def _rope_chunk(a, c, sa, sb, r):
    return a * c + pltpu.roll(a, LANES - r, 1) * sa + pltpu.roll(a, r, 1) * sb


def _proj_kernel(*refs, rope_r, has_bias, sigmoid, hi_prec, rms):
    it = iter(refs)
    x_ref = next(it)
    w_ref = next(it)
    n_ref = next(it) if rms else None
    tabs = (next(it), next(it), next(it)) if rope_r else None
    b_ref = next(it) if has_bias else None
    o_ref = next(it)
    x = x_ref[...]
    if rms:
        ms = jnp.mean(x * x, axis=-1, keepdims=True)
        x = x * lax.rsqrt(ms + RMS_EPS) * n_ref[...]
    if hi_prec:
        xh, xl = _split_bf16(x)
        wh, wl = _split_bf16(w_ref[...])
        acc = _dot(xh, wh) + _dot(xl, wh) + _dot(xh, wl)
    else:
        acc = _dot(x.astype(BF16), w_ref[...].astype(BF16))
    if has_bias:
        acc = acc + b_ref[...]
    if sigmoid:
        acc = jax.nn.sigmoid(acc)
    if rope_r:
        c_ref, sa_ref, sb_ref = tabs
        for c in range(acc.shape[1] // LANES):
            sl = slice(c * LANES, (c + 1) * LANES)
            o_ref[:, sl] = _rope_chunk(acc[:, sl], c_ref[:, sl], sa_ref[:, sl], sb_ref[:, sl],
                                       rope_r).astype(o_ref.dtype)
    else:
        o_ref[...] = acc.astype(o_ref.dtype)


def _proj(x, w, *, out_dtype, lp, x_col=0, kdim=None, rope=None, bias=None, sigmoid=False,
          hi_prec=False, rms_gain=None, tm_prefs=(1056, 768, 384, 128), tn_prefs=(512, 384, 256, 128),
          name="proj"):
    t = x.shape[0]
    kdim = x.shape[1] if kdim is None else kdim
    n = w.shape[1]
    tm = _pick(lp, tm_prefs)
    tn = _pick(n, tn_prefs)
    nrow = lp // tm
    in_specs = [pl.BlockSpec((tm, kdim), lambda i, j: (i, x_col)),
                pl.BlockSpec((kdim, tn), lambda i, j: (0, j))]
    ins = [x, w]
    if rms_gain is not None:
        in_specs.append(pl.BlockSpec((1, kdim), lambda i, j: (0, 0)))
        ins.append(rms_gain.reshape(1, kdim))
    if rope is not None:
        tab = pl.BlockSpec((tm, tn), lambda i, j: (i % nrow, j))
        in_specs += [tab, tab, tab]
        ins += list(rope[:3])
    if bias is not None:
        in_specs.append(pl.BlockSpec((1, tn), lambda i, j: (0, j)))
        ins.append(bias.reshape(1, n))
    kern = functools.partial(_proj_kernel, rope_r=rope[3] if rope is not None else 0,
                             has_bias=bias is not None, sigmoid=sigmoid, hi_prec=hi_prec,
                             rms=rms_gain is not None)
    return pl.pallas_call(
        kern,
        grid=(t // tm, n // tn),
        in_specs=in_specs,
        out_specs=pl.BlockSpec((tm, tn), lambda i, j: (i, j)),
        out_shape=jax.ShapeDtypeStruct((t, n), out_dtype),
        compiler_params=_cparams(("parallel", "arbitrary")),
        name=name,
    )(*ins)


def _mla_down_kernel(x_ref, w_ref, c_ref, sa_ref, sb_ref, o_ref, kr_ref):
    acc = _dot(x_ref[...], w_ref[...].astype(BF16))
    nq = o_ref.shape[1]
    o_ref[...] = acc[:, :nq]
    kr_ref[...] = _rope_chunk(acc[:, nq:], c_ref[...], sa_ref[...], sb_ref[...],
                              MLA_ROPE // 2).astype(BF16)


def _mla_down(hb, w, rope, lp):
    t, d = hb.shape
    n = w.shape[1]
    nq = MLA_Q_LORA + MLA_KV_LORA
    tm = _pick(lp, (704, 384, 128))
    nrow = lp // tm
    tab = pl.BlockSpec((tm, LANES), lambda i: (i % nrow, 0))
    return pl.pallas_call(
        _mla_down_kernel,
        grid=(t // tm,),
        in_specs=[pl.BlockSpec((tm, d), lambda i: (i, 0)), pl.BlockSpec((d, n), lambda i: (0, 0)),
                  tab, tab, tab],
        out_specs=[pl.BlockSpec((tm, nq), lambda i: (i, 0)), pl.BlockSpec((tm, LANES), lambda i: (i, 0))],
        out_shape=[jax.ShapeDtypeStruct((t, nq), F32), jax.ShapeDtypeStruct((t, LANES), BF16)],
        compiler_params=_cparams(("parallel",)),
        name="mla_down",
    )(hb, w, *rope[:3])


def _glu_kernel(y_ref, wa_ref, wb_ref, o_ref):
    y = y_ref[...]
    ga = _dot(y, wa_ref[...].astype(BF16))
    gb = _dot(y, wb_ref[...].astype(BF16))
    o_ref[...] = (ga * jax.nn.sigmoid(gb)).astype(o_ref.dtype)


def _glu(y, w_glu, lp):
    t, k = y.shape
    n = w_glu.shape[1] // 2
    tm = _pick(lp, (1056, 768, 384, 128))
    tn = _pick(n, (512, 256, 128))
    nj = n // tn
    return pl.pallas_call(
        _glu_kernel,
        grid=(t // tm, nj),
        in_specs=[pl.BlockSpec((tm, k), lambda i, j: (i, 0)),
                  pl.BlockSpec((k, tn), lambda i, j: (0, j)),
                  pl.BlockSpec((k, tn), lambda i, j: (0, j + nj))],
        out_specs=pl.BlockSpec((tm, tn), lambda i, j: (i, j)),
        out_shape=jax.ShapeDtypeStruct((t, n), BF16),
        compiler_params=_cparams(("parallel", "arbitrary")),
        name="glu",
    )(y, w_glu, w_glu)


def _merge_kernel(a_ref, b_ref, c_ref, wa_ref, wb_ref, wc_ref, g0_ref, g1_ref, g2_ref, o_ref):
    m = (g0_ref[...] * _dot(a_ref[...], wa_ref[...].astype(BF16))
         + g1_ref[...] * _dot(b_ref[...], wb_ref[...].astype(BF16))
         + g2_ref[...] * _dot(c_ref[...], wc_ref[...].astype(BF16)))
    o_ref[...] = m.astype(o_ref.dtype)


def _merge(out_a, out_b, out_c, wb_a, wb_b, wb_c, gates, lp):
    t = out_a.shape[0]
    n = wb_a.shape[1]
    tm = _pick(lp, (704, 384, 128))
    tn = _pick(n, (512, 256, 128))
    nj = n // tn

    def act(arr):
        return pl.BlockSpec((tm, arr.shape[1]), lambda i, j: (i, 0))

    def wgt(arr):
        return pl.BlockSpec((arr.shape[0], tn), lambda i, j: (0, j))

    def gate(br):
        return pl.BlockSpec((tm, tn), lambda i, j: (i, j + br * nj))

    return pl.pallas_call(
        _merge_kernel,
        grid=(t // tm, nj),
        in_specs=[act(out_a), act(out_b), act(out_c), wgt(wb_a), wgt(wb_b), wgt(wb_c),
                  gate(0), gate(1), gate(2)],
        out_specs=pl.BlockSpec((tm, tn), lambda i, j: (i, j)),
        out_shape=jax.ShapeDtypeStruct((t, n), BF16),
        compiler_params=_cparams(("parallel", "arbitrary")),
        name="merge",
    )(out_a, out_b, out_c, wb_a, wb_b, wb_c, gates, gates, gates)


def _softmax_update(s, v, m_ref, l_ref, acc_ref):
    tk = s.shape[1]
    m_prev = m_ref[...]
    m_new = jnp.maximum(m_prev, jnp.max(s, axis=1, keepdims=True))
    alpha = jnp.exp(m_prev - m_new)
    p = jnp.exp(s - jnp.concatenate([m_new] * (tk // LANES), axis=1))
    l_ref[...] = alpha * l_ref[...] + jnp.sum(p, axis=1, keepdims=True)
    acc_ref[...] = alpha * acc_ref[...] + _dot(p.astype(BF16), v)
    m_ref[...] = m_new


def _mla_attn_kernel(q_ref, kn_ref, kr_ref, v_ref, o_ref, m_ref, l_ref, acc_ref):
    qi = pl.program_id(2)
    ki = pl.program_id(3)
    tq, tk = q_ref.shape[0], kn_ref.shape[0]

    @pl.when(ki == 0)
    def _():
        m_ref[...] = jnp.full(m_ref.shape, -jnp.inf, F32)
        l_ref[...] = jnp.zeros(l_ref.shape, F32)
        acc_ref[...] = jnp.zeros(acc_ref.shape, F32)

    def scores():
        k = jnp.concatenate([kn_ref[...], kr_ref[...]], axis=1)
        return _dot_nt(q_ref[...], k)

    @pl.when(ki < qi)
    def _():
        _softmax_update(scores(), v_ref[...], m_ref, l_ref, acc_ref)

    @pl.when(ki == qi)
    def _():
        row = lax.broadcasted_iota(I32, (tq, tk), 0)
        col = lax.broadcasted_iota(I32, (tq, tk), 1)
        s = jnp.where(col <= row, scores(), -jnp.inf)
        _softmax_update(s, v_ref[...], m_ref, l_ref, acc_ref)

    @pl.when(ki == pl.num_programs(3) - 1)
    def _():
        o_ref[...] = (acc_ref[...] / l_ref[...]).astype(o_ref.dtype)


def _mla_attention(q, kv, kr, bsz, lp):
    t = q.shape[0]
    tq = _pick(lp, (384, 128))
    nq = lp // tq
    q3 = q.reshape(bsz, lp, q.shape[1])
    kv3 = kv.reshape(bsz, lp, kv.shape[1])
    kr3 = kr.reshape(bsz, lp, LANES)
    out = pl.pallas_call(
        _mla_attn_kernel,
        grid=(bsz, MLA_HEADS, nq, nq),
        in_specs=[
            pl.BlockSpec((None, tq, 2 * LANES), lambda b, h, i, j: (b, i, h)),
            pl.BlockSpec((None, tq, LANES), lambda b, h, i, j: (b, jnp.minimum(i, j), 2 * h)),
            pl.BlockSpec((None, tq, LANES), lambda b, h, i, j: (b, jnp.minimum(i, j), 0)),
            pl.BlockSpec((None, tq, LANES), lambda b, h, i, j: (b, jnp.minimum(i, j), 2 * h + 1)),
        ],
        out_specs=pl.BlockSpec((None, tq, LANES), lambda b, h, i, j: (b, i, h)),
        out_shape=jax.ShapeDtypeStruct((bsz, lp, MLA_WIDTH), BF16),
        scratch_shapes=[pltpu.VMEM((tq, LANES), F32), pltpu.VMEM((tq, LANES), F32),
                        pltpu.VMEM((tq, MLA_V), F32)],
        compiler_params=_cparams(("parallel", "parallel", "parallel", "arbitrary")),
        name="mla_attention",
    )(q3, kv3, kr3, kv3)
    return out.reshape(t, MLA_WIDTH)


def _dsa_attn_kernel(q_ref, k_ref, v_ref, bias_ref, o_ref, m_ref, l_ref, acc_ref):
    qi = pl.program_id(1)
    ki = pl.program_id(2)
    tk = k_ref.shape[0]
    rep = A_HEADS // A_KV_HEADS

    @pl.when(ki == 0)
    def _():
        m_ref[...] = jnp.full(m_ref.shape, -jnp.inf, F32)
        l_ref[...] = jnp.zeros(l_ref.shape, F32)
        acc_ref[...] = jnp.zeros(acc_ref.shape, F32)

    @pl.when(ki * tk < (qi + 1) * BLOCK)
    def _():
        bias = jnp.concatenate([bias_ref[j] for j in range(tk // BLOCK)], axis=1)
        bias = jnp.concatenate([bias] * rep, axis=0)
        for g in range(A_KV_HEADS):
            qg = jnp.concatenate(
                [q_ref[:, (g * rep + r) * A_HEAD_DIM:(g * rep + r + 1) * A_HEAD_DIM] for r in range(rep)],
                axis=0)
            s = _dot_nt(qg, k_ref[:, g * A_HEAD_DIM:(g + 1) * A_HEAD_DIM]) + bias
            _softmax_update(s, v_ref[:, g * A_HEAD_DIM:(g + 1) * A_HEAD_DIM],
                            m_ref.at[g], l_ref.at[g], acc_ref.at[g])

    @pl.when(ki == pl.num_programs(2) - 1)
    def _():
        for g in range(A_KV_HEADS):
            o = acc_ref[g] / l_ref[g]
            for r in range(rep):
                h = g * rep + r
                o_ref[:, h * A_HEAD_DIM:(h + 1) * A_HEAD_DIM] = (
                    o[r * BLOCK:(r + 1) * BLOCK]).astype(o_ref.dtype)


def _dsa_attention(q, k, v, bias, bsz, lp):
    t = q.shape[0]
    nb = lp // BLOCK
    tk = _pick(lp, (384, 128))
    nkt = tk // BLOCK
    q3 = q.reshape(bsz, lp, A_WIDTH)
    k3 = k.reshape(bsz, lp, A_KV_HEADS * A_HEAD_DIM)
    v3 = v.reshape(bsz, lp, A_KV_HEADS * A_HEAD_DIM)

    def kmap(b, i, j):
        return (b, jnp.minimum(j, (i * BLOCK) // tk), 0)

    rows = (A_HEADS // A_KV_HEADS) * BLOCK
    out = pl.pallas_call(
        _dsa_attn_kernel,
        grid=(bsz, nb, lp // tk),
        in_specs=[
            pl.BlockSpec((None, BLOCK, A_WIDTH), lambda b, i, j: (b, i, 0)),
            pl.BlockSpec((None, tk, A_KV_HEADS * A_HEAD_DIM), kmap),
            pl.BlockSpec((None, tk, A_KV_HEADS * A_HEAD_DIM), kmap),
            pl.BlockSpec((None, None, nkt, BLOCK, BLOCK),
                         lambda b, i, j: (b, i, jnp.minimum(j, (i * BLOCK) // tk), 0, 0)),
        ],
        out_specs=pl.BlockSpec((None, BLOCK, A_WIDTH), lambda b, i, j: (b, i, 0)),
        out_shape=jax.ShapeDtypeStruct((bsz, lp, A_WIDTH), BF16),
        scratch_shapes=[pltpu.VMEM((A_KV_HEADS, rows, LANES), F32),
                        pltpu.VMEM((A_KV_HEADS, rows, LANES), F32),
                        pltpu.VMEM((A_KV_HEADS, rows, A_HEAD_DIM), F32)],
        compiler_params=_cparams(("parallel", "parallel", "arbitrary")),
        name="dsa_attention",
    )(q3, k3, v3, bias)
    return out.reshape(t, A_WIDTH)


def _indexer_kernel(iq_ref, wq_ref, kw_ref, o_ref, kcat_ref, qcat_ref, wb_ref, key_ref, *, n_sel):
    qb = pl.program_id(1)
    nb = o_ref.shape[0]
    half = IDX_DIM
    lane = lax.broadcasted_iota(I32, (BLOCK, LANES), 1)
    row = lax.broadcasted_iota(I32, (BLOCK, LANES), 0)

    @pl.when(qb == 0)
    def _():
        kf = kw_ref[...]
        klane = lax.broadcasted_iota(I32, kf.shape, 1)
        kz = jnp.where(klane < half, kf, 0.0)
        hi = kz.astype(BF16).astype(F32)
        lo = kz - hi
        kcat_ref[...] = jnp.concatenate(
            [(hi + pltpu.roll(hi, half, 1)).astype(BF16), lo.astype(BF16)], axis=1)

    wq = wq_ref[...]
    for h in range(IDX_HEADS):
        chunk = iq_ref[:, (h // 2) * LANES:(h // 2 + 1) * LANES]
        if h % 2 == 0:
            a = jnp.where(lane < half, chunk, 0.0)
        else:
            a = pltpu.roll(jnp.where(lane >= half, chunk, 0.0), half, 1)
        hi = a.astype(BF16).astype(F32)
        lo = a - hi
        qcat_ref[h * BLOCK:(h + 1) * BLOCK, :] = jnp.concatenate(
            [(hi + pltpu.roll(lo, half, 1)).astype(BF16), hi.astype(BF16)], axis=1)
        wb_ref[h] = jnp.broadcast_to(wq[:, half + h:half + h + 1], (BLOCK, LANES))

    qpos = qb * BLOCK + row

    def sortable(x):
        b = pltpu.bitcast(x, I32)
        return b ^ ((b >> 31) & 0x7FFFFFFF)

    def score_tile(kt, carry):
        kblk = kcat_ref[pl.ds(pl.multiple_of(kt * BLOCK, BLOCK), BLOCK), :]
        s = _dot_nt(qcat_ref[...], kblk)
        sc = jnp.zeros((BLOCK, LANES), F32)
        for h in range(IDX_HEADS):
            sc = sc + jnp.maximum(s[h * BLOCK:(h + 1) * BLOCK], 0.0) * wb_ref[h]
        kpos = kt * BLOCK + lane
        sc = jnp.where(kpos < N_META, jnp.inf, sc)
        sc = jnp.where(kpos <= qpos, sc, -jnp.inf)
        key_ref[kt] = sortable(sc)
        return carry

    lax.fori_loop(0, qb + 1, score_tile, 0)

    def count_ge(cand):
        def body(kt, c):
            return c + jnp.where(key_ref[kt] >= cand, 1.0, 0.0)
        c = lax.fori_loop(0, qb + 1, body, jnp.zeros((BLOCK, LANES), F32))
        return jnp.sum(c, axis=1, keepdims=True)

    def bit_step(i, thr):
        bit = 31 - i
        cand = jnp.where(bit == 31, thr ^ INT_MIN, thr | (1 << jnp.minimum(bit, 30)))
        return jnp.where(count_ge(cand) >= n_sel, cand, thr)

    thr = lax.fori_loop(0, 32, bit_step, jnp.full((BLOCK, LANES), INT_MIN, I32))

    def count_gt(kt, c):
        return c + jnp.where(key_ref[kt] > thr, 1.0, 0.0)

    n_gt = jnp.sum(lax.fori_loop(0, qb + 1, count_gt, jnp.zeros((BLOCK, LANES), F32)),
                   axis=1, keepdims=True)
    need = n_sel - n_gt
    tri = (lax.broadcasted_iota(I32, (LANES, LANES), 0)
           <= lax.broadcasted_iota(I32, (LANES, LANES), 1)).astype(BF16)

    def emit(kt, taken):
        key = key_ref[kt]
        eq = key == thr
        rank = _dot(jnp.where(eq, 1.0, 0.0).astype(BF16), tri)
        sel = (key > thr) | (eq & (taken + rank <= need))
        kpos = kt * BLOCK + lane
        o_ref[kt] = jnp.where(sel & (kpos <= qpos), 0.0, -jnp.inf)
        return taken + rank[:, LANES - 1:LANES]

    lax.fori_loop(0, qb + 1, emit, jnp.zeros((BLOCK, 1), F32))

    def fill(kt, carry):
        o_ref[kt] = jnp.full((BLOCK, LANES), -jnp.inf, F32)
        return carry

    lax.fori_loop(qb + 1, nb, fill, 0)


def _indexer(idx, bsz, lp, n_sel):
    nb = lp // BLOCK
    idx3 = idx.reshape(bsz, lp, idx.shape[1])
    nq = IDX_HEADS * IDX_DIM
    kw_col = nq // LANES
    return pl.pallas_call(
        functools.partial(_indexer_kernel, n_sel=n_sel),
        grid=(bsz, nb),
        in_specs=[
            pl.BlockSpec((None, BLOCK, nq), lambda b, i: (b, i, 0)),
            pl.BlockSpec((None, BLOCK, LANES), lambda b, i: (b, i, kw_col)),
            pl.BlockSpec((None, lp, LANES), lambda b, i: (b, 0, kw_col)),
        ],
        out_specs=pl.BlockSpec((None, None, nb, BLOCK, BLOCK), lambda b, i: (b, i, 0, 0, 0)),
        out_shape=jax.ShapeDtypeStruct((bsz, nb, nb, BLOCK, BLOCK), F32),
        scratch_shapes=[pltpu.VMEM((lp, 2 * LANES), BF16),
                        pltpu.VMEM((IDX_HEADS * BLOCK, 2 * LANES), BF16),
                        pltpu.VMEM((IDX_HEADS, BLOCK, LANES), F32),
                        pltpu.VMEM((nb, BLOCK, LANES), I32)],
        compiler_params=_cparams(("parallel", "arbitrary")),
        name="indexer",
    )(idx3, idx3, idx3)


def _s5_kernel(u_ref, mw_ref, vre_ref, vim_ref, are_ref, aim_ref, o_ref,
               y_ref, sre_ref, sim_ref, pre_ref, pim_ref, *, bsz):
    ng = u_ref.shape[0]
    rows = u_ref.shape[1]
    p = SSM_STATE
    nm = SSM_CHUNK * SSM_GROUP
    for g in range(ng):
        r = _dot(u_ref[g], mw_ref[g])
        y_ref[g] = r[:, :nm]
        sre_ref[:, g * p:(g + 1) * p] = r[:, nm:nm + p]
        sim_ref[:, g * p:(g + 1) * p] = r[:, nm + p:nm + 2 * p]
    ar = are_ref[...]
    ai = aim_ref[...]

    sub = SUBLANES // bsz

    def step(c, carry):
        cr, ci = carry
        rs = pl.ds(pl.multiple_of(c * SUBLANES, SUBLANES), SUBLANES)
        lr = sre_ref[rs, :]
        li = sim_ref[rs, :]
        before_r, before_i = [], []
        for k in range(sub):
            before_r.append(cr)
            before_i.append(ci)
            cr, ci = (ar * cr - ai * ci + lr[k * bsz:(k + 1) * bsz],
                      ar * ci + ai * cr + li[k * bsz:(k + 1) * bsz])
        pre_ref[rs, :] = jnp.concatenate(before_r, axis=0)
        pim_ref[rs, :] = jnp.concatenate(before_i, axis=0)
        return cr, ci

    zero = jnp.zeros((bsz, ng * p), F32)
    lax.fori_loop(0, rows // SUBLANES, step, (zero, zero))
    for g in range(ng):
        y = (y_ref[g] + _dot(pre_ref[:, g * p:(g + 1) * p].astype(BF16), vre_ref[g])
             + _dot(pim_ref[:, g * p:(g + 1) * p].astype(BF16), vim_ref[g]))
        o_ref[g] = y.astype(o_ref.dtype)


def _s5_weights(a_re, a_im, log_dt, b_re, b_im, c_re, c_im, d_skip):
    hp = lax.Precision.HIGHEST
    g, p, c, tc = SSM_GROUPS, SSM_STATE, SSM_GROUP, SSM_CHUNK
    dt = jnp.exp(log_dt)[:, None]
    lam_re, lam_im = dt * a_re, dt * a_im
    mag = jnp.exp(lam_re)
    ab_re, ab_im = mag * jnp.cos(lam_im), mag * jnp.sin(lam_im)
    den = a_re * a_re + a_im * a_im
    f_re = ((ab_re - 1.0) * a_re + ab_im * a_im) / den
    f_im = (ab_im * a_re - (ab_re - 1.0) * a_im) / den
    bb_re = f_re[..., None] * b_re - f_im[..., None] * b_im
    bb_im = f_re[..., None] * b_im + f_im[..., None] * b_re
    d = jnp.arange(tc + 1, dtype=F32)[:, None, None]
    pmag = jnp.exp(d * lam_re)
    pw_re, pw_im = pmag * jnp.cos(d * lam_im), pmag * jnp.sin(d * lam_im)
    z_re = pw_re[:tc, :, :, None] * bb_re - pw_im[:tc, :, :, None] * bb_im
    z_im = pw_re[:tc, :, :, None] * bb_im + pw_im[:tc, :, :, None] * bb_re
    kmat = (jnp.einsum('gop,dgpi->gdio', c_re, z_re, precision=hp)
            - jnp.einsum('gop,dgpi->gdio', c_im, z_im, precision=hp))
    ti = jnp.arange(tc)
    lag = ti[None, :] - ti[:, None]
    m5 = kmat[:, jnp.clip(lag, 0, tc - 1)]
    m5 = jnp.where((lag >= 0)[None, :, :, None, None], m5, 0.0)
    m = jnp.transpose(m5, (0, 1, 3, 2, 4)).reshape(g, tc * c, tc * c)
    m = m + jnp.eye(tc * c, dtype=F32) * jnp.tile(d_skip.reshape(g, 1, c), (1, tc, 1)).reshape(g, 1, tc * c)
    w_re = jnp.transpose(z_re[::-1], (1, 0, 3, 2)).reshape(g, tc * c, p)
    w_im = jnp.transpose(z_im[::-1], (1, 0, 3, 2)).reshape(g, tc * c, p)
    mw = jnp.concatenate([m, w_re, w_im], axis=2).astype(BF16)
    q_re, q_im = pw_re[1:], pw_im[1:]
    v_re = c_re[None] * q_re[:, :, None, :] - c_im[None] * q_im[:, :, None, :]
    v_im = c_re[None] * q_im[:, :, None, :] + c_im[None] * q_re[:, :, None, :]
    v_re = jnp.transpose(v_re, (1, 3, 0, 2)).reshape(g, p, tc * c).astype(BF16)
    v_im = jnp.transpose(-v_im, (1, 3, 0, 2)).reshape(g, p, tc * c).astype(BF16)
    return mw, v_re, v_im, pw_re[tc].reshape(1, g * p), pw_im[tc].reshape(1, g * p)


def _s5(su, weights, bsz, lp):
    mw, v_re, v_im, a_re, a_im = weights
    g, c, tc, p = SSM_GROUPS, SSM_GROUP, SSM_CHUNK, SSM_STATE
    nc = lp // tc
    rows = nc * bsz
    ng = SSM_GROUPS_PER_STEP
    u = jnp.transpose(su.reshape(bsz, nc, tc, g, c), (3, 1, 0, 2, 4)).reshape(g, rows, tc * c)
    y = pl.pallas_call(
        functools.partial(_s5_kernel, bsz=bsz),
        grid=(g // ng,),
        in_specs=[
            pl.BlockSpec((ng, rows, tc * c), lambda i: (i, 0, 0)),
            pl.BlockSpec((ng, tc * c, tc * c + 2 * p), lambda i: (i, 0, 0)),
            pl.BlockSpec((ng, p, tc * c), lambda i: (i, 0, 0)),
            pl.BlockSpec((ng, p, tc * c), lambda i: (i, 0, 0)),
            pl.BlockSpec((1, ng * p), lambda i: (0, i)),
            pl.BlockSpec((1, ng * p), lambda i: (0, i)),
        ],
        out_specs=pl.BlockSpec((ng, rows, tc * c), lambda i: (i, 0, 0)),
        out_shape=jax.ShapeDtypeStruct((g, rows, tc * c), BF16),
        scratch_shapes=[pltpu.VMEM((ng, rows, tc * c), F32),
                        pltpu.VMEM((rows, ng * p), F32), pltpu.VMEM((rows, ng * p), F32),
                        pltpu.VMEM((rows, ng * p), F32), pltpu.VMEM((rows, ng * p), F32)],
        compiler_params=_cparams(("parallel",)),
        name="s5",
    )(u, mw, v_re, v_im, a_re, a_im)
    return jnp.transpose(y.reshape(g, nc, bsz, tc, c), (2, 1, 3, 0, 4)).reshape(bsz * lp, g * c)


def _ffn_kernel(be_ref, nu_ref, x_ref, w1_ref, w3_ref, w2_ref, o_ref):
    i = pl.program_id(0)
    f = pl.program_id(1)

    @pl.when(f == 0)
    def _():
        o_ref[...] = jnp.zeros(o_ref.shape, F32)

    @pl.when(i < nu_ref[0])
    def _():
        x = x_ref[...].astype(BF16)
        a = _dot(x, w1_ref[0].astype(BF16))
        b = _dot(x, w3_ref[0].astype(BF16))
        act = (a * jax.nn.sigmoid(a) * b).astype(BF16)
        o_ref[...] += _dot(act, w2_ref[0].astype(BF16))


def _ffn(x, w1, w3, w2, blk_expert, n_used, tm):
    n, d = x.shape
    dff = w1.shape[2]
    tf = _pick(dff, (256, 128))
    nf = dff // tf

    def live(i, nu):
        return jnp.minimum(i, nu[0] - 1)

    def fidx(i, f, nu):
        return jnp.where(i < nu[0], f, nf - 1)

    grid_spec = pltpu.PrefetchScalarGridSpec(
        num_scalar_prefetch=2,
        grid=(n // tm, nf),
        in_specs=[
            pl.BlockSpec((tm, d), lambda i, f, be, nu: (live(i, nu), 0)),
            pl.BlockSpec((1, d, tf), lambda i, f, be, nu: (be[live(i, nu)], 0, fidx(i, f, nu))),
            pl.BlockSpec((1, d, tf), lambda i, f, be, nu: (be[live(i, nu)], 0, fidx(i, f, nu))),
            pl.BlockSpec((1, tf, d), lambda i, f, be, nu: (be[live(i, nu)], fidx(i, f, nu), 0)),
        ],
        out_specs=pl.BlockSpec((tm, d), lambda i, f, be, nu: (i, 0)),
    )
    return pl.pallas_call(
        _ffn_kernel,
        grid_spec=grid_spec,
        out_shape=jax.ShapeDtypeStruct((n, d), F32),
        compiler_params=_cparams(("arbitrary", "arbitrary")),
        name="swiglu",
    )(blk_expert, n_used, x, w1, w3, w2)


def _router_kernel(h_ref, w_ref, idx_ref, gate_ref):
    hh, hl = _split_bf16(h_ref[...])
    wh, wl = _split_bf16(w_ref[...])
    logits = _dot_nt(wh, hh) + _dot_nt(wl, hh) + _dot_nt(wh, hl)
    e = lax.broadcasted_iota(I32, logits.shape, 0).astype(F32)
    m1 = jnp.max(logits, axis=0, keepdims=True)
    i1 = jnp.min(jnp.where(logits == m1, e, float(N_EXPERTS)), axis=0, keepdims=True)
    rest = jnp.where(e == i1, -jnp.inf, logits)
    m2 = jnp.max(rest, axis=0, keepdims=True)
    i2 = jnp.min(jnp.where(rest == m2, e, float(N_EXPERTS)), axis=0, keepdims=True)
    e2 = jnp.exp(m2 - m1)
    den = 1.0 + e2
    idx_ref[...] = jnp.concatenate([i1, i2], axis=0).astype(I32)
    gate_ref[...] = jnp.concatenate([1.0 / den, e2 / den], axis=0)


def _router(h, w_router_t):
    t, d = h.shape
    tm = _pick(t, (384, 256, 128))
    return pl.pallas_call(
        _router_kernel,
        grid=(t // tm,),
        in_specs=[pl.BlockSpec((tm, d), lambda i: (i, 0)), pl.BlockSpec((N_EXPERTS, d), lambda i: (0, 0))],
        out_specs=[pl.BlockSpec((TOP_K_EXPERTS, tm), lambda i: (0, i)),
                   pl.BlockSpec((TOP_K_EXPERTS, tm), lambda i: (0, i))],
        out_shape=[jax.ShapeDtypeStruct((TOP_K_EXPERTS, t), I32),
                   jax.ShapeDtypeStruct((TOP_K_EXPERTS, t), F32)],
        compiler_params=_cparams(("parallel",)),
        name="router",
    )(h, w_router_t)


def _gather_kernel(idx_ref, src_ref, o_ref, sem, *, tg):
    base = pl.program_id(0) * tg

    def row_copy(src_row, r):
        return pltpu.make_async_copy(src_ref.at[pl.ds(src_row, 1), :], o_ref.at[pl.ds(r, 1), :], sem)

    def issue(r, carry):
        row_copy(idx_ref[base + r], r).start()
        return carry

    lax.fori_loop(0, tg, issue, 0)

    def wait(r, carry):
        row_copy(0, r).wait()
        return carry

    lax.fori_loop(0, tg, wait, 0)


def _gather_rows(src, idx, tg):
    n = idx.shape[0]
    d = src.shape[1]
    grid_spec = pltpu.PrefetchScalarGridSpec(
        num_scalar_prefetch=1,
        grid=(n // tg,),
        in_specs=[pl.BlockSpec(memory_space=pl.ANY)],
        out_specs=pl.BlockSpec((tg, d), lambda i, idx_ref: (i, 0)),
        scratch_shapes=[pltpu.SemaphoreType.DMA(())],
    )
    return pl.pallas_call(
        functools.partial(_gather_kernel, tg=tg),
        grid_spec=grid_spec,
        out_shape=jax.ShapeDtypeStruct((n, d), src.dtype),
        compiler_params=_cparams(("arbitrary",)),
        name="gather_rows",
    )(idx, src)


def _moe(h, hb, w_router, w1, w3, w2, ln_g, ln_b):
    t, d = h.shape
    tm = 512
    top_idx, gates = _router(h, w_router.T)
    n_assign = t * TOP_K_EXPERTS
    expert = top_idx.reshape(-1)
    onehot = (expert[:, None] == jnp.arange(N_EXPERTS)[None, :]).astype(I32)
    rank = jnp.sum((jnp.cumsum(onehot, axis=0) - onehot) * onehot, axis=1)
    counts = jnp.sum(onehot, axis=0)
    padded = (counts + tm - 1) // tm * tm
    pad_end = jnp.cumsum(padded)
    dest = (pad_end - padded)[expert] + rank
    n_blk = -(-(n_assign + N_EXPERTS * (tm - 1)) // tm)
    n_rows = n_blk * tm
    token = jnp.arange(n_assign, dtype=I32) % t
    row_token = jnp.zeros((n_rows,), I32).at[dest].set(token)
    blk_expert = jnp.minimum(
        jnp.searchsorted(pad_end, jnp.arange(n_blk) * tm, side='right'), N_EXPERTS - 1).astype(I32)
    n_used = (pad_end[-1:] // tm).astype(I32)
    xs = _gather_rows(h, row_token, 256)
    ys = _ffn(xs, w1, w3, w2, blk_expert, n_used, tm)
    y01 = _gather_rows(ys, dest.astype(I32), 256)
    del hb
    return _moe_combine_ln(h, y01, gates.T, ln_g, ln_b)


def _rope_tables(lp, n_heads, head_dim, off, rot_dim, scale=1.0):
    r = rot_dim // 2
    pos = jnp.arange(lp)
    inv = ROPE_THETA ** (-jnp.arange(0, rot_dim, 2, dtype=F32) / rot_dim)
    ang = pos.astype(F32)[:, None] * inv[None, :]
    cos, sin = jnp.cos(ang), jnp.sin(ang)
    c = jnp.ones((lp, head_dim), F32).at[:, off:off + r].set(cos).at[:, off + r:off + 2 * r].set(cos)
    sa = jnp.zeros((lp, head_dim), F32).at[:, off:off + r].set(-sin)
    sb = jnp.zeros((lp, head_dim), F32).at[:, off + r:off + 2 * r].set(sin)
    return tuple(jnp.tile(x * scale, (1, n_heads)) for x in (c, sa, sb)) + (r,)


def _pad_cols(w, n):
    return jnp.pad(w, ((0, 0), (0, n - w.shape[1])))


def kernel(x, meta, ln_in_g, ln_in_b, w_in, b_gate, mla_q_norm, mla_kv_norm, w_uq, w_ukv, ssm_a_re, ssm_a_im, ssm_log_dt, ssm_b_re, ssm_b_im, ssm_c_re, ssm_c_im, ssm_d, w_glu, w_branch_a, w_branch_b, w_branch_c, w_o, ln1_g, ln1_b, ffn_w1, ffn_w3, ffn_w2, w_router, moe_w1, moe_w3, moe_w2, ln2_g, ln2_b):
    bsz, seq, _ = x.shape
    n_tok = seq + N_META
    lp = -(-n_tok // BLOCK) * BLOCK
    t = bsz * lp
    n_sel = min(TOPK_MAX, seq // 4)
    meta_b = jnp.broadcast_to(meta[None].astype(x.dtype), (bsz, N_META, D_MODEL))
    pad = jnp.zeros((bsz, lp - n_tok, D_MODEL), x.dtype)
    h, hb = _layernorm(jnp.concatenate([meta_b, x, pad], axis=1).reshape(t, D_MODEL), ln_in_g, ln_in_b)

    a_scale = A_HEAD_DIM ** -0.5
    rope_q = _rope_tables(lp, A_HEADS, A_HEAD_DIM, 0, A_ROT, a_scale)
    rope_k = _rope_tables(lp, A_KV_HEADS, A_HEAD_DIM, 0, A_ROT)
    iq_tab = _rope_tables(lp, IDX_HEADS, IDX_DIM, 0, IDX_ROT)
    ik_tab = _rope_tables(lp, 1, LANES, 0, IDX_ROT)
    w_scale = jnp.ones((LANES,), F32).at[IDX_DIM:IDX_DIM + IDX_HEADS].set((IDX_HEADS * IDX_DIM) ** -0.5)
    rope_i = tuple(jnp.concatenate([a, b * w_scale[None, :]], axis=1)
                   for a, b in zip(iq_tab[:3], ik_tab[:3])) + (IDX_ROT // 2,)
    rope_kr = _rope_tables(lp, 1, LANES, 0, MLA_ROPE)
    m_scale = (MLA_NOPE + MLA_ROPE) ** -0.5
    rope_mq = _rope_tables(lp, MLA_HEADS, 2 * LANES, MLA_NOPE, MLA_ROPE, m_scale)

    offs = [0]
    for s in IN_SIZES:
        offs.append(offs[-1] + s)
    o_aq, o_ak, o_av, o_iq, o_ik, o_iw, o_dq, o_dkv, o_kr, o_su, o_gl, o_end = offs

    for layer in range(DEPTH):
        wl = w_in[layer]
        w_aq = wl[:, o_aq:o_ak]
        w_ak = wl[:, o_ak:o_av]
        w_av = wl[:, o_av:o_iq]
        w_idx = _pad_cols(wl[:, o_iq:o_dq], IDX_HEADS * IDX_DIM + LANES)
        w_md = _pad_cols(wl[:, o_dq:o_su], MLA_Q_LORA + MLA_KV_LORA + LANES)
        w_su = wl[:, o_su:o_gl]
        w_gl = wl[:, o_gl:o_end]

        q_a = _proj(hb, w_aq, out_dtype=BF16, lp=lp, rope=rope_q, name="proj_aq")
        k_a = _proj(hb, w_ak, out_dtype=BF16, lp=lp, rope=rope_k, name="proj_ak")
        v_a = _proj(hb, w_av, out_dtype=BF16, lp=lp, name="proj_av")
        idx = _proj(h, w_idx, out_dtype=F32, lp=lp, rope=rope_i, hi_prec=True,
                    tm_prefs=(384, 128), tn_prefs=(640,), name="proj_idx")
        bias = _indexer(idx, bsz, lp, n_sel)
        out_a = _dsa_attention(q_a, k_a, v_a, bias, bsz, lp)

        dqkv, kr = _mla_down(hb, w_md, rope_kr, lp)
        wq = w_uq[layer].reshape(MLA_Q_LORA, MLA_HEADS, MLA_NOPE + MLA_ROPE)
        wq = jnp.pad(wq, ((0, 0), (0, 0), (0, 2 * LANES - MLA_NOPE - MLA_ROPE))).reshape(MLA_Q_LORA, -1)
        q_m = _proj(dqkv, wq, out_dtype=BF16, lp=lp, x_col=0, kdim=MLA_Q_LORA, rope=rope_mq,
                    rms_gain=mla_q_norm[layer], name="proj_mq")
        kv_m = _proj(dqkv, w_ukv[layer], out_dtype=BF16, lp=lp, x_col=1, kdim=MLA_KV_LORA,
                     rms_gain=mla_kv_norm[layer], name="proj_mkv")
        out_b = _mla_attention(q_m, kv_m, kr, bsz, lp)

        su = _proj(hb, w_su, out_dtype=BF16, lp=lp, name="proj_su")
        s5w = _s5_weights(ssm_a_re[layer], ssm_a_im[layer], ssm_log_dt[layer], ssm_b_re[layer],
                          ssm_b_im[layer], ssm_c_re[layer], ssm_c_im[layer], ssm_d[layer])
        y = _s5(su, s5w, bsz, lp)
        out_c = _glu(y, w_glu[layer], lp)

        gates = _proj(hb, w_gl, out_dtype=F32, lp=lp, bias=b_gate[layer], sigmoid=True, name="proj_gates")
        merged = _merge(out_a, out_b, out_c, w_branch_a[layer], w_branch_b[layer], w_branch_c[layer],
                        gates, lp)
        mix = _proj(merged, w_o[layer], out_dtype=F32, lp=lp, name="proj_o")
        h, hb = _layernorm(h, ln1_g[layer], ln1_b[layer], res=mix)

        i = layer // 2
        if layer % 2 == 0:
            tm = _pick(t, (704, 384, 128))
            nblk = t // tm
            f = _ffn(hb, ffn_w1[i][None], ffn_w3[i][None], ffn_w2[i][None],
                     jnp.zeros((nblk,), I32), jnp.full((1,), nblk, I32), tm)
            h, hb = _layernorm(h, ln2_g[layer], ln2_b[layer], res=f)
        else:
            h, hb = _moe(h, hb, w_router[i], moe_w1[i], moe_w3[i], moe_w2[i],
                         ln2_g[layer], ln2_b[layer])

    return h.reshape(bsz, lp, D_MODEL)[:, N_META:N_META + seq]
```

```python
import functools
import math

import jax
import jax.numpy as jnp
from jax import lax
from jax.experimental import pallas as pl
from jax.experimental.pallas import tpu as pltpu

F32 = jnp.float32
BF16 = jnp.bfloat16
I32 = jnp.int32

D_MODEL = 2048
DEPTH = 4
N_META = 16
BLOCK = 128
ROPE_THETA = 500000.0
LN_EPS = 1e-5
RMS_EPS = 1e-6
ALPHA = (2 * DEPTH) ** 0.25

A_HEADS = 8
A_KV_HEADS = 2
A_HEAD_DIM = 128
A_ROT = A_HEAD_DIM // 4
IDX_HEADS = 8
IDX_DIM = 64
IDX_ROT = IDX_DIM // 4
TOPK_MAX = 256

MLA_HEADS = 8
MLA_Q_LORA = 512
MLA_KV_LORA = 512
MLA_NOPE = 128
MLA_ROPE = 64
MLA_V = 128

SSM_WIDTH = 1024
SSM_GROUP = 16
SSM_GROUPS = SSM_WIDTH // SSM_GROUP
SSM_STATE = 64
SSM_CHUNK = 16
SSM_GROUPS_PER_STEP = 8

N_BRANCH = 3
A_WIDTH = A_HEADS * A_HEAD_DIM
MLA_WIDTH = MLA_HEADS * MLA_V
IN_SIZES = (A_HEADS * A_HEAD_DIM, A_KV_HEADS * A_HEAD_DIM, A_KV_HEADS * A_HEAD_DIM,
            IDX_HEADS * IDX_DIM, IDX_DIM, IDX_HEADS,
            MLA_Q_LORA, MLA_KV_LORA, MLA_ROPE,
            SSM_WIDTH, N_BRANCH * D_MODEL)

D_FF = 5632
N_EXPERTS = 8
TOP_K_EXPERTS = 2

LANES = 128
SUBLANES = 8
VMEM_LIMIT = 56 * 1024 * 1024
INT_MIN = -2 ** 31
LOG2E = math.log2(math.e)


def _pick(n, prefs):
    for p in prefs:
        if n % p == 0:
            return p
    raise ValueError(f"no tile in {prefs} divides {n}")


def _cparams(sem):
    return pltpu.CompilerParams(dimension_semantics=sem, vmem_limit_bytes=VMEM_LIMIT)


def _dot(a, b):
    return jnp.dot(a, b, preferred_element_type=F32)


def _dot_nt(a, b):
    return lax.dot_general(a, b, (((1,), (1,)), ((), ())), preferred_element_type=F32)


def _split_bf16(x):
    hi = x.astype(BF16)
    lo = (x - hi.astype(F32)).astype(BF16)
    return hi, lo


def _ln_math(x, g, b):
    mu = jnp.mean(x, axis=-1, keepdims=True)
    xc = x - mu
    var = jnp.mean(xc * xc, axis=-1, keepdims=True)
    return xc * lax.rsqrt(var + LN_EPS) * g + b


def _ln_kernel(x_ref, g_ref, b_ref, o_ref, ob_ref):
    y = _ln_math(x_ref[...], g_ref[...], b_ref[...])
    o_ref[...] = y
    ob_ref[...] = y.astype(BF16)


def _ln_res_kernel(h_ref, r_ref, g_ref, b_ref, o_ref, ob_ref):
    y = _ln_math(ALPHA * h_ref[...] + r_ref[...], g_ref[...], b_ref[...])
    o_ref[...] = y
    ob_ref[...] = y.astype(BF16)


def _moe_ln_kernel(h_ref, y0_ref, y1_ref, gt_ref, g_ref, b_ref, o_ref, ob_ref):
    gt = gt_ref[...]
    f = y0_ref[...] * gt[:, 0:1] + y1_ref[...] * gt[:, 1:2]
    y = _ln_math(ALPHA * h_ref[...] + f, g_ref[...], b_ref[...])
    o_ref[...] = y
    ob_ref[...] = y.astype(BF16)


def _layernorm(x, g, b, res=None):
    t, d = x.shape
    tm = _pick(t, (384, 256, 128))
    row = pl.BlockSpec((tm, d), lambda i: (i, 0))
    vec = pl.BlockSpec((1, d), lambda i: (0, 0))
    ins = [x] if res is None else [x, res]
    return pl.pallas_call(
        _ln_kernel if res is None else _ln_res_kernel,
        grid=(t // tm,),
        in_specs=[row] * len(ins) + [vec, vec],
        out_specs=[row, row],
        out_shape=[jax.ShapeDtypeStruct((t, d), F32), jax.ShapeDtypeStruct((t, d), BF16)],
        compiler_params=_cparams(("parallel",)),
        name="layernorm",
    )(*ins, g.reshape(1, d), b.reshape(1, d))


def _moe_combine_ln(h, y01, gates_t, g, b):
    t, d = h.shape
    tm = _pick(t, (384, 256, 128))
    nb = t // tm
    row = pl.BlockSpec((tm, d), lambda i: (i, 0))
    vec = pl.BlockSpec((1, d), lambda i: (0, 0))
    return pl.pallas_call(
        _moe_ln_kernel,
        grid=(nb,),
        in_specs=[row, row, pl.BlockSpec((tm, d), lambda i: (i + nb, 0)),
                  pl.BlockSpec((tm, TOP_K_EXPERTS), lambda i: (i, 0)), vec, vec],
        out_specs=[row, row],
        out_shape=[jax.ShapeDtypeStruct((t, d), F32), jax.ShapeDtypeStruct((t, d), BF16)],
        compiler_params=_cparams(("parallel",)),
        name="moe_combine_ln",
    )(h, y01, y01, gates_t, g.reshape(1, d), b.reshape(1, d))


def _rope_chunk(a, c, sa, sb, r):
    return a * c + pltpu.roll(a, LANES - r, 1) * sa + pltpu.roll(a, r, 1) * sb


def _proj_kernel(*refs, rope_r, has_bias, sigmoid, hi_prec, rms):
    it = iter(refs)
    x_ref = next(it)
    w_ref = next(it)
    n_ref = next(it) if rms else None
    tabs = (next(it), next(it), next(it)) if rope_r else None
    b_ref = next(it) if has_bias else None
    o_ref = next(it)
    x = x_ref[...]
    if rms:
        ms = jnp.mean(x * x, axis=-1, keepdims=True)
        x = x * lax.rsqrt(ms + RMS_EPS) * n_ref[...]
    if hi_prec:
        xh, xl = _split_bf16(x)
        wh, wl = _split_bf16(w_ref[...])
        acc = _dot(xh, wh) + _dot(xl, wh) + _dot(xh, wl)
    else:
        acc = _dot(x.astype(BF16), w_ref[...].astype(BF16))
    if has_bias:
        acc = acc + b_ref[...]
    if sigmoid:
        acc = jax.nn.sigmoid(acc)
    if rope_r:
        c_ref, sa_ref, sb_ref = tabs
        for c in range(acc.shape[1] // LANES):
            sl = slice(c * LANES, (c + 1) * LANES)
            o_ref[:, sl] = _rope_chunk(acc[:, sl], c_ref[:, sl], sa_ref[:, sl], sb_ref[:, sl],
                                       rope_r).astype(o_ref.dtype)
    else:
        o_ref[...] = acc.astype(o_ref.dtype)


def _proj(x, w, *, out_dtype, lp, x_col=0, kdim=None, w_cols=None, rope=None, bias=None, sigmoid=False,
          hi_prec=False, rms_gain=None, tm_prefs=(1056, 768, 384, 128), tn_prefs=(512, 384, 256, 128),
          name="proj"):
    t = x.shape[0]
    kdim = x.shape[1] if kdim is None else kdim
    tm = _pick(lp, tm_prefs)
    nrow = lp // tm
    if w_cols is None:
        n = w.shape[1]
        tn = _pick(n, tn_prefs)
        w_spec = pl.BlockSpec((kdim, tn), lambda i, j: (0, j))
    else:
        layer, col0, n = w_cols
        tn = _pick(math.gcd(n, col0) if col0 else n, tn_prefs)
        cb0 = col0 // tn
        w_spec = pl.BlockSpec((None, kdim, tn), lambda i, j: (layer, 0, cb0 + j))
    in_specs = [pl.BlockSpec((tm, kdim), lambda i, j: (i, x_col)), w_spec]
    ins = [x, w]
    if rms_gain is not None:
        in_specs.append(pl.BlockSpec((1, kdim), lambda i, j: (0, 0)))
        ins.append(rms_gain.reshape(1, kdim))
    if rope is not None:
        tab = pl.BlockSpec((tm, tn), lambda i, j: (i % nrow, j))
        in_specs += [tab, tab, tab]
        ins += list(rope[:3])
    if bias is not None:
        in_specs.append(pl.BlockSpec((1, tn), lambda i, j: (0, j)))
        ins.append(bias.reshape(1, n))
    kern = functools.partial(_proj_kernel, rope_r=rope[3] if rope is not None else 0,
                             has_bias=bias is not None, sigmoid=sigmoid, hi_prec=hi_prec,
                             rms=rms_gain is not None)
    return pl.pallas_call(
        kern,
        grid=(t // tm, n // tn),
        in_specs=in_specs,
        out_specs=pl.BlockSpec((tm, tn), lambda i, j: (i, j)),
        out_shape=jax.ShapeDtypeStruct((t, n), out_dtype),
        compiler_params=_cparams(("parallel", "arbitrary")),
        name=name,
    )(*ins)


def _mla_down_kernel(x_ref, w_ref, c_ref, sa_ref, sb_ref, o_ref, kr_ref):
    acc = _dot(x_ref[...], w_ref[...].astype(BF16))
    nq = o_ref.shape[1]
    o_ref[...] = acc[:, :nq]
    kr_ref[...] = _rope_chunk(acc[:, nq:], c_ref[...], sa_ref[...], sb_ref[...],
                              MLA_ROPE // 2).astype(BF16)


def _mla_down(hb, w, rope, lp):
    t, d = hb.shape
    n = w.shape[1]
    nq = MLA_Q_LORA + MLA_KV_LORA
    tm = _pick(lp, (704, 384, 128))
    nrow = lp // tm
    tab = pl.BlockSpec((tm, LANES), lambda i: (i % nrow, 0))
    return pl.pallas_call(
        _mla_down_kernel,
        grid=(t // tm,),
        in_specs=[pl.BlockSpec((tm, d), lambda i: (i, 0)), pl.BlockSpec((d, n), lambda i: (0, 0)),
                  tab, tab, tab],
        out_specs=[pl.BlockSpec((tm, nq), lambda i: (i, 0)), pl.BlockSpec((tm, LANES), lambda i: (i, 0))],
        out_shape=[jax.ShapeDtypeStruct((t, nq), F32), jax.ShapeDtypeStruct((t, LANES), BF16)],
        compiler_params=_cparams(("parallel",)),
        name="mla_down",
    )(hb, w, *rope[:3])


def _glu_kernel(y_ref, wa_ref, wb_ref, o_ref):
    y = y_ref[...]
    ga = _dot(y, wa_ref[...].astype(BF16))
    gb = _dot(y, wb_ref[...].astype(BF16))
    o_ref[...] = (ga * jax.nn.sigmoid(gb)).astype(o_ref.dtype)


def _glu(y, w_glu, layer, lp):
    t, k = y.shape
    n = w_glu.shape[2] // 2
    tm = _pick(lp, (1056, 768, 384, 128))
    tn = _pick(n, (512, 256, 128))
    nj = n // tn
    return pl.pallas_call(
        _glu_kernel,
        grid=(t // tm, nj),
        in_specs=[pl.BlockSpec((tm, k), lambda i, j: (i, 0)),
                  pl.BlockSpec((None, k, tn), lambda i, j: (layer, 0, j)),
                  pl.BlockSpec((None, k, tn), lambda i, j: (layer, 0, j + nj))],
        out_specs=pl.BlockSpec((tm, tn), lambda i, j: (i, j)),
        out_shape=jax.ShapeDtypeStruct((t, n), BF16),
        compiler_params=_cparams(("parallel", "arbitrary")),
        name="glu",
    )(y, w_glu, w_glu)


def _merge_kernel(a_ref, b_ref, c_ref, wa_ref, wb_ref, wc_ref, g0_ref, g1_ref, g2_ref, o_ref):
    m = (g0_ref[...] * _dot(a_ref[...], wa_ref[...].astype(BF16))
         + g1_ref[...] * _dot(b_ref[...], wb_ref[...].astype(BF16))
         + g2_ref[...] * _dot(c_ref[...], wc_ref[...].astype(BF16)))
    o_ref[...] = m.astype(o_ref.dtype)


def _merge(out_a, out_b, out_c, wb_a, wb_b, wb_c, layer, gates, lp):
    t = out_a.shape[0]
    n = wb_a.shape[2]
    tm = _pick(lp, (704, 384, 128))
    tn = _pick(n, (512, 256, 128))
    nj = n // tn

    def act(arr):
        return pl.BlockSpec((tm, arr.shape[1]), lambda i, j: (i, 0))

    def wgt(arr):
        return pl.BlockSpec((None, arr.shape[1], tn), lambda i, j: (layer, 0, j))

    def gate(br):
        return pl.BlockSpec((tm, tn), lambda i, j: (i, j + br * nj))

    return pl.pallas_call(
        _merge_kernel,
        grid=(t // tm, nj),
        in_specs=[act(out_a), act(out_b), act(out_c), wgt(wb_a), wgt(wb_b), wgt(wb_c),
                  gate(0), gate(1), gate(2)],
        out_specs=pl.BlockSpec((tm, tn), lambda i, j: (i, j)),
        out_shape=jax.ShapeDtypeStruct((t, n), BF16),
        compiler_params=_cparams(("parallel", "arbitrary")),
        name="merge",
    )(out_a, out_b, out_c, wb_a, wb_b, wb_c, gates, gates, gates)


def _softmax_update(s, v, m_ref, l_ref, acc_ref):
    tk = s.shape[1]
    m_prev = m_ref[...]
    m_new = jnp.maximum(m_prev, jnp.max(s, axis=1, keepdims=True))
    alpha = jnp.exp2(m_prev - m_new)
    p = jnp.exp2(s - jnp.concatenate([m_new] * (tk // LANES), axis=1))
    l_ref[...] = alpha * l_ref[...] + jnp.sum(p, axis=1, keepdims=True)
    acc_ref[...] = alpha * acc_ref[...] + _dot(p.astype(BF16), v)
    m_ref[...] = m_new


def _mla_attn_kernel(q_ref, kn_ref, kr_ref, v_ref, o_ref, s_ref, m_ref, l_ref, acc_ref):
    qi = pl.program_id(2)
    tq = q_ref.shape[0]
    m_ref[...] = jnp.full(m_ref.shape, -jnp.inf, F32)
    l_ref[...] = jnp.zeros(l_ref.shape, F32)
    acc_ref[...] = jnp.zeros(acc_ref.shape, F32)
    q = q_ref[...]

    def rows_of(kt):
        return pl.ds(pl.multiple_of(kt * tq, tq), tq)

    def scores(kt):
        rows = rows_of(kt)
        return _dot_nt(q, jnp.concatenate([kn_ref[rows, :], kr_ref[rows, :]], axis=1))

    s_ref[...] = scores(0)

    def body(kt, carry):
        s = s_ref[...]
        s_ref[...] = scores(kt + 1)
        _softmax_update(s, v_ref[rows_of(kt), :], m_ref, l_ref, acc_ref)
        return carry

    lax.fori_loop(0, qi, body, 0)
    row = lax.broadcasted_iota(I32, (tq, tq), 0)
    col = lax.broadcasted_iota(I32, (tq, tq), 1)
    s = jnp.where(col <= row, s_ref[...], -jnp.inf)
    _softmax_update(s, v_ref[rows_of(qi), :], m_ref, l_ref, acc_ref)
    o_ref[...] = (acc_ref[...] / l_ref[...]).astype(o_ref.dtype)


def _mla_attention(q, kv, kr, bsz, lp):
    t = q.shape[0]
    tq = _pick(lp, (384, 128))
    nq = lp // tq
    q3 = q.reshape(bsz, lp, q.shape[1])
    kv3 = kv.reshape(bsz, lp, kv.shape[1])
    kr3 = kr.reshape(bsz, lp, LANES)
    out = pl.pallas_call(
        _mla_attn_kernel,
        grid=(bsz, MLA_HEADS, nq),
        in_specs=[
            pl.BlockSpec((None, tq, 2 * LANES), lambda b, h, i: (b, i, h)),
            pl.BlockSpec((None, lp, LANES), lambda b, h, i: (b, 0, 2 * h)),
            pl.BlockSpec((None, lp, LANES), lambda b, h, i: (b, 0, 0)),
            pl.BlockSpec((None, lp, LANES), lambda b, h, i: (b, 0, 2 * h + 1)),
        ],
        out_specs=pl.BlockSpec((None, tq, LANES), lambda b, h, i: (b, i, h)),
        out_shape=jax.ShapeDtypeStruct((bsz, lp, MLA_WIDTH), BF16),
        scratch_shapes=[pltpu.VMEM((tq, tq), F32),
                        pltpu.VMEM((tq, LANES), F32), pltpu.VMEM((tq, LANES), F32),
                        pltpu.VMEM((tq, MLA_V), F32)],
        compiler_params=_cparams(("parallel", "parallel", "arbitrary")),
        name="mla_attention",
    )(q3, kv3, kr3, kv3)
    return out.reshape(t, MLA_WIDTH)


def _dsa_attn_kernel(q_ref, k_ref, v_ref, bias_ref, o_ref, qg_ref, s_ref, m_ref, l_ref, acc_ref, *, tk):
    qi = pl.program_id(1)
    rep = A_HEADS // A_KV_HEADS
    nkt = tk // BLOCK
    m_ref[...] = jnp.full(m_ref.shape, -jnp.inf, F32)
    l_ref[...] = jnp.zeros(l_ref.shape, F32)
    acc_ref[...] = jnp.zeros(acc_ref.shape, F32)
    for g in range(A_KV_HEADS):
        for r in range(rep):
            h = g * rep + r
            qg_ref[g, r * BLOCK:(r + 1) * BLOCK, :] = q_ref[:, h * A_HEAD_DIM:(h + 1) * A_HEAD_DIM]

    def rows_of(kt):
        return pl.ds(pl.multiple_of(kt * tk, tk), tk)

    def scores(kt):
        rows = rows_of(kt)
        for g in range(A_KV_HEADS):
            s_ref[g] = _dot_nt(qg_ref[g], k_ref[rows, g * A_HEAD_DIM:(g + 1) * A_HEAD_DIM])

    def consume(kt, s):
        bias = jnp.concatenate([bias_ref[kt * nkt + j] for j in range(nkt)], axis=1)
        bias = jnp.concatenate([bias] * rep, axis=0)
        for g in range(A_KV_HEADS):
            _softmax_update(s[g] + bias, v_ref[rows_of(kt), g * A_HEAD_DIM:(g + 1) * A_HEAD_DIM],
                            m_ref.at[g], l_ref.at[g], acc_ref.at[g])

    scores(0)

    def body(kt, carry):
        s = [s_ref[g] for g in range(A_KV_HEADS)]
        scores(kt + 1)
        consume(kt, s)
        return carry

    last = (qi * BLOCK) // tk
    lax.fori_loop(0, last, body, 0)
    consume(last, [s_ref[g] for g in range(A_KV_HEADS)])
    for g in range(A_KV_HEADS):
        o = acc_ref[g] / l_ref[g]
        for r in range(rep):
            h = g * rep + r
            o_ref[:, h * A_HEAD_DIM:(h + 1) * A_HEAD_DIM] = (
                o[r * BLOCK:(r + 1) * BLOCK]).astype(o_ref.dtype)


def _dsa_attention(q, k, v, bias, bsz, lp):
    t = q.shape[0]
    nb = lp // BLOCK
    tk = _pick(lp, (384, 128))
    kvw = A_KV_HEADS * A_HEAD_DIM
    q3 = q.reshape(bsz, lp, A_WIDTH)
    k3 = k.reshape(bsz, lp, kvw)
    v3 = v.reshape(bsz, lp, kvw)
    rows = (A_HEADS // A_KV_HEADS) * BLOCK
    out = pl.pallas_call(
        functools.partial(_dsa_attn_kernel, tk=tk),
        grid=(bsz, nb),
        in_specs=[
            pl.BlockSpec((None, BLOCK, A_WIDTH), lambda b, i: (b, i, 0)),
            pl.BlockSpec((None, lp, kvw), lambda b, i: (b, 0, 0)),
            pl.BlockSpec((None, lp, kvw), lambda b, i: (b, 0, 0)),
            pl.BlockSpec((None, None, nb, BLOCK, BLOCK), lambda b, i: (b, i, 0, 0, 0)),
        ],
        out_specs=pl.BlockSpec((None, BLOCK, A_WIDTH), lambda b, i: (b, i, 0)),
        out_shape=jax.ShapeDtypeStruct((bsz, lp, A_WIDTH), BF16),
        scratch_shapes=[pltpu.VMEM((A_KV_HEADS, rows, A_HEAD_DIM), BF16),
                        pltpu.VMEM((A_KV_HEADS, rows, tk), F32),
                        pltpu.VMEM((A_KV_HEADS, rows, LANES), F32),
                        pltpu.VMEM((A_KV_HEADS, rows, LANES), F32),
                        pltpu.VMEM((A_KV_HEADS, rows, A_HEAD_DIM), F32)],
        compiler_params=_cparams(("parallel", "arbitrary")),
        name="dsa_attention",
    )(q3, k3, v3, bias)
    return out.reshape(t, A_WIDTH)


def _indexer_kernel(iq_ref, wq_ref, kw_ref, o_ref, kcat_ref, qcat_ref, wb_ref, key_ref, *, n_sel):
    qb = pl.program_id(1)
    nb = o_ref.shape[0]
    half = IDX_DIM
    lane = lax.broadcasted_iota(I32, (BLOCK, LANES), 1)
    row = lax.broadcasted_iota(I32, (BLOCK, LANES), 0)

    @pl.when(qb == 0)
    def _():
        kf = kw_ref[...]
        klane = lax.broadcasted_iota(I32, kf.shape, 1)
        kz = jnp.where(klane < half, kf, 0.0)
        hi = kz.astype(BF16).astype(F32)
        lo = kz - hi
        lp = kf.shape[0]
        kcat_ref[0:lp, :] = jnp.concatenate(
            [(hi + pltpu.roll(hi, half, 1)).astype(BF16), lo.astype(BF16)], axis=1)
        kcat_ref[lp:lp + BLOCK, :] = jnp.zeros((BLOCK, 2 * LANES), BF16)

    wq = wq_ref[...]
    for h in range(IDX_HEADS):
        chunk = iq_ref[:, (h // 2) * LANES:(h // 2 + 1) * LANES]
        if h % 2 == 0:
            a = jnp.where(lane < half, chunk, 0.0)
        else:
            a = pltpu.roll(jnp.where(lane >= half, chunk, 0.0), half, 1)
        hi = a.astype(BF16).astype(F32)
        lo = a - hi
        qcat_ref[h * BLOCK:(h + 1) * BLOCK, :] = jnp.concatenate(
            [(hi + pltpu.roll(lo, half, 1)).astype(BF16), hi.astype(BF16)], axis=1)
        wb_ref[h] = jnp.broadcast_to(wq[:, half + h:half + h + 1], (BLOCK, LANES))

    qpos = qb * BLOCK + row

    def sortable(x):
        b = pltpu.bitcast(x, I32)
        return b ^ ((b >> 31) & 0x7FFFFFFF)

    n_pair = (qb + 2) // 2

    def score_pair(kp, carry):
        kblk = kcat_ref[pl.ds(pl.multiple_of(kp * 2 * BLOCK, 2 * BLOCK), 2 * BLOCK), :]
        s = _dot_nt(qcat_ref[...], kblk)
        for j in range(2):
            sc = jnp.zeros((BLOCK, LANES), F32)
            for h in range(IDX_HEADS):
                sc = sc + jnp.maximum(s[h * BLOCK:(h + 1) * BLOCK, j * LANES:(j + 1) * LANES], 0.0) * wb_ref[h]
            kpos = (kp * 2 + j) * BLOCK + lane
            sc = jnp.where(kpos < N_META, jnp.inf, sc)
            sc = jnp.where(kpos <= qpos, sc, -jnp.inf)
            key_ref[kp * 2 + j] = sortable(sc)
        return carry

    lax.fori_loop(0, n_pair, score_pair, 0)

    def count(pred):
        def body(kp, c):
            return (c + jnp.where(pred(key_ref[kp * 2]), 1.0, 0.0)
                    + jnp.where(pred(key_ref[kp * 2 + 1]), 1.0, 0.0))
        c = lax.fori_loop(0, n_pair, body, jnp.zeros((BLOCK, LANES), F32))
        return jnp.sum(c, axis=1, keepdims=True)

    def bit_step(i, thr):
        bit = 31 - i
        cand = jnp.where(bit == 31, thr ^ INT_MIN, thr | (1 << jnp.minimum(bit, 30)))
        return jnp.where(count(lambda k: k >= cand) >= n_sel, cand, thr)

    thr = lax.fori_loop(0, 32, bit_step, jnp.full((BLOCK, LANES), INT_MIN, I32))
    n_ge = count(lambda k: k >= thr)
    all_ties_fit = jnp.max(n_ge) <= n_sel

    @pl.when(all_ties_fit)
    def _():
        def emit(kt, carry):
            kpos = kt * BLOCK + lane
            o_ref[kt] = jnp.where((key_ref[kt] >= thr) & (kpos <= qpos), 0.0, -jnp.inf)
            return carry

        lax.fori_loop(0, qb + 1, emit, 0)

    @pl.when(jnp.logical_not(all_ties_fit))
    def _():
        need = n_sel - count(lambda k: k > thr)
        tri = (lax.broadcasted_iota(I32, (LANES, LANES), 0)
               <= lax.broadcasted_iota(I32, (LANES, LANES), 1)).astype(BF16)

        def emit(kt, taken):
            key = key_ref[kt]
            eq = key == thr
            rank = _dot(jnp.where(eq, 1.0, 0.0).astype(BF16), tri)
            sel = (key > thr) | (eq & (taken + rank <= need))
            kpos = kt * BLOCK + lane
            o_ref[kt] = jnp.where(sel & (kpos <= qpos), 0.0, -jnp.inf)
            return taken + rank[:, LANES - 1:LANES]

        lax.fori_loop(0, qb + 1, emit, jnp.zeros((BLOCK, 1), F32))

    def fill(kt, carry):
        o_ref[kt] = jnp.full((BLOCK, LANES), -jnp.inf, F32)
        return carry

    lax.fori_loop(qb + 1, nb, fill, 0)


def _indexer(idx, bsz, lp, n_sel):
    nb = lp // BLOCK
    idx3 = idx.reshape(bsz, lp, idx.shape[1])
    nq = IDX_HEADS * IDX_DIM
    kw_col = nq // LANES
    return pl.pallas_call(
        functools.partial(_indexer_kernel, n_sel=n_sel),
        grid=(bsz, nb),
        in_specs=[
            pl.BlockSpec((None, BLOCK, nq), lambda b, i: (b, i, 0)),
            pl.BlockSpec((None, BLOCK, LANES), lambda b, i: (b, i, kw_col)),
            pl.BlockSpec((None, lp, LANES), lambda b, i: (b, 0, kw_col)),
        ],
        out_specs=pl.BlockSpec((None, None, nb, BLOCK, BLOCK), lambda b, i: (b, i, 0, 0, 0)),
        out_shape=jax.ShapeDtypeStruct((bsz, nb, nb, BLOCK, BLOCK), F32),
        scratch_shapes=[pltpu.VMEM((lp + BLOCK, 2 * LANES), BF16),
                        pltpu.VMEM((IDX_HEADS * BLOCK, 2 * LANES), BF16),
                        pltpu.VMEM((IDX_HEADS, BLOCK, LANES), F32),
                        pltpu.VMEM((nb + 1, BLOCK, LANES), I32)],
        compiler_params=_cparams(("parallel", "arbitrary")),
        name="indexer",
    )(idx3, idx3, idx3)


def _s5_kernel(u_ref, mw_ref, vre_ref, vim_ref, are_ref, aim_ref, o_ref,
               y_ref, sre_ref, sim_ref, pre_ref, pim_ref, *, bsz):
    ng = u_ref.shape[0]
    rows = u_ref.shape[1]
    p = SSM_STATE
    nm = SSM_CHUNK * SSM_GROUP
    for g in range(ng):
        r = _dot(u_ref[g], mw_ref[g])
        y_ref[g] = r[:, :nm]
        sre_ref[:, g * p:(g + 1) * p] = r[:, nm:nm + p]
        sim_ref[:, g * p:(g + 1) * p] = r[:, nm + p:nm + 2 * p]
    ar = are_ref[...]
    ai = aim_ref[...]

    sub = SUBLANES // bsz

    def step(c, carry):
        cr, ci = carry
        rs = pl.ds(pl.multiple_of(c * SUBLANES, SUBLANES), SUBLANES)
        lr = sre_ref[rs, :]
        li = sim_ref[rs, :]
        before_r, before_i = [], []
        for k in range(sub):
            before_r.append(cr)
            before_i.append(ci)
            cr, ci = (ar * cr - ai * ci + lr[k * bsz:(k + 1) * bsz],
                      ar * ci + ai * cr + li[k * bsz:(k + 1) * bsz])
        pre_ref[rs, :] = jnp.concatenate(before_r, axis=0)
        pim_ref[rs, :] = jnp.concatenate(before_i, axis=0)
        return cr, ci

    zero = jnp.zeros((bsz, ng * p), F32)
    lax.fori_loop(0, rows // SUBLANES, step, (zero, zero))
    for g in range(ng):
        y = (y_ref[g] + _dot(pre_ref[:, g * p:(g + 1) * p].astype(BF16), vre_ref[g])
             + _dot(pim_ref[:, g * p:(g + 1) * p].astype(BF16), vim_ref[g]))
        o_ref[g] = y.astype(o_ref.dtype)


def _s5_weights(a_re, a_im, log_dt, b_re, b_im, c_re, c_im, d_skip):
    hp = lax.Precision.HIGHEST
    g, p, c, tc = SSM_GROUPS, SSM_STATE, SSM_GROUP, SSM_CHUNK
    dt = jnp.exp(log_dt)[:, None]
    lam_re, lam_im = dt * a_re, dt * a_im
    mag = jnp.exp(lam_re)
    ab_re, ab_im = mag * jnp.cos(lam_im), mag * jnp.sin(lam_im)
    den = a_re * a_re + a_im * a_im
    f_re = ((ab_re - 1.0) * a_re + ab_im * a_im) / den
    f_im = (ab_im * a_re - (ab_re - 1.0) * a_im) / den
    bb_re = f_re[..., None] * b_re - f_im[..., None] * b_im
    bb_im = f_re[..., None] * b_im + f_im[..., None] * b_re
    d = jnp.arange(tc + 1, dtype=F32)[:, None, None]
    pmag = jnp.exp(d * lam_re)
    pw_re, pw_im = pmag * jnp.cos(d * lam_im), pmag * jnp.sin(d * lam_im)
    z_re = pw_re[:tc, :, :, None] * bb_re - pw_im[:tc, :, :, None] * bb_im
    z_im = pw_re[:tc, :, :, None] * bb_im + pw_im[:tc, :, :, None] * bb_re
    kmat = (jnp.einsum('gop,dgpi->gdio', c_re, z_re, precision=hp)
            - jnp.einsum('gop,dgpi->gdio', c_im, z_im, precision=hp))
    ti = jnp.arange(tc)
    lag = ti[None, :] - ti[:, None]
    m5 = kmat[:, jnp.clip(lag, 0, tc - 1)]
    m5 = jnp.where((lag >= 0)[None, :, :, None, None], m5, 0.0)
    m = jnp.transpose(m5, (0, 1, 3, 2, 4)).reshape(g, tc * c, tc * c)
    m = m + jnp.eye(tc * c, dtype=F32) * jnp.tile(d_skip.reshape(g, 1, c), (1, tc, 1)).reshape(g, 1, tc * c)
    w_re = jnp.transpose(z_re[::-1], (1, 0, 3, 2)).reshape(g, tc * c, p)
    w_im = jnp.transpose(z_im[::-1], (1, 0, 3, 2)).reshape(g, tc * c, p)
    mw = jnp.concatenate([m, w_re, w_im], axis=2).astype(BF16)
    q_re, q_im = pw_re[1:], pw_im[1:]
    v_re = c_re[None] * q_re[:, :, None, :] - c_im[None] * q_im[:, :, None, :]
    v_im = c_re[None] * q_im[:, :, None, :] + c_im[None] * q_re[:, :, None, :]
    v_re = jnp.transpose(v_re, (1, 3, 0, 2)).reshape(g, p, tc * c).astype(BF16)
    v_im = jnp.transpose(-v_im, (1, 3, 0, 2)).reshape(g, p, tc * c).astype(BF16)
    return mw, v_re, v_im, pw_re[tc].reshape(1, g * p), pw_im[tc].reshape(1, g * p)


def _s5(su, weights, bsz, lp):
    mw, v_re, v_im, a_re, a_im = weights
    g, c, tc, p = SSM_GROUPS, SSM_GROUP, SSM_CHUNK, SSM_STATE
    nc = lp // tc
    rows = nc * bsz
    ng = SSM_GROUPS_PER_STEP
    u = jnp.transpose(su.reshape(bsz, nc, tc, g, c), (3, 1, 0, 2, 4)).reshape(g, rows, tc * c)
    y = pl.pallas_call(
        functools.partial(_s5_kernel, bsz=bsz),
        grid=(g // ng,),
        in_specs=[
            pl.BlockSpec((ng, rows, tc * c), lambda i: (i, 0, 0)),
            pl.BlockSpec((ng, tc * c, tc * c + 2 * p), lambda i: (i, 0, 0)),
            pl.BlockSpec((ng, p, tc * c), lambda i: (i, 0, 0)),
            pl.BlockSpec((ng, p, tc * c), lambda i: (i, 0, 0)),
            pl.BlockSpec((1, ng * p), lambda i: (0, i)),
            pl.BlockSpec((1, ng * p), lambda i: (0, i)),
        ],
        out_specs=pl.BlockSpec((ng, rows, tc * c), lambda i: (i, 0, 0)),
        out_shape=jax.ShapeDtypeStruct((g, rows, tc * c), BF16),
        scratch_shapes=[pltpu.VMEM((ng, rows, tc * c), F32),
                        pltpu.VMEM((rows, ng * p), F32), pltpu.VMEM((rows, ng * p), F32),
                        pltpu.VMEM((rows, ng * p), F32), pltpu.VMEM((rows, ng * p), F32)],
        compiler_params=_cparams(("parallel",)),
        name="s5",
    )(u, mw, v_re, v_im, a_re, a_im)
    return jnp.transpose(y.reshape(g, nc, bsz, tc, c), (2, 1, 3, 0, 4)).reshape(bsz * lp, g * c)


def _ffn_kernel(be_ref, nu_ref, x_ref, w1_ref, w3_ref, w2_ref, o_ref):
    i = pl.program_id(0)
    f = pl.program_id(1)

    @pl.when(f == 0)
    def _():
        o_ref[...] = jnp.zeros(o_ref.shape, F32)

    @pl.when(i < nu_ref[0])
    def _():
        x = x_ref[...].astype(BF16)
        a = _dot(x, w1_ref[0].astype(BF16))
        b = _dot(x, w3_ref[0].astype(BF16))
        act = (a * jax.nn.sigmoid(a) * b).astype(BF16)
        o_ref[...] += _dot(act, w2_ref[0].astype(BF16))


def _ffn(x, w1, w3, w2, blk_expert, n_used, tm):
    n, d = x.shape
    dff = w1.shape[2]
    tf = _pick(dff, (256, 128))
    nf = dff // tf

    def live(i, nu):
        return jnp.minimum(i, nu[0] - 1)

    def fidx(i, f, nu):
        return jnp.where(i < nu[0], f, nf - 1)

    grid_spec = pltpu.PrefetchScalarGridSpec(
        num_scalar_prefetch=2,
        grid=(n // tm, nf),
        in_specs=[
            pl.BlockSpec((tm, d), lambda i, f, be, nu: (live(i, nu), 0)),
            pl.BlockSpec((1, d, tf), lambda i, f, be, nu: (be[live(i, nu)], 0, fidx(i, f, nu))),
            pl.BlockSpec((1, d, tf), lambda i, f, be, nu: (be[live(i, nu)], 0, fidx(i, f, nu))),
            pl.BlockSpec((1, tf, d), lambda i, f, be, nu: (be[live(i, nu)], fidx(i, f, nu), 0)),
        ],
        out_specs=pl.BlockSpec((tm, d), lambda i, f, be, nu: (i, 0)),
    )
    return pl.pallas_call(
        _ffn_kernel,
        grid_spec=grid_spec,
        out_shape=jax.ShapeDtypeStruct((n, d), F32),
        compiler_params=_cparams(("arbitrary", "arbitrary")),
        name="swiglu",
    )(blk_expert, n_used, x, w1, w3, w2)


def _router_kernel(h_ref, w_ref, idx_ref, gate_ref):
    hh, hl = _split_bf16(h_ref[...])
    wh, wl = _split_bf16(w_ref[...])
    logits = _dot_nt(wh, hh) + _dot_nt(wl, hh) + _dot_nt(wh, hl)
    e = lax.broadcasted_iota(I32, logits.shape, 0).astype(F32)
    m1 = jnp.max(logits, axis=0, keepdims=True)
    i1 = jnp.min(jnp.where(logits == m1, e, float(N_EXPERTS)), axis=0, keepdims=True)
    rest = jnp.where(e == i1, -jnp.inf, logits)
    m2 = jnp.max(rest, axis=0, keepdims=True)
    i2 = jnp.min(jnp.where(rest == m2, e, float(N_EXPERTS)), axis=0, keepdims=True)
    e2 = jnp.exp(m2 - m1)
    den = 1.0 + e2
    idx_ref[...] = jnp.concatenate([i1, i2], axis=0).astype(I32)
    gate_ref[...] = jnp.concatenate([1.0 / den, e2 / den], axis=0)


def _router(h, w_router_t):
    t, d = h.shape
    tm = _pick(t, (384, 256, 128))
    return pl.pallas_call(
        _router_kernel,
        grid=(t // tm,),
        in_specs=[pl.BlockSpec((tm, d), lambda i: (i, 0)), pl.BlockSpec((N_EXPERTS, d), lambda i: (0, 0))],
        out_specs=[pl.BlockSpec((TOP_K_EXPERTS, tm), lambda i: (0, i)),
                   pl.BlockSpec((TOP_K_EXPERTS, tm), lambda i: (0, i))],
        out_shape=[jax.ShapeDtypeStruct((TOP_K_EXPERTS, t), I32),
                   jax.ShapeDtypeStruct((TOP_K_EXPERTS, t), F32)],
        compiler_params=_cparams(("parallel",)),
        name="router",
    )(h, w_router_t)


def _gather_kernel(idx_ref, src_ref, o_ref, sem, *, tg):
    base = pl.program_id(0) * tg

    def row_copy(src_row, r):
        return pltpu.make_async_copy(src_ref.at[pl.ds(src_row, 1), :], o_ref.at[pl.ds(r, 1), :], sem)

    def issue(r, carry):
        row_copy(idx_ref[base + r], r).start()
        return carry

    lax.fori_loop(0, tg, issue, 0)

    def wait(r, carry):
        row_copy(0, r).wait()
        return carry

    lax.fori_loop(0, tg, wait, 0)


def _gather_rows(src, idx, tg):
    n = idx.shape[0]
    d = src.shape[1]
    grid_spec = pltpu.PrefetchScalarGridSpec(
        num_scalar_prefetch=1,
        grid=(n // tg,),
        in_specs=[pl.BlockSpec(memory_space=pl.ANY)],
        out_specs=pl.BlockSpec((tg, d), lambda i, idx_ref: (i, 0)),
        scratch_shapes=[pltpu.SemaphoreType.DMA(())],
    )
    return pl.pallas_call(
        functools.partial(_gather_kernel, tg=tg),
        grid_spec=grid_spec,
        out_shape=jax.ShapeDtypeStruct((n, d), src.dtype),
        compiler_params=_cparams(("arbitrary",)),
        name="gather_rows",
    )(idx, src)


def _moe(h, w_router, w1, w3, w2, e_base, ln_g, ln_b):
    t, d = h.shape
    tm = 512
    top_idx, gates = _router(h, w_router.T)
    n_assign = t * TOP_K_EXPERTS
    expert = top_idx.reshape(-1)
    onehot = (expert[:, None] == jnp.arange(N_EXPERTS)[None, :]).astype(I32)
    rank = jnp.sum((jnp.cumsum(onehot, axis=0) - onehot) * onehot, axis=1)
    counts = jnp.sum(onehot, axis=0)
    padded = (counts + tm - 1) // tm * tm
    pad_end = jnp.cumsum(padded)
    dest = (pad_end - padded)[expert] + rank
    n_blk = -(-(n_assign + N_EXPERTS * (tm - 1)) // tm)
    n_rows = n_blk * tm
    token = jnp.arange(n_assign, dtype=I32) % t
    row_token = jnp.zeros((n_rows,), I32).at[dest].set(token)
    blk_expert = jnp.minimum(
        jnp.searchsorted(pad_end, jnp.arange(n_blk) * tm, side='right'), N_EXPERTS - 1).astype(I32)
    n_used = (pad_end[-1:] // tm).astype(I32)
    xs = _gather_rows(h, row_token, 256)
    ys = _ffn(xs, w1, w3, w2, blk_expert + e_base, n_used, tm)
    y01 = _gather_rows(ys, dest.astype(I32), 256)
    return _moe_combine_ln(h, y01, gates.T, ln_g, ln_b)


def _rope_tables(lp, n_heads, head_dim, off, rot_dim, scale=1.0):
    r = rot_dim // 2
    pos = jnp.arange(lp)
    inv = ROPE_THETA ** (-jnp.arange(0, rot_dim, 2, dtype=F32) / rot_dim)
    ang = pos.astype(F32)[:, None] * inv[None, :]
    cos, sin = jnp.cos(ang), jnp.sin(ang)
    c = jnp.ones((lp, head_dim), F32).at[:, off:off + r].set(cos).at[:, off + r:off + 2 * r].set(cos)
    sa = jnp.zeros((lp, head_dim), F32).at[:, off:off + r].set(-sin)
    sb = jnp.zeros((lp, head_dim), F32).at[:, off + r:off + 2 * r].set(sin)
    return tuple(jnp.tile(x * scale, (1, n_heads)) for x in (c, sa, sb)) + (r,)


def _pad_cols(w, n):
    return jnp.pad(w, ((0, 0), (0, n - w.shape[1])))


def kernel(x, meta, ln_in_g, ln_in_b, w_in, b_gate, mla_q_norm, mla_kv_norm, w_uq, w_ukv, ssm_a_re, ssm_a_im, ssm_log_dt, ssm_b_re, ssm_b_im, ssm_c_re, ssm_c_im, ssm_d, w_glu, w_branch_a, w_branch_b, w_branch_c, w_o, ln1_g, ln1_b, ffn_w1, ffn_w3, ffn_w2, w_router, moe_w1, moe_w3, moe_w2, ln2_g, ln2_b):
    bsz, seq, _ = x.shape
    n_tok = seq + N_META
    lp = -(-n_tok // BLOCK) * BLOCK
    t = bsz * lp
    n_sel = min(TOPK_MAX, seq // 4)
    meta_b = jnp.broadcast_to(meta[None].astype(x.dtype), (bsz, N_META, D_MODEL))
    pad = jnp.zeros((bsz, lp - n_tok, D_MODEL), x.dtype)
    h, hb = _layernorm(jnp.concatenate([meta_b, x, pad], axis=1).reshape(t, D_MODEL), ln_in_g, ln_in_b)

    a_scale = A_HEAD_DIM ** -0.5 * LOG2E
    rope_q = _rope_tables(lp, A_HEADS, A_HEAD_DIM, 0, A_ROT, a_scale)
    rope_k = _rope_tables(lp, A_KV_HEADS, A_HEAD_DIM, 0, A_ROT)
    iq_tab = _rope_tables(lp, IDX_HEADS, IDX_DIM, 0, IDX_ROT)
    ik_tab = _rope_tables(lp, 1, LANES, 0, IDX_ROT)
    w_scale = (jnp.zeros((LANES,), F32).at[:IDX_DIM].set(1.0)
               .at[IDX_DIM:IDX_DIM + IDX_HEADS].set((IDX_HEADS * IDX_DIM) ** -0.5))
    rope_i = tuple(jnp.concatenate([a, b * w_scale[None, :], jnp.zeros((lp, LANES), F32)], axis=1)
                   for a, b in zip(iq_tab[:3], ik_tab[:3])) + (IDX_ROT // 2,)
    n_idx = IDX_HEADS * IDX_DIM + 2 * LANES
    rope_kr = _rope_tables(lp, 1, LANES, 0, MLA_ROPE)
    m_scale = (MLA_NOPE + MLA_ROPE) ** -0.5 * LOG2E
    rope_mq = _rope_tables(lp, MLA_HEADS, 2 * LANES, MLA_NOPE, MLA_ROPE, m_scale)

    offs = [0]
    for s in IN_SIZES:
        offs.append(offs[-1] + s)
    o_aq, o_ak, o_av, o_iq, o_ik, o_iw, o_dq, o_dkv, o_kr, o_su, o_gl, o_end = offs

    for layer in range(DEPTH):
        w_md = _pad_cols(w_in[layer, :, o_dq:o_su], MLA_Q_LORA + MLA_KV_LORA + LANES)
        w_su = w_in[layer, :, o_su:o_gl]
        w_gl = w_in[layer, :, o_gl:o_end]

        q_a = _proj(hb, w_in, w_cols=(layer, o_aq, o_ak - o_aq), out_dtype=BF16, lp=lp, rope=rope_q,
                    name="proj_aq")
        k_a = _proj(hb, w_in, w_cols=(layer, o_ak, o_av - o_ak), out_dtype=BF16, lp=lp, rope=rope_k,
                    name="proj_ak")
        v_a = _proj(hb, w_in, w_cols=(layer, o_av, o_iq - o_av), out_dtype=BF16, lp=lp, name="proj_av")
        idx = _proj(h, w_in, w_cols=(layer, o_iq, n_idx), out_dtype=F32, lp=lp, rope=rope_i, hi_prec=True,
                    tm_prefs=(384, 128), tn_prefs=(n_idx,), name="proj_idx")
        bias = _indexer(idx, bsz, lp, n_sel)
        out_a = _dsa_attention(q_a, k_a, v_a, bias, bsz, lp)

        dqkv, kr = _mla_down(hb, w_md, rope_kr, lp)
        wq = w_uq[layer].reshape(MLA_Q_LORA, MLA_HEADS, MLA_NOPE + MLA_ROPE)
        wq = jnp.pad(wq, ((0, 0), (0, 0), (0, 2 * LANES - MLA_NOPE - MLA_ROPE))).reshape(MLA_Q_LORA, -1)
        q_m = _proj(dqkv, wq, out_dtype=BF16, lp=lp, x_col=0, kdim=MLA_Q_LORA, rope=rope_mq,
                    rms_gain=mla_q_norm[layer], name="proj_mq")
        kv_m = _proj(dqkv, w_ukv, w_cols=(layer, 0, w_ukv.shape[2]), out_dtype=BF16, lp=lp, x_col=1,
                     kdim=MLA_KV_LORA, rms_gain=mla_kv_norm[layer], name="proj_mkv")
        out_b = _mla_attention(q_m, kv_m, kr, bsz, lp)

        su = _proj(hb, w_su, out_dtype=BF16, lp=lp, name="proj_su")
        s5w = _s5_weights(ssm_a_re[layer], ssm_a_im[layer], ssm_log_dt[layer], ssm_b_re[layer],
                          ssm_b_im[layer], ssm_c_re[layer], ssm_c_im[layer], ssm_d[layer])
        y = _s5(su, s5w, bsz, lp)
        out_c = _glu(y, w_glu, layer, lp)

        gates = _proj(hb, w_gl, out_dtype=F32, lp=lp, bias=b_gate[layer], sigmoid=True, name="proj_gates")
        merged = _merge(out_a, out_b, out_c, w_branch_a, w_branch_b, w_branch_c, layer, gates, lp)
        mix = _proj(merged, w_o, w_cols=(layer, 0, D_MODEL), out_dtype=F32, lp=lp, name="proj_o")
        h, hb = _layernorm(h, ln1_g[layer], ln1_b[layer], res=mix)

        i = layer // 2
        if layer % 2 == 0:
            tm = _pick(t, (704, 384, 128))
            nblk = t // tm
            f = _ffn(hb, ffn_w1, ffn_w3, ffn_w2, jnp.full((nblk,), i, I32), jnp.full((1,), nblk, I32), tm)
            h, hb = _layernorm(h, ln2_g[layer], ln2_b[layer], res=f)
        else:
            n_all = moe_w1.shape[0] * N_EXPERTS
            h, hb = _moe(h, w_router[i], moe_w1.reshape((n_all,) + moe_w1.shape[2:]),
                         moe_w3.reshape((n_all,) + moe_w3.shape[2:]),
                         moe_w2.reshape((n_all,) + moe_w2.shape[2:]), i * N_EXPERTS,
                         ln2_g[layer], ln2_b[layer])

    return h.reshape(bsz, lp, D_MODEL)[:, N_META:N_META + seq]
```

```python
import functools
import math

import jax
import jax.numpy as jnp
from jax import lax
from jax.experimental import pallas as pl
from jax.experimental.pallas import tpu as pltpu

F32 = jnp.float32
BF16 = jnp.bfloat16
I32 = jnp.int32

D_MODEL = 2048
DEPTH = 4
N_META = 16
BLOCK = 128
ROPE_THETA = 500000.0
LN_EPS = 1e-5
RMS_EPS = 1e-6
ALPHA = (2 * DEPTH) ** 0.25

A_HEADS = 8
A_KV_HEADS = 2
A_HEAD_DIM = 128
A_ROT = A_HEAD_DIM // 4
IDX_HEADS = 8
IDX_DIM = 64
IDX_ROT = IDX_DIM // 4
TOPK_MAX = 256

MLA_HEADS = 8
MLA_Q_LORA = 512
MLA_KV_LORA = 512
MLA_NOPE = 128
MLA_ROPE = 64
MLA_V = 128

SSM_WIDTH = 1024
SSM_GROUP = 16
SSM_GROUPS = SSM_WIDTH // SSM_GROUP
SSM_STATE = 64
SSM_CHUNK = 16

N_BRANCH = 3
A_WIDTH = A_HEADS * A_HEAD_DIM
MLA_WIDTH = MLA_HEADS * MLA_V
IN_SIZES = (A_HEADS * A_HEAD_DIM, A_KV_HEADS * A_HEAD_DIM, A_KV_HEADS * A_HEAD_DIM,
            IDX_HEADS * IDX_DIM, IDX_DIM, IDX_HEADS,
            MLA_Q_LORA, MLA_KV_LORA, MLA_ROPE,
            SSM_WIDTH, N_BRANCH * D_MODEL)

D_FF = 5632
N_EXPERTS = 8
TOP_K_EXPERTS = 2

LANES = 128
SUBLANES = 8
VMEM_LIMIT = 56 * 1024 * 1024
INT_MIN = -2 ** 31
LOG2E = math.log2(math.e)
GATHER_UNROLL = 8
MOE_ROWS = 768


def _pick(n, prefs):
    for p in prefs:
        if n % p == 0:
            return p
    raise ValueError(f"no tile in {prefs} divides {n}")


def _cparams(sem):
    return pltpu.CompilerParams(dimension_semantics=sem, vmem_limit_bytes=VMEM_LIMIT)


def _dot(a, b):
    return jnp.dot(a, b, preferred_element_type=F32)


def _dot_nt(a, b):
    return lax.dot_general(a, b, (((1,), (1,)), ((), ())), preferred_element_type=F32)


def _split_bf16(x):
    hi = x.astype(BF16)
    lo = (x - hi.astype(F32)).astype(BF16)
    return hi, lo


def _ln_math(x, g, b):
    mu = jnp.mean(x, axis=-1, keepdims=True)
    xc = x - mu
    var = jnp.mean(xc * xc, axis=-1, keepdims=True)
    return xc * lax.rsqrt(var + LN_EPS) * g + b


def _ln_kernel(x_ref, g_ref, b_ref, o_ref, ob_ref):
    y = _ln_math(x_ref[...], g_ref[...], b_ref[...])
    o_ref[...] = y
    ob_ref[...] = y.astype(BF16)


def _ln_res_kernel(h_ref, r_ref, g_ref, b_ref, o_ref, ob_ref):
    y = _ln_math(ALPHA * h_ref[...] + r_ref[...], g_ref[...], b_ref[...])
    o_ref[...] = y
    ob_ref[...] = y.astype(BF16)


def _moe_ln_kernel(h_ref, y_ref, gt_ref, g_ref, b_ref, o_ref, ob_ref):
    gt = gt_ref[...]
    f = y_ref[0] * gt[:, 0:1] + y_ref[1] * gt[:, 1:2]
    y = _ln_math(ALPHA * h_ref[...] + f, g_ref[...], b_ref[...])
    o_ref[...] = y
    ob_ref[...] = y.astype(BF16)


def _layernorm(x, g, b, res=None):
    t, d = x.shape
    tm = _pick(t, (384, 256, 128))
    row = pl.BlockSpec((tm, d), lambda i: (i, 0))
    vec = pl.BlockSpec((1, d), lambda i: (0, 0))
    ins = [x] if res is None else [x, res]
    return pl.pallas_call(
        _ln_kernel if res is None else _ln_res_kernel,
        grid=(t // tm,),
        in_specs=[row] * len(ins) + [vec, vec],
        out_specs=[row, row],
        out_shape=[jax.ShapeDtypeStruct((t, d), F32), jax.ShapeDtypeStruct((t, d), BF16)],
        compiler_params=_cparams(("parallel",)),
        name="layernorm",
    )(*ins, g.reshape(1, d), b.reshape(1, d))


def _moe_combine_ln(h, y01, gates_t, g, b):
    t, d = h.shape
    tm = _pick(t, (384, 256, 128))
    nb = t // tm
    row = pl.BlockSpec((tm, d), lambda i: (i, 0))
    vec = pl.BlockSpec((1, d), lambda i: (0, 0))
    return pl.pallas_call(
        _moe_ln_kernel,
        grid=(nb,),
        in_specs=[row, pl.BlockSpec((TOP_K_EXPERTS, tm, d), lambda i: (0, i, 0)),
                  pl.BlockSpec((tm, TOP_K_EXPERTS), lambda i: (i, 0)), vec, vec],
        out_specs=[row, row],
        out_shape=[jax.ShapeDtypeStruct((t, d), F32), jax.ShapeDtypeStruct((t, d), BF16)],
        compiler_params=_cparams(("parallel",)),
        name="moe_combine_ln",
    )(h, y01.reshape(TOP_K_EXPERTS, t, d), gates_t, g.reshape(1, d), b.reshape(1, d))


def _rope_chunk(a, c, sa, sb, r):
    return a * c + pltpu.roll(a, LANES - r, 1) * sa + pltpu.roll(a, r, 1) * sb


def _proj_kernel(*refs, rope_r, has_bias, sigmoid, hi_prec, rms):
    it = iter(refs)
    x_ref = next(it)
    w_ref = next(it)
    n_ref = next(it) if rms else None
    tabs = (next(it), next(it), next(it)) if rope_r else None
    b_ref = next(it) if has_bias else None
    o_ref = next(it)
    x = x_ref[...]
    if rms:
        ms = jnp.mean(x * x, axis=-1, keepdims=True)
        x = x * lax.rsqrt(ms + RMS_EPS) * n_ref[...]
    if hi_prec:
        xh, xl = _split_bf16(x)
        wh, wl = _split_bf16(w_ref[...])
        acc = _dot(xh, wh) + _dot(xl, wh) + _dot(xh, wl)
    else:
        acc = _dot(x.astype(BF16), w_ref[...].astype(BF16))
    if has_bias:
        acc = acc + b_ref[...]
    if sigmoid:
        acc = jax.nn.sigmoid(acc)
    if rope_r:
        c_ref, sa_ref, sb_ref = tabs
        for c in range(acc.shape[1] // LANES):
            sl = slice(c * LANES, (c + 1) * LANES)
            o_ref[:, sl] = _rope_chunk(acc[:, sl], c_ref[:, sl], sa_ref[:, sl], sb_ref[:, sl],
                                       rope_r).astype(o_ref.dtype)
    else:
        o_ref[...] = acc.astype(o_ref.dtype)


def _proj(x, w, *, out_dtype, lp, x_col=0, kdim=None, w_cols=None, rope=None, bias=None, sigmoid=False,
          hi_prec=False, rms_gain=None, tm_prefs=(1056, 768, 384, 128), tn_prefs=(512, 384, 256, 128),
          name="proj"):
    t = x.shape[0]
    kdim = x.shape[1] if kdim is None else kdim
    tm = _pick(lp, tm_prefs)
    nrow = lp // tm
    if w_cols is None:
        n = w.shape[1]
        tn = _pick(n, tn_prefs)
        w_spec = pl.BlockSpec((kdim, tn), lambda i, j: (0, j))
    else:
        layer, col0, n = w_cols
        tn = _pick(math.gcd(n, col0) if col0 else n, tn_prefs)
        cb0 = col0 // tn
        w_spec = pl.BlockSpec((None, kdim, tn), lambda i, j: (layer, 0, cb0 + j))
    in_specs = [pl.BlockSpec((tm, kdim), lambda i, j: (i, x_col)), w_spec]
    ins = [x, w]
    if rms_gain is not None:
        in_specs.append(pl.BlockSpec((1, kdim), lambda i, j: (0, 0)))
        ins.append(rms_gain.reshape(1, kdim))
    if rope is not None:
        tab = pl.BlockSpec((tm, tn), lambda i, j: (i % nrow, j))
        in_specs += [tab, tab, tab]
        ins += list(rope[:3])
    if bias is not None:
        in_specs.append(pl.BlockSpec((1, tn), lambda i, j: (0, j)))
        ins.append(bias.reshape(1, n))
    kern = functools.partial(_proj_kernel, rope_r=rope[3] if rope is not None else 0,
                             has_bias=bias is not None, sigmoid=sigmoid, hi_prec=hi_prec,
                             rms=rms_gain is not None)
    return pl.pallas_call(
        kern,
        grid=(t // tm, n // tn),
        in_specs=in_specs,
        out_specs=pl.BlockSpec((tm, tn), lambda i, j: (i, j)),
        out_shape=jax.ShapeDtypeStruct((t, n), out_dtype),
        compiler_params=_cparams(("parallel", "arbitrary")),
        name=name,
    )(*ins)


def _mla_down_kernel(x_ref, w_ref, c_ref, sa_ref, sb_ref, o_ref, kr_ref):
    acc = _dot(x_ref[...], w_ref[...].astype(BF16))
    nq = o_ref.shape[1]
    o_ref[...] = acc[:, :nq]
    kr_ref[...] = _rope_chunk(acc[:, nq:], c_ref[...], sa_ref[...], sb_ref[...],
                              MLA_ROPE // 2).astype(BF16)


def _mla_down(hb, w, rope, lp):
    t, d = hb.shape
    n = w.shape[1]
    nq = MLA_Q_LORA + MLA_KV_LORA
    tm = _pick(lp, (704, 384, 128))
    nrow = lp // tm
    tab = pl.BlockSpec((tm, LANES), lambda i: (i % nrow, 0))
    return pl.pallas_call(
        _mla_down_kernel,
        grid=(t // tm,),
        in_specs=[pl.BlockSpec((tm, d), lambda i: (i, 0)), pl.BlockSpec((d, n), lambda i: (0, 0)),
                  tab, tab, tab],
        out_specs=[pl.BlockSpec((tm, nq), lambda i: (i, 0)), pl.BlockSpec((tm, LANES), lambda i: (i, 0))],
        out_shape=[jax.ShapeDtypeStruct((t, nq), F32), jax.ShapeDtypeStruct((t, LANES), BF16)],
        compiler_params=_cparams(("parallel",)),
        name="mla_down",
    )(hb, w, *rope[:3])


def _glu_kernel(y_ref, w_ref, o_ref):
    n = o_ref.shape[1]
    y = y_ref[...].astype(BF16)
    ga = _dot(y, w_ref[:, :n].astype(BF16))
    gb = _dot(y, w_ref[:, n:].astype(BF16))
    o_ref[...] = (ga * jax.nn.sigmoid(gb)).astype(o_ref.dtype)


def _glu(y, w_glu, layer, lp):
    t, k = y.shape
    n = w_glu.shape[2] // 2
    tm = _pick(lp, (704, 384, 128))
    return pl.pallas_call(
        _glu_kernel,
        grid=(t // tm,),
        in_specs=[pl.BlockSpec((tm, k), lambda i: (i, 0)),
                  pl.BlockSpec((None, k, 2 * n), lambda i: (layer, 0, 0))],
        out_specs=pl.BlockSpec((tm, n), lambda i: (i, 0)),
        out_shape=jax.ShapeDtypeStruct((t, n), BF16),
        compiler_params=_cparams(("parallel",)),
        name="glu",
    )(y, w_glu)


def _merge_kernel(a_ref, b_ref, c_ref, wa_ref, wb_ref, wc_ref, g0_ref, g1_ref, g2_ref, o_ref):
    m = (g0_ref[...] * _dot(a_ref[...], wa_ref[...].astype(BF16))
         + g1_ref[...] * _dot(b_ref[...], wb_ref[...].astype(BF16))
         + g2_ref[...] * _dot(c_ref[...], wc_ref[...].astype(BF16)))
    o_ref[...] = m.astype(o_ref.dtype)


def _merge(out_a, out_b, out_c, wb_a, wb_b, wb_c, layer, gates, lp):
    t = out_a.shape[0]
    n = wb_a.shape[2]
    tm = _pick(lp, (704, 384, 128))
    tn = _pick(n, (512, 256, 128))
    nj = n // tn

    def act(arr):
        return pl.BlockSpec((tm, arr.shape[1]), lambda i, j: (i, 0))

    def wgt(arr):
        return pl.BlockSpec((None, arr.shape[1], tn), lambda i, j: (layer, 0, j))

    def gate(br):
        return pl.BlockSpec((tm, tn), lambda i, j: (i, j + br * nj))

    return pl.pallas_call(
        _merge_kernel,
        grid=(t // tm, nj),
        in_specs=[act(out_a), act(out_b), act(out_c), wgt(wb_a), wgt(wb_b), wgt(wb_c),
                  gate(0), gate(1), gate(2)],
        out_specs=pl.BlockSpec((tm, tn), lambda i, j: (i, j)),
        out_shape=jax.ShapeDtypeStruct((t, n), BF16),
        compiler_params=_cparams(("parallel", "arbitrary")),
        name="merge",
    )(out_a, out_b, out_c, wb_a, wb_b, wb_c, gates, gates, gates)


def _softmax_update(s, v, m_ref, l_ref, acc_ref):
    tk = s.shape[1]
    m_prev = m_ref[...]
    m_new = jnp.maximum(m_prev, jnp.max(s, axis=1, keepdims=True))
    alpha = jnp.exp2(m_prev - m_new)
    p = jnp.exp2(s - jnp.concatenate([m_new] * (tk // LANES), axis=1))
    l_ref[...] = alpha * l_ref[...] + jnp.sum(p, axis=1, keepdims=True)
    acc_ref[...] = alpha * acc_ref[...] + _dot(p.astype(BF16), v)
    m_ref[...] = m_new


def _mla_attn_kernel(q_ref, kv_ref, kr_ref, o_ref, s_ref, m_ref, l_ref, acc_ref):
    qi = pl.program_id(2)
    tq = q_ref.shape[0]
    m_ref[...] = jnp.full(m_ref.shape, -jnp.inf, F32)
    l_ref[...] = jnp.zeros(l_ref.shape, F32)
    acc_ref[...] = jnp.zeros(acc_ref.shape, F32)
    q = q_ref[...]

    def rows_of(kt):
        return pl.ds(pl.multiple_of(kt * tq, tq), tq)

    def scores(kt):
        rows = rows_of(kt)
        return _dot_nt(q, jnp.concatenate([kv_ref[rows, :MLA_NOPE], kr_ref[rows, :]], axis=1))

    s_ref[...] = scores(0)

    def body(kt, carry):
        s = s_ref[...]
        s_ref[...] = scores(kt + 1)
        _softmax_update(s, kv_ref[rows_of(kt), MLA_NOPE:], m_ref, l_ref, acc_ref)
        return carry

    lax.fori_loop(0, qi, body, 0)
    row = lax.broadcasted_iota(I32, (tq, tq), 0)
    col = lax.broadcasted_iota(I32, (tq, tq), 1)
    s = jnp.where(col <= row, s_ref[...], -jnp.inf)
    _softmax_update(s, kv_ref[rows_of(qi), MLA_NOPE:], m_ref, l_ref, acc_ref)
    o_ref[...] = (acc_ref[...] / l_ref[...]).astype(o_ref.dtype)


def _mla_attention(q, kv, kr, bsz, lp):
    t = q.shape[0]
    tq = _pick(lp, (384, 128))
    nq = lp // tq
    q3 = q.reshape(bsz, lp, q.shape[1])
    kv3 = kv.reshape(bsz, lp, kv.shape[1])
    kr3 = kr.reshape(bsz, lp, LANES)
    out = pl.pallas_call(
        _mla_attn_kernel,
        grid=(bsz, MLA_HEADS, nq),
        in_specs=[
            pl.BlockSpec((None, tq, 2 * LANES), lambda b, h, i: (b, i, h)),
            pl.BlockSpec((None, lp, MLA_NOPE + MLA_V), lambda b, h, i: (b, 0, h)),
            pl.BlockSpec((None, lp, LANES), lambda b, h, i: (b, 0, 0)),
        ],
        out_specs=pl.BlockSpec((None, tq, LANES), lambda b, h, i: (b, i, h)),
        out_shape=jax.ShapeDtypeStruct((bsz, lp, MLA_WIDTH), BF16),
        scratch_shapes=[pltpu.VMEM((tq, tq), F32),
                        pltpu.VMEM((tq, LANES), F32), pltpu.VMEM((tq, LANES), F32),
                        pltpu.VMEM((tq, MLA_V), F32)],
        compiler_params=_cparams(("parallel", "parallel", "arbitrary")),
        name="mla_attention",
    )(q3, kv3, kr3)
    return out.reshape(t, MLA_WIDTH)


def _dsa_attn_kernel(q_ref, k_ref, v_ref, bias_ref, o_ref, qg_ref, s_ref, m_ref, l_ref, acc_ref, *, tk):
    qi = pl.program_id(1)
    rep = A_HEADS // A_KV_HEADS
    nkt = tk // BLOCK
    m_ref[...] = jnp.full(m_ref.shape, -jnp.inf, F32)
    l_ref[...] = jnp.zeros(l_ref.shape, F32)
    acc_ref[...] = jnp.zeros(acc_ref.shape, F32)
    for g in range(A_KV_HEADS):
        for r in range(rep):
            h = g * rep + r
            qg_ref[g, r * BLOCK:(r + 1) * BLOCK, :] = q_ref[:, h * A_HEAD_DIM:(h + 1) * A_HEAD_DIM]

    def rows_of(kt):
        return pl.ds(pl.multiple_of(kt * tk, tk), tk)

    def scores(kt):
        rows = rows_of(kt)
        for g in range(A_KV_HEADS):
            s_ref[g] = _dot_nt(qg_ref[g], k_ref[rows, g * A_HEAD_DIM:(g + 1) * A_HEAD_DIM])

    def consume(kt, s):
        bias = jnp.concatenate([bias_ref[kt * nkt + j] for j in range(nkt)], axis=1)
        bias = jnp.concatenate([bias] * rep, axis=0)
        for g in range(A_KV_HEADS):
            _softmax_update(s[g] + bias, v_ref[rows_of(kt), g * A_HEAD_DIM:(g + 1) * A_HEAD_DIM],
                            m_ref.at[g], l_ref.at[g], acc_ref.at[g])

    scores(0)

    def body(kt, carry):
        s = [s_ref[g] for g in range(A_KV_HEADS)]
        scores(kt + 1)
        consume(kt, s)
        return carry

    last = (qi * BLOCK) // tk
    lax.fori_loop(0, last, body, 0)
    consume(last, [s_ref[g] for g in range(A_KV_HEADS)])
    for g in range(A_KV_HEADS):
        o = acc_ref[g] / l_ref[g]
        for r in range(rep):
            h = g * rep + r
            o_ref[:, h * A_HEAD_DIM:(h + 1) * A_HEAD_DIM] = (
                o[r * BLOCK:(r + 1) * BLOCK]).astype(o_ref.dtype)


def _dsa_attention(q, k, v, bias, bsz, lp):
    t = q.shape[0]
    nb = lp // BLOCK
    tk = _pick(lp, (384, 128))
    kvw = A_KV_HEADS * A_HEAD_DIM
    q3 = q.reshape(bsz, lp, A_WIDTH)
    k3 = k.reshape(bsz, lp, kvw)
    v3 = v.reshape(bsz, lp, kvw)
    rows = (A_HEADS // A_KV_HEADS) * BLOCK
    out = pl.pallas_call(
        functools.partial(_dsa_attn_kernel, tk=tk),
        grid=(bsz, nb),
        in_specs=[
            pl.BlockSpec((None, BLOCK, A_WIDTH), lambda b, i: (b, i, 0)),
            pl.BlockSpec((None, lp, kvw), lambda b, i: (b, 0, 0)),
            pl.BlockSpec((None, lp, kvw), lambda b, i: (b, 0, 0)),
            pl.BlockSpec((None, None, nb, BLOCK, BLOCK), lambda b, i: (b, i, 0, 0, 0)),
        ],
        out_specs=pl.BlockSpec((None, BLOCK, A_WIDTH), lambda b, i: (b, i, 0)),
        out_shape=jax.ShapeDtypeStruct((bsz, lp, A_WIDTH), BF16),
        scratch_shapes=[pltpu.VMEM((A_KV_HEADS, rows, A_HEAD_DIM), BF16),
                        pltpu.VMEM((A_KV_HEADS, rows, tk), F32),
                        pltpu.VMEM((A_KV_HEADS, rows, LANES), F32),
                        pltpu.VMEM((A_KV_HEADS, rows, LANES), F32),
                        pltpu.VMEM((A_KV_HEADS, rows, A_HEAD_DIM), F32)],
        compiler_params=_cparams(("parallel", "arbitrary")),
        name="dsa_attention",
    )(q3, k3, v3, bias)
    return out.reshape(t, A_WIDTH)


def _indexer_kernel(iq_ref, wq_ref, kw_ref, o_ref, kcat_ref, qcat_ref, wb_ref, key_ref, *, n_sel):
    qb = pl.program_id(1)
    nb = o_ref.shape[0]
    half = IDX_DIM
    lane = lax.broadcasted_iota(I32, (BLOCK, LANES), 1)
    row = lax.broadcasted_iota(I32, (BLOCK, LANES), 0)

    @pl.when(qb == 0)
    def _():
        kf = kw_ref[...]
        klane = lax.broadcasted_iota(I32, kf.shape, 1)
        kz = jnp.where(klane < half, kf, 0.0)
        hi = kz.astype(BF16).astype(F32)
        lo = kz - hi
        lp = kf.shape[0]
        kcat_ref[0:lp, :] = jnp.concatenate(
            [(hi + pltpu.roll(hi, half, 1)).astype(BF16), lo.astype(BF16)], axis=1)
        kcat_ref[lp:lp + BLOCK, :] = jnp.zeros((BLOCK, 2 * LANES), BF16)

    wq = wq_ref[...]
    for h in range(IDX_HEADS):
        chunk = iq_ref[:, (h // 2) * LANES:(h // 2 + 1) * LANES]
        if h % 2 == 0:
            a = jnp.where(lane < half, chunk, 0.0)
        else:
            a = pltpu.roll(jnp.where(lane >= half, chunk, 0.0), half, 1)
        hi = a.astype(BF16).astype(F32)
        lo = a - hi
        qcat_ref[h * BLOCK:(h + 1) * BLOCK, :] = jnp.concatenate(
            [(hi + pltpu.roll(lo, half, 1)).astype(BF16), hi.astype(BF16)], axis=1)
        wb_ref[h] = jnp.broadcast_to(wq[:, half + h:half + h + 1], (BLOCK, LANES))

    qpos = qb * BLOCK + row

    def sortable(x):
        b = pltpu.bitcast(x, I32)
        return b ^ ((b >> 31) & 0x7FFFFFFF)

    n_pair = (qb + 2) // 2

    def score_pair(kp, carry):
        kblk = kcat_ref[pl.ds(pl.multiple_of(kp * 2 * BLOCK, 2 * BLOCK), 2 * BLOCK), :]
        s = _dot_nt(qcat_ref[...], kblk)
        for j in range(2):
            sc = jnp.zeros((BLOCK, LANES), F32)
            for h in range(IDX_HEADS):
                sc = sc + jnp.maximum(s[h * BLOCK:(h + 1) * BLOCK, j * LANES:(j + 1) * LANES], 0.0) * wb_ref[h]
            kpos = (kp * 2 + j) * BLOCK + lane
            sc = jnp.where(kpos < N_META, jnp.inf, sc)
            sc = jnp.where(kpos <= qpos, sc, -jnp.inf)
            key_ref[kp * 2 + j] = sortable(sc)
        return carry

    lax.fori_loop(0, n_pair, score_pair, 0)

    def count(pred):
        def body(kp, c):
            return (c + jnp.where(pred(key_ref[kp * 2]), 1.0, 0.0)
                    + jnp.where(pred(key_ref[kp * 2 + 1]), 1.0, 0.0))
        c = lax.fori_loop(0, n_pair, body, jnp.zeros((BLOCK, LANES), F32))
        return jnp.sum(c, axis=1, keepdims=True)

    def bit_step(i, thr):
        bit = 31 - i
        cand = jnp.where(bit == 31, thr ^ INT_MIN, thr | (1 << jnp.minimum(bit, 30)))
        return jnp.where(count(lambda k: k >= cand) >= n_sel, cand, thr)

    thr = lax.fori_loop(0, 32, bit_step, jnp.full((BLOCK, LANES), INT_MIN, I32))
    n_ge = count(lambda k: k >= thr)
    all_ties_fit = jnp.max(n_ge) <= n_sel

    @pl.when(all_ties_fit)
    def _():
        def emit(kt, carry):
            kpos = kt * BLOCK + lane
            o_ref[kt] = jnp.where((key_ref[kt] >= thr) & (kpos <= qpos), 0.0, -jnp.inf)
            return carry

        lax.fori_loop(0, qb + 1, emit, 0)

    @pl.when(jnp.logical_not(all_ties_fit))
    def _():
        need = n_sel - count(lambda k: k > thr)
        tri = (lax.broadcasted_iota(I32, (LANES, LANES), 0)
               <= lax.broadcasted_iota(I32, (LANES, LANES), 1)).astype(BF16)

        def emit(kt, taken):
            key = key_ref[kt]
            eq = key == thr
            rank = _dot(jnp.where(eq, 1.0, 0.0).astype(BF16), tri)
            sel = (key > thr) | (eq & (taken + rank <= need))
            kpos = kt * BLOCK + lane
            o_ref[kt] = jnp.where(sel & (kpos <= qpos), 0.0, -jnp.inf)
            return taken + rank[:, LANES - 1:LANES]

        lax.fori_loop(0, qb + 1, emit, jnp.zeros((BLOCK, 1), F32))

    def fill(kt, carry):
        o_ref[kt] = jnp.full((BLOCK, LANES), -jnp.inf, F32)
        return carry

    lax.fori_loop(qb + 1, nb, fill, 0)


def _indexer(idx, bsz, lp, n_sel):
    nb = lp // BLOCK
    idx3 = idx.reshape(bsz, lp, idx.shape[1])
    nq = IDX_HEADS * IDX_DIM
    kw_col = nq // LANES
    return pl.pallas_call(
        functools.partial(_indexer_kernel, n_sel=n_sel),
        grid=(bsz, nb),
        in_specs=[
            pl.BlockSpec((None, BLOCK, nq), lambda b, i: (b, i, 0)),
            pl.BlockSpec((None, BLOCK, LANES), lambda b, i: (b, i, kw_col)),
            pl.BlockSpec((None, lp, LANES), lambda b, i: (b, 0, kw_col)),
        ],
        out_specs=pl.BlockSpec((None, None, nb, BLOCK, BLOCK), lambda b, i: (b, i, 0, 0, 0)),
        out_shape=jax.ShapeDtypeStruct((bsz, nb, nb, BLOCK, BLOCK), F32),
        scratch_shapes=[pltpu.VMEM((lp + BLOCK, 2 * LANES), BF16),
                        pltpu.VMEM((IDX_HEADS * BLOCK, 2 * LANES), BF16),
                        pltpu.VMEM((IDX_HEADS, BLOCK, LANES), F32),
                        pltpu.VMEM((nb + 1, BLOCK, LANES), I32)],
        compiler_params=_cparams(("parallel", "arbitrary")),
        name="indexer",
    )(idx3, idx3, idx3)


def _s5_kernel(x_ref, mw_ref, vre_ref, vim_ref, are_ref, aim_ref, o_ref,
               perm_ref, y_ref, yb_ref, sre_ref, sim_ref, pre_ref, pim_ref, *, bsz):
    tc, c, p = SSM_CHUNK, SSM_GROUP, SSM_STATE
    ng = LANES // c
    nm = tc * c
    rows = x_ref.shape[0] // tc
    per_b = rows // bsz
    width = tc * LANES

    @pl.when(pl.program_id(0) == 0)
    def _():
        src = lax.broadcasted_iota(I32, (width, width), 0)
        dst = lax.broadcasted_iota(I32, (width, width), 1)
        group = (src & (LANES - 1)) >> (c.bit_length() - 1)
        step_in_chunk = src >> (LANES.bit_length() - 1)
        want = group * nm + step_in_chunk * c + (src & (c - 1))
        perm_ref[...] = jnp.where(dst == want, 1.0, 0.0).astype(BF16)

    x_all = jnp.concatenate(
        [x_ref[pl.ds(j, rows, stride=tc), :].astype(BF16) for j in range(tc)], axis=1)
    u_all = _dot(x_all, perm_ref[...]).astype(BF16)
    for g in range(ng):
        r = _dot(u_all[:, g * nm:(g + 1) * nm], mw_ref[g])
        y_ref[:, g * nm:(g + 1) * nm] = r[:, :nm]
        sre_ref[:, g * p:(g + 1) * p] = r[:, nm:nm + p]
        sim_ref[:, g * p:(g + 1) * p] = r[:, nm + p:nm + 2 * p]
    ar = are_ref[...]
    ai = aim_ref[...]

    def step(ti, carry):
        out = []
        for b in range(bsz):
            cr, ci = carry[b]
            rs = pl.ds(pl.multiple_of(b * per_b + ti * SUBLANES, SUBLANES), SUBLANES)
            lr = sre_ref[rs, :]
            li = sim_ref[rs, :]
            before_r, before_i = [], []
            for k in range(SUBLANES):
                before_r.append(cr)
                before_i.append(ci)
                cr, ci = (ar * cr - ai * ci + lr[k:k + 1], ar * ci + ai * cr + li[k:k + 1])
            pre_ref[rs, :] = jnp.concatenate(before_r, axis=0)
            pim_ref[rs, :] = jnp.concatenate(before_i, axis=0)
            out.append((cr, ci))
        return tuple(out)

    zero = jnp.zeros((1, ng * p), F32)
    lax.fori_loop(0, per_b // SUBLANES, step, tuple((zero, zero) for _ in range(bsz)))
    for g in range(ng):
        y = (y_ref[:, g * nm:(g + 1) * nm]
             + _dot(pre_ref[:, g * p:(g + 1) * p].astype(BF16), vre_ref[g])
             + _dot(pim_ref[:, g * p:(g + 1) * p].astype(BF16), vim_ref[g]))
        yb_ref[:, g * nm:(g + 1) * nm] = y.astype(BF16)
    y_all = _dot_nt(yb_ref[...], perm_ref[...])
    for j in range(tc):
        o_ref[pl.ds(j, rows, stride=tc), :] = y_all[:, j * LANES:(j + 1) * LANES]


def _s5_weights(a_re, a_im, log_dt, b_re, b_im, c_re, c_im, d_skip):
    hp = lax.Precision.HIGHEST
    g, p, c, tc = SSM_GROUPS, SSM_STATE, SSM_GROUP, SSM_CHUNK
    dt = jnp.exp(log_dt)[:, None]
    lam_re, lam_im = dt * a_re, dt * a_im
    mag = jnp.exp(lam_re)
    ab_re, ab_im = mag * jnp.cos(lam_im), mag * jnp.sin(lam_im)
    den = a_re * a_re + a_im * a_im
    f_re = ((ab_re - 1.0) * a_re + ab_im * a_im) / den
    f_im = (ab_im * a_re - (ab_re - 1.0) * a_im) / den
    bb_re = f_re[..., None] * b_re - f_im[..., None] * b_im
    bb_im = f_re[..., None] * b_im + f_im[..., None] * b_re
    d = jnp.arange(tc + 1, dtype=F32)[:, None, None]
    pmag = jnp.exp(d * lam_re)
    pw_re, pw_im = pmag * jnp.cos(d * lam_im), pmag * jnp.sin(d * lam_im)
    z_re = pw_re[:tc, :, :, None] * bb_re - pw_im[:tc, :, :, None] * bb_im
    z_im = pw_re[:tc, :, :, None] * bb_im + pw_im[:tc, :, :, None] * bb_re
    kmat = (jnp.einsum('gop,dgpi->gdio', c_re, z_re, precision=hp)
            - jnp.einsum('gop,dgpi->gdio', c_im, z_im, precision=hp))
    ti = jnp.arange(tc)
    lag = ti[None, :] - ti[:, None]
    m5 = kmat[:, jnp.clip(lag, 0, tc - 1)]
    m5 = jnp.where((lag >= 0)[None, :, :, None, None], m5, 0.0)
    m = jnp.transpose(m5, (0, 1, 3, 2, 4)).reshape(g, tc * c, tc * c)
    m = m + jnp.eye(tc * c, dtype=F32) * jnp.tile(d_skip.reshape(g, 1, c), (1, tc, 1)).reshape(g, 1, tc * c)
    w_re = jnp.transpose(z_re[::-1], (1, 0, 3, 2)).reshape(g, tc * c, p)
    w_im = jnp.transpose(z_im[::-1], (1, 0, 3, 2)).reshape(g, tc * c, p)
    mw = jnp.concatenate([m, w_re, w_im], axis=2).astype(BF16)
    q_re, q_im = pw_re[1:], pw_im[1:]
    v_re = c_re[None] * q_re[:, :, None, :] - c_im[None] * q_im[:, :, None, :]
    v_im = c_re[None] * q_im[:, :, None, :] + c_im[None] * q_re[:, :, None, :]
    v_re = jnp.transpose(v_re, (1, 3, 0, 2)).reshape(g, p, tc * c).astype(BF16)
    v_im = jnp.transpose(-v_im, (1, 3, 0, 2)).reshape(g, p, tc * c).astype(BF16)
    return mw, v_re, v_im, pw_re[tc].reshape(1, g * p), pw_im[tc].reshape(1, g * p)


def _s5(su, weights, bsz, lp):
    del lp
    mw, v_re, v_im, a_re, a_im = weights
    t = su.shape[0]
    c, tc, p = SSM_GROUP, SSM_CHUNK, SSM_STATE
    ng = LANES // c
    rows = t // tc
    return pl.pallas_call(
        functools.partial(_s5_kernel, bsz=bsz),
        grid=(SSM_WIDTH // LANES,),
        in_specs=[
            pl.BlockSpec((t, LANES), lambda i: (0, i)),
            pl.BlockSpec((ng, tc * c, tc * c + 2 * p), lambda i: (i, 0, 0)),
            pl.BlockSpec((ng, p, tc * c), lambda i: (i, 0, 0)),
            pl.BlockSpec((ng, p, tc * c), lambda i: (i, 0, 0)),
            pl.BlockSpec((1, ng * p), lambda i: (0, i)),
            pl.BlockSpec((1, ng * p), lambda i: (0, i)),
        ],
        out_specs=pl.BlockSpec((t, LANES), lambda i: (0, i)),
        out_shape=jax.ShapeDtypeStruct((t, SSM_WIDTH), F32),
        scratch_shapes=[pltpu.VMEM((tc * LANES, tc * LANES), BF16),
                        pltpu.VMEM((rows, tc * LANES), F32), pltpu.VMEM((rows, tc * LANES), BF16),
                        pltpu.VMEM((rows, ng * p), F32), pltpu.VMEM((rows, ng * p), F32),
                        pltpu.VMEM((rows, ng * p), F32), pltpu.VMEM((rows, ng * p), F32)],
        compiler_params=_cparams(("arbitrary",)),
        name="s5",
    )(su, mw, v_re, v_im, a_re, a_im)


def _ffn_kernel(be_ref, nu_ref, x_ref, w1_ref, w3_ref, w2_ref, o_ref):
    i = pl.program_id(0)
    f = pl.program_id(1)

    @pl.when(f == 0)
    def _():
        o_ref[...] = jnp.zeros(o_ref.shape, F32)

    @pl.when(i < nu_ref[0])
    def _():
        x = x_ref[...].astype(BF16)
        a = _dot(x, w1_ref[0].astype(BF16))
        b = _dot(x, w3_ref[0].astype(BF16))
        act = (a * jax.nn.sigmoid(a) * b).astype(BF16)
        o_ref[...] += _dot(act, w2_ref[0].astype(BF16))


def _ffn(x, w1, w3, w2, blk_expert, n_used, tm):
    n, d = x.shape
    dff = w1.shape[2]
    tf = _pick(dff, (256, 128))
    nf = dff // tf

    def live(i, nu):
        return jnp.minimum(i, nu[0] - 1)

    def fidx(i, f, nu):
        return jnp.where(i < nu[0], f, nf - 1)

    grid_spec = pltpu.PrefetchScalarGridSpec(
        num_scalar_prefetch=2,
        grid=(n // tm, nf),
        in_specs=[
            pl.BlockSpec((tm, d), lambda i, f, be, nu: (live(i, nu), 0)),
            pl.BlockSpec((1, d, tf), lambda i, f, be, nu: (be[live(i, nu)], 0, fidx(i, f, nu))),
            pl.BlockSpec((1, d, tf), lambda i, f, be, nu: (be[live(i, nu)], 0, fidx(i, f, nu))),
            pl.BlockSpec((1, tf, d), lambda i, f, be, nu: (be[live(i, nu)], fidx(i, f, nu), 0)),
        ],
        out_specs=pl.BlockSpec((tm, d), lambda i, f, be, nu: (i, 0)),
    )
    return pl.pallas_call(
        _ffn_kernel,
        grid_spec=grid_spec,
        out_shape=jax.ShapeDtypeStruct((n, d), F32),
        compiler_params=_cparams(("arbitrary", "arbitrary")),
        name="swiglu",
    )(blk_expert, n_used, x, w1, w3, w2)


def _router_kernel(h_ref, w_ref, idx_ref, gate_ref):
    hh, hl = _split_bf16(h_ref[...])
    wh, wl = _split_bf16(w_ref[...])
    logits = _dot_nt(wh, hh) + _dot_nt(wl, hh) + _dot_nt(wh, hl)
    e = lax.broadcasted_iota(I32, logits.shape, 0).astype(F32)
    m1 = jnp.max(logits, axis=0, keepdims=True)
    i1 = jnp.min(jnp.where(logits == m1, e, float(N_EXPERTS)), axis=0, keepdims=True)
    rest = jnp.where(e == i1, -jnp.inf, logits)
    m2 = jnp.max(rest, axis=0, keepdims=True)
    i2 = jnp.min(jnp.where(rest == m2, e, float(N_EXPERTS)), axis=0, keepdims=True)
    e2 = jnp.exp(m2 - m1)
    den = 1.0 + e2
    idx_ref[...] = jnp.concatenate([i1, i2], axis=0).astype(I32)
    gate_ref[...] = jnp.concatenate([1.0 / den, e2 / den], axis=0)


def _router(h, w_router_t):
    t, d = h.shape
    tm = _pick(t, (384, 256, 128))
    return pl.pallas_call(
        _router_kernel,
        grid=(t // tm,),
        in_specs=[pl.BlockSpec((tm, d), lambda i: (i, 0)), pl.BlockSpec((N_EXPERTS, d), lambda i: (0, 0))],
        out_specs=[pl.BlockSpec((TOP_K_EXPERTS, tm), lambda i: (0, i)),
                   pl.BlockSpec((TOP_K_EXPERTS, tm), lambda i: (0, i))],
        out_shape=[jax.ShapeDtypeStruct((TOP_K_EXPERTS, t), I32),
                   jax.ShapeDtypeStruct((TOP_K_EXPERTS, t), F32)],
        compiler_params=_cparams(("parallel",)),
        name="router",
    )(h, w_router_t)


def _gather_kernel(idx_ref, src_ref, o_ref, sem, *, tg):
    base = pl.program_id(0) * tg

    def row_copy(src_row, r):
        return pltpu.make_async_copy(src_ref.at[pl.ds(src_row, 1), :], o_ref.at[pl.ds(r, 1), :], sem)

    def issue(r8, carry):
        for k in range(GATHER_UNROLL):
            r = r8 * GATHER_UNROLL + k
            row_copy(idx_ref[base + r], r).start()
        return carry

    lax.fori_loop(0, tg // GATHER_UNROLL, issue, 0)

    def wait(r8, carry):
        for k in range(GATHER_UNROLL):
            row_copy(0, r8 * GATHER_UNROLL + k).wait()
        return carry

    lax.fori_loop(0, tg // GATHER_UNROLL, wait, 0)


def _gather_rows(src, idx, tg):
    n = idx.shape[0]
    d = src.shape[1]
    grid_spec = pltpu.PrefetchScalarGridSpec(
        num_scalar_prefetch=1,
        grid=(n // tg,),
        in_specs=[pl.BlockSpec(memory_space=pl.ANY)],
        out_specs=pl.BlockSpec((tg, d), lambda i, idx_ref: (i, 0)),
        scratch_shapes=[pltpu.SemaphoreType.DMA(())],
    )
    return pl.pallas_call(
        functools.partial(_gather_kernel, tg=tg),
        grid_spec=grid_spec,
        out_shape=jax.ShapeDtypeStruct((n, d), src.dtype),
        compiler_params=_cparams(("arbitrary",)),
        name="gather_rows",
    )(idx, src)


def _moe(h, w_router, w1, w3, w2, e_base, ln_g, ln_b):
    t, d = h.shape
    tm = MOE_ROWS
    top_idx, gates = _router(h, w_router.T)
    n_assign = t * TOP_K_EXPERTS
    expert = top_idx.reshape(-1)
    onehot = (expert[:, None] == jnp.arange(N_EXPERTS)[None, :]).astype(I32)
    rank = jnp.sum((jnp.cumsum(onehot, axis=0) - onehot) * onehot, axis=1)
    counts = jnp.sum(onehot, axis=0)
    padded = (counts + tm - 1) // tm * tm
    pad_end = jnp.cumsum(padded)
    dest = (pad_end - padded)[expert] + rank
    n_blk = -(-(n_assign + N_EXPERTS * (tm - 1)) // tm)
    n_rows = n_blk * tm
    token = jnp.arange(n_assign, dtype=I32) % t
    row_token = jnp.zeros((n_rows,), I32).at[dest].set(token)
    blk_expert = jnp.minimum(
        jnp.searchsorted(pad_end, jnp.arange(n_blk) * tm, side='right'), N_EXPERTS - 1).astype(I32)
    n_used = (pad_end[-1:] // tm).astype(I32)
    xs = _gather_rows(h, row_token, 256)
    ys = _ffn(xs, w1, w3, w2, blk_expert + e_base, n_used, tm)
    y01 = _gather_rows(ys, dest.astype(I32), 256)
    return _moe_combine_ln(h, y01, gates.T, ln_g, ln_b)


def _rope_tables(lp, n_heads, head_dim, off, rot_dim, scale=1.0):
    r = rot_dim // 2
    pos = jnp.arange(lp)
    inv = ROPE_THETA ** (-jnp.arange(0, rot_dim, 2, dtype=F32) / rot_dim)
    ang = pos.astype(F32)[:, None] * inv[None, :]
    cos, sin = jnp.cos(ang), jnp.sin(ang)
    c = jnp.ones((lp, head_dim), F32).at[:, off:off + r].set(cos).at[:, off + r:off + 2 * r].set(cos)
    sa = jnp.zeros((lp, head_dim), F32).at[:, off:off + r].set(-sin)
    sb = jnp.zeros((lp, head_dim), F32).at[:, off + r:off + 2 * r].set(sin)
    return tuple(jnp.tile(x * scale, (1, n_heads)) for x in (c, sa, sb)) + (r,)


def _pad_cols(w, n):
    return jnp.pad(w, ((0, 0), (0, n - w.shape[1])))


def kernel(x, meta, ln_in_g, ln_in_b, w_in, b_gate, mla_q_norm, mla_kv_norm, w_uq, w_ukv, ssm_a_re, ssm_a_im, ssm_log_dt, ssm_b_re, ssm_b_im, ssm_c_re, ssm_c_im, ssm_d, w_glu, w_branch_a, w_branch_b, w_branch_c, w_o, ln1_g, ln1_b, ffn_w1, ffn_w3, ffn_w2, w_router, moe_w1, moe_w3, moe_w2, ln2_g, ln2_b):
    bsz, seq, _ = x.shape
    n_tok = seq + N_META
    lp = -(-n_tok // BLOCK) * BLOCK
    t = bsz * lp
    n_sel = min(TOPK_MAX, seq // 4)
    meta_b = jnp.broadcast_to(meta[None].astype(x.dtype), (bsz, N_META, D_MODEL))
    pad = jnp.zeros((bsz, lp - n_tok, D_MODEL), x.dtype)
    h, hb = _layernorm(jnp.concatenate([meta_b, x, pad], axis=1).reshape(t, D_MODEL), ln_in_g, ln_in_b)

    a_scale = A_HEAD_DIM ** -0.5 * LOG2E
    rope_q = _rope_tables(lp, A_HEADS, A_HEAD_DIM, 0, A_ROT, a_scale)
    rope_k = _rope_tables(lp, A_KV_HEADS, A_HEAD_DIM, 0, A_ROT)
    iq_tab = _rope_tables(lp, IDX_HEADS, IDX_DIM, 0, IDX_ROT)
    ik_tab = _rope_tables(lp, 1, LANES, 0, IDX_ROT)
    w_scale = (jnp.zeros((LANES,), F32).at[:IDX_DIM].set(1.0)
               .at[IDX_DIM:IDX_DIM + IDX_HEADS].set((IDX_HEADS * IDX_DIM) ** -0.5))
    rope_i = tuple(jnp.concatenate([a, b * w_scale[None, :], jnp.zeros((lp, LANES), F32)], axis=1)
                   for a, b in zip(iq_tab[:3], ik_tab[:3])) + (IDX_ROT // 2,)
    n_idx = IDX_HEADS * IDX_DIM + 2 * LANES
    rope_kr = _rope_tables(lp, 1, LANES, 0, MLA_ROPE)
    m_scale = (MLA_NOPE + MLA_ROPE) ** -0.5 * LOG2E
    rope_mq = _rope_tables(lp, MLA_HEADS, 2 * LANES, MLA_NOPE, MLA_ROPE, m_scale)

    offs = [0]
    for s in IN_SIZES:
        offs.append(offs[-1] + s)
    o_aq, o_ak, o_av, o_iq, o_ik, o_iw, o_dq, o_dkv, o_kr, o_su, o_gl, o_end = offs

    for layer in range(DEPTH):
        w_md = _pad_cols(w_in[layer, :, o_dq:o_su], MLA_Q_LORA + MLA_KV_LORA + LANES)
        w_su = w_in[layer, :, o_su:o_gl]
        w_gl = w_in[layer, :, o_gl:o_end]

        q_a = _proj(hb, w_in, w_cols=(layer, o_aq, o_ak - o_aq), out_dtype=BF16, lp=lp, rope=rope_q,
                    name="proj_aq")
        k_a = _proj(hb, w_in, w_cols=(layer, o_ak, o_av - o_ak), out_dtype=BF16, lp=lp, rope=rope_k,
                    name="proj_ak")
        v_a = _proj(hb, w_in, w_cols=(layer, o_av, o_iq - o_av), out_dtype=BF16, lp=lp, name="proj_av")
        idx = _proj(h, w_in, w_cols=(layer, o_iq, n_idx), out_dtype=F32, lp=lp, rope=rope_i, hi_prec=True,
                    tm_prefs=(384, 128), tn_prefs=(n_idx,), name="proj_idx")
        bias = _indexer(idx, bsz, lp, n_sel)
        out_a = _dsa_attention(q_a, k_a, v_a, bias, bsz, lp)

        dqkv, kr = _mla_down(hb, w_md, rope_kr, lp)
        wq = w_uq[layer].reshape(MLA_Q_LORA, MLA_HEADS, MLA_NOPE + MLA_ROPE)
        wq = jnp.pad(wq, ((0, 0), (0, 0), (0, 2 * LANES - MLA_NOPE - MLA_ROPE))).reshape(MLA_Q_LORA, -1)
        q_m = _proj(dqkv, wq, out_dtype=BF16, lp=lp, x_col=0, kdim=MLA_Q_LORA, rope=rope_mq,
                    rms_gain=mla_q_norm[layer], name="proj_mq")
        kv_m = _proj(dqkv, w_ukv, w_cols=(layer, 0, w_ukv.shape[2]), out_dtype=BF16, lp=lp, x_col=1,
                     kdim=MLA_KV_LORA, rms_gain=mla_kv_norm[layer], name="proj_mkv")
        out_b = _mla_attention(q_m, kv_m, kr, bsz, lp)

        su = _proj(hb, w_su, out_dtype=F32, lp=lp, name="proj_su")
        s5w = _s5_weights(ssm_a_re[layer], ssm_a_im[layer], ssm_log_dt[layer], ssm_b_re[layer],
                          ssm_b_im[layer], ssm_c_re[layer], ssm_c_im[layer], ssm_d[layer])
        y = _s5(su, s5w, bsz, lp)
        out_c = _glu(y, w_glu, layer, lp)

        gates = _proj(hb, w_gl, out_dtype=F32, lp=lp, bias=b_gate[layer], sigmoid=True, name="proj_gates")
        merged = _merge(out_a, out_b, out_c, w_branch_a, w_branch_b, w_branch_c, layer, gates, lp)
        mix = _proj(merged, w_o, w_cols=(layer, 0, D_MODEL), out_dtype=F32, lp=lp, name="proj_o")
        h, hb = _layernorm(h, ln1_g[layer], ln1_b[layer], res=mix)

        i = layer // 2
        if layer % 2 == 0:
            tm = _pick(t, (1056, 384, 128))
            nblk = t // tm
            f = _ffn(hb, ffn_w1, ffn_w3, ffn_w2, jnp.full((nblk,), i, I32), jnp.full((1,), nblk, I32), tm)
            h, hb = _layernorm(h, ln2_g[layer], ln2_b[layer], res=f)
        else:
            n_all = moe_w1.shape[0] * N_EXPERTS
            h, hb = _moe(h, w_router[i], moe_w1.reshape((n_all,) + moe_w1.shape[2:]),
                         moe_w3.reshape((n_all,) + moe_w3.shape[2:]),
                         moe_w2.reshape((n_all,) + moe_w2.shape[2:]), i * N_EXPERTS,
                         ln2_g[layer], ln2_b[layer])

    return h.reshape(bsz, lp, D_MODEL)[:, N_META:N_META + seq]
```

```python
import functools
import math

import jax
import jax.numpy as jnp
from jax import lax
from jax.experimental import pallas as pl
from jax.experimental.pallas import tpu as pltpu

F32 = jnp.float32
BF16 = jnp.bfloat16
I32 = jnp.int32

D_MODEL = 2048
DEPTH = 4
N_META = 16
BLOCK = 128
ROPE_THETA = 500000.0
LN_EPS = 1e-5
RMS_EPS = 1e-6
ALPHA = (2 * DEPTH) ** 0.25

A_HEADS = 8
A_KV_HEADS = 2
A_HEAD_DIM = 128
A_ROT = A_HEAD_DIM // 4
IDX_HEADS = 8
IDX_DIM = 64
IDX_ROT = IDX_DIM // 4
TOPK_MAX = 256

MLA_HEADS = 8
MLA_Q_LORA = 512
MLA_KV_LORA = 512
MLA_NOPE = 128
MLA_ROPE = 64
MLA_V = 128

SSM_WIDTH = 1024
SSM_GROUP = 16
SSM_GROUPS = SSM_WIDTH // SSM_GROUP
SSM_STATE = 64
SSM_CHUNK = 16

N_BRANCH = 3
A_WIDTH = A_HEADS * A_HEAD_DIM
MLA_WIDTH = MLA_HEADS * MLA_V
IN_SIZES = (A_HEADS * A_HEAD_DIM, A_KV_HEADS * A_HEAD_DIM, A_KV_HEADS * A_HEAD_DIM,
            IDX_HEADS * IDX_DIM, IDX_DIM, IDX_HEADS,
            MLA_Q_LORA, MLA_KV_LORA, MLA_ROPE,
            SSM_WIDTH, N_BRANCH * D_MODEL)

D_FF = 5632
N_EXPERTS = 8
TOP_K_EXPERTS = 2

LANES = 128
SUBLANES = 8
VMEM_LIMIT = 56 * 1024 * 1024
INT_MIN = -2 ** 31
LOG2E = math.log2(math.e)
GATHER_UNROLL = 8
MOE_ROWS = 768


def _pick(n, prefs):
    for p in prefs:
        if n % p == 0:
            return p
    raise ValueError(f"no tile in {prefs} divides {n}")


def _cparams(sem):
    return pltpu.CompilerParams(dimension_semantics=sem, vmem_limit_bytes=VMEM_LIMIT)


def _dot(a, b):
    return jnp.dot(a, b, preferred_element_type=F32)


def _dot_nt(a, b):
    return lax.dot_general(a, b, (((1,), (1,)), ((), ())), preferred_element_type=F32)


def _split_bf16(x):
    hi = x.astype(BF16)
    lo = (x - hi.astype(F32)).astype(BF16)
    return hi, lo


def _ln_math(x, g, b):
    mu = jnp.mean(x, axis=-1, keepdims=True)
    xc = x - mu
    var = jnp.mean(xc * xc, axis=-1, keepdims=True)
    return xc * lax.rsqrt(var + LN_EPS) * g + b


def _ln_kernel(x_ref, g_ref, b_ref, o_ref, ob_ref):
    y = _ln_math(x_ref[...], g_ref[...], b_ref[...])
    o_ref[...] = y
    ob_ref[...] = y.astype(BF16)


def _ln_res_kernel(h_ref, r_ref, g_ref, b_ref, o_ref, ob_ref):
    y = _ln_math(ALPHA * h_ref[...] + r_ref[...], g_ref[...], b_ref[...])
    o_ref[...] = y
    ob_ref[...] = y.astype(BF16)


def _moe_ln_kernel(h_ref, y_ref, gt_ref, g_ref, b_ref, o_ref, ob_ref):
    gt = gt_ref[...]
    f = y_ref[0] * gt[:, 0:1] + y_ref[1] * gt[:, 1:2]
    y = _ln_math(ALPHA * h_ref[...] + f, g_ref[...], b_ref[...])
    o_ref[...] = y
    ob_ref[...] = y.astype(BF16)


def _layernorm(x, g, b, res=None):
    t, d = x.shape
    tm = _pick(t, (384, 256, 128))
    row = pl.BlockSpec((tm, d), lambda i: (i, 0))
    vec = pl.BlockSpec((1, d), lambda i: (0, 0))
    ins = [x] if res is None else [x, res]
    return pl.pallas_call(
        _ln_kernel if res is None else _ln_res_kernel,
        grid=(t // tm,),
        in_specs=[row] * len(ins) + [vec, vec],
        out_specs=[row, row],
        out_shape=[jax.ShapeDtypeStruct((t, d), F32), jax.ShapeDtypeStruct((t, d), BF16)],
        compiler_params=_cparams(("parallel",)),
        name="layernorm",
    )(*ins, g.reshape(1, d), b.reshape(1, d))


def _moe_combine_ln(h, y01, gates_t, g, b):
    t, d = h.shape
    tm = _pick(t, (384, 256, 128))
    nb = t // tm
    row = pl.BlockSpec((tm, d), lambda i: (i, 0))
    vec = pl.BlockSpec((1, d), lambda i: (0, 0))
    return pl.pallas_call(
        _moe_ln_kernel,
        grid=(nb,),
        in_specs=[row, pl.BlockSpec((TOP_K_EXPERTS, tm, d), lambda i: (0, i, 0)),
                  pl.BlockSpec((tm, TOP_K_EXPERTS), lambda i: (i, 0)), vec, vec],
        out_specs=[row, row],
        out_shape=[jax.ShapeDtypeStruct((t, d), F32), jax.ShapeDtypeStruct((t, d), BF16)],
        compiler_params=_cparams(("parallel",)),
        name="moe_combine_ln",
    )(h, y01.reshape(TOP_K_EXPERTS, t, d), gates_t, g.reshape(1, d), b.reshape(1, d))


def _rope_chunk(a, c, sa, sb, r):
    return a * c + pltpu.roll(a, LANES - r, 1) * sa + pltpu.roll(a, r, 1) * sb


def _proj_kernel(*refs, rope_r, has_bias, sigmoid, hi_prec, rms):
    it = iter(refs)
    x_ref = next(it)
    w_ref = next(it)
    n_ref = next(it) if rms else None
    tabs = (next(it), next(it), next(it)) if rope_r else None
    b_ref = next(it) if has_bias else None
    o_ref = next(it)
    x = x_ref[...]
    if rms:
        ms = jnp.mean(x * x, axis=-1, keepdims=True)
        x = x * lax.rsqrt(ms + RMS_EPS) * n_ref[...]
    if hi_prec:
        xh, xl = _split_bf16(x)
        wh, wl = _split_bf16(w_ref[...])
        acc = _dot(xh, wh) + _dot(xl, wh) + _dot(xh, wl)
    else:
        acc = _dot(x.astype(BF16), w_ref[...].astype(BF16))
    if has_bias:
        acc = acc + b_ref[...]
    if sigmoid:
        acc = jax.nn.sigmoid(acc)
    if rope_r:
        c_ref, sa_ref, sb_ref = tabs
        pw = c_ref.shape[1]
        for c in range(acc.shape[1] // LANES):
            sl = slice(c * LANES, (c + 1) * LANES)
            ts = slice((c * LANES) % pw, (c * LANES) % pw + LANES)
            o_ref[:, sl] = _rope_chunk(acc[:, sl], c_ref[:, ts], sa_ref[:, ts], sb_ref[:, ts],
                                       rope_r).astype(o_ref.dtype)
    else:
        o_ref[...] = acc.astype(o_ref.dtype)


def _proj(x, w, *, out_dtype, lp, x_col=0, kdim=None, w_cols=None, rope=None, bias=None, sigmoid=False,
          hi_prec=False, rms_gain=None, tm_prefs=(1056, 768, 384, 128), tn_prefs=(512, 384, 256, 128),
          name="proj"):
    t = x.shape[0]
    kdim = x.shape[1] if kdim is None else kdim
    tm = _pick(lp, tm_prefs)
    nrow = lp // tm
    if w_cols is None:
        n = w.shape[1]
        tn = _pick(n, tn_prefs)
        w_spec = pl.BlockSpec((kdim, tn), lambda i, j: (0, j))
    else:
        layer, col0, n = w_cols
        tn = _pick(math.gcd(n, col0) if col0 else n, tn_prefs)
        cb0 = col0 // tn
        w_spec = pl.BlockSpec((None, kdim, tn), lambda i, j: (layer, 0, cb0 + j))
    in_specs = [pl.BlockSpec((tm, kdim), lambda i, j: (i, x_col)), w_spec]
    ins = [x, w]
    if rms_gain is not None:
        in_specs.append(pl.BlockSpec((1, kdim), lambda i, j: (0, 0)))
        ins.append(rms_gain.reshape(1, kdim))
    if rope is not None:
        pw = rope[0].shape[1]
        if pw == n:
            tab = pl.BlockSpec((tm, tn), lambda i, j: (i % nrow, j))
        else:
            assert tn % pw == 0
            tab = pl.BlockSpec((tm, pw), lambda i, j: (i % nrow, 0))
        in_specs += [tab, tab, tab]
        ins += list(rope[:3])
    if bias is not None:
        in_specs.append(pl.BlockSpec((1, tn), lambda i, j: (0, j)))
        ins.append(bias.reshape(1, n))
    kern = functools.partial(_proj_kernel, rope_r=rope[3] if rope is not None else 0,
                             has_bias=bias is not None, sigmoid=sigmoid, hi_prec=hi_prec,
                             rms=rms_gain is not None)
    return pl.pallas_call(
        kern,
        grid=(t // tm, n // tn),
        in_specs=in_specs,
        out_specs=pl.BlockSpec((tm, tn), lambda i, j: (i, j)),
        out_shape=jax.ShapeDtypeStruct((t, n), out_dtype),
        compiler_params=_cparams(("parallel", "arbitrary")),
        name=name,
    )(*ins)


def _mla_down_kernel(x_ref, w_ref, c_ref, sa_ref, sb_ref, o_ref, kr_ref):
    acc = _dot(x_ref[...], w_ref[...].astype(BF16))
    nq = o_ref.shape[1]
    o_ref[...] = acc[:, :nq]
    kr_ref[...] = _rope_chunk(acc[:, nq:], c_ref[...], sa_ref[...], sb_ref[...],
                              MLA_ROPE // 2).astype(BF16)


def _mla_down(hb, w, rope, lp):
    t, d = hb.shape
    n = w.shape[1]
    nq = MLA_Q_LORA + MLA_KV_LORA
    tm = _pick(lp, (704, 384, 128))
    nrow = lp // tm
    tab = pl.BlockSpec((tm, LANES), lambda i: (i % nrow, 0))
    return pl.pallas_call(
        _mla_down_kernel,
        grid=(t // tm,),
        in_specs=[pl.BlockSpec((tm, d), lambda i: (i, 0)), pl.BlockSpec((d, n), lambda i: (0, 0)),
                  tab, tab, tab],
        out_specs=[pl.BlockSpec((tm, nq), lambda i: (i, 0)), pl.BlockSpec((tm, LANES), lambda i: (i, 0))],
        out_shape=[jax.ShapeDtypeStruct((t, nq), F32), jax.ShapeDtypeStruct((t, LANES), BF16)],
        compiler_params=_cparams(("parallel",)),
        name="mla_down",
    )(hb, w, *rope[:3])


def _glu_kernel(y_ref, w_ref, o_ref):
    n = o_ref.shape[1]
    y = y_ref[...].astype(BF16)
    ga = _dot(y, w_ref[:, :n].astype(BF16))
    gb = _dot(y, w_ref[:, n:].astype(BF16))
    o_ref[...] = (ga * jax.nn.sigmoid(gb)).astype(o_ref.dtype)


def _glu(y, w_glu, layer, lp):
    t, k = y.shape
    n = w_glu.shape[2] // 2
    tm = _pick(lp, (704, 384, 128))
    return pl.pallas_call(
        _glu_kernel,
        grid=(t // tm,),
        in_specs=[pl.BlockSpec((tm, k), lambda i: (i, 0)),
                  pl.BlockSpec((None, k, 2 * n), lambda i: (layer, 0, 0))],
        out_specs=pl.BlockSpec((tm, n), lambda i: (i, 0)),
        out_shape=jax.ShapeDtypeStruct((t, n), BF16),
        compiler_params=_cparams(("parallel",)),
        name="glu",
    )(y, w_glu)


def _merge_kernel(a_ref, b_ref, c_ref, wa_ref, wb_ref, wc_ref, g0_ref, g1_ref, g2_ref, o_ref):
    m = (g0_ref[...] * _dot(a_ref[...], wa_ref[...].astype(BF16))
         + g1_ref[...] * _dot(b_ref[...], wb_ref[...].astype(BF16))
         + g2_ref[...] * _dot(c_ref[...], wc_ref[...].astype(BF16)))
    o_ref[...] = m.astype(o_ref.dtype)


def _merge(out_a, out_b, out_c, wb_a, wb_b, wb_c, layer, gates, lp):
    t = out_a.shape[0]
    n = wb_a.shape[2]
    tm = _pick(lp, (704, 384, 128))
    tn = _pick(n, (512, 256, 128))
    nj = n // tn

    def act(arr):
        return pl.BlockSpec((tm, arr.shape[1]), lambda i, j: (i, 0))

    def wgt(arr):
        return pl.BlockSpec((None, arr.shape[1], tn), lambda i, j: (layer, 0, j))

    def gate(br):
        return pl.BlockSpec((tm, tn), lambda i, j: (i, j + br * nj))

    return pl.pallas_call(
        _merge_kernel,
        grid=(t // tm, nj),
        in_specs=[act(out_a), act(out_b), act(out_c), wgt(wb_a), wgt(wb_b), wgt(wb_c),
                  gate(0), gate(1), gate(2)],
        out_specs=pl.BlockSpec((tm, tn), lambda i, j: (i, j)),
        out_shape=jax.ShapeDtypeStruct((t, n), BF16),
        compiler_params=_cparams(("parallel", "arbitrary")),
        name="merge",
    )(out_a, out_b, out_c, wb_a, wb_b, wb_c, gates, gates, gates)


def _softmax_update(s, v, m_ref, l_ref, acc_ref):
    tk = s.shape[1]
    m_prev = m_ref[...]
    m_new = jnp.maximum(m_prev, jnp.max(s, axis=1, keepdims=True))
    alpha = jnp.exp2(m_prev - m_new)
    p = jnp.exp2(s - jnp.concatenate([m_new] * (tk // LANES), axis=1))
    l_ref[...] = alpha * l_ref[...] + jnp.sum(p, axis=1, keepdims=True)
    acc_ref[...] = alpha * acc_ref[...] + _dot(p.astype(BF16), v)
    m_ref[...] = m_new


def _mla_attn_kernel(q_ref, kv_ref, kr_ref, o_ref, s_ref, m_ref, l_ref, acc_ref):
    qi = pl.program_id(2)
    tq = q_ref.shape[0]
    m_ref[...] = jnp.full(m_ref.shape, -jnp.inf, F32)
    l_ref[...] = jnp.zeros(l_ref.shape, F32)
    acc_ref[...] = jnp.zeros(acc_ref.shape, F32)
    q = q_ref[...]

    def rows_of(kt):
        return pl.ds(pl.multiple_of(kt * tq, tq), tq)

    def scores(kt):
        rows = rows_of(kt)
        return _dot_nt(q, jnp.concatenate([kv_ref[rows, :MLA_NOPE], kr_ref[rows, :]], axis=1))

    s_ref[...] = scores(0)

    def body(kt, carry):
        s = s_ref[...]
        s_ref[...] = scores(kt + 1)
        _softmax_update(s, kv_ref[rows_of(kt), MLA_NOPE:], m_ref, l_ref, acc_ref)
        return carry

    lax.fori_loop(0, qi, body, 0)
    row = lax.broadcasted_iota(I32, (tq, tq), 0)
    col = lax.broadcasted_iota(I32, (tq, tq), 1)
    s = jnp.where(col <= row, s_ref[...], -jnp.inf)
    _softmax_update(s, kv_ref[rows_of(qi), MLA_NOPE:], m_ref, l_ref, acc_ref)
    o_ref[...] = (acc_ref[...] / l_ref[...]).astype(o_ref.dtype)


def _mla_attention(q, kv, kr, bsz, lp):
    t = q.shape[0]
    tq = _pick(lp, (384, 128))
    nq = lp // tq
    q3 = q.reshape(bsz, lp, q.shape[1])
    kv3 = kv.reshape(bsz, lp, kv.shape[1])
    kr3 = kr.reshape(bsz, lp, LANES)
    out = pl.pallas_call(
        _mla_attn_kernel,
        grid=(bsz, MLA_HEADS, nq),
        in_specs=[
            pl.BlockSpec((None, tq, 2 * LANES), lambda b, h, i: (b, i, h)),
            pl.BlockSpec((None, lp, MLA_NOPE + MLA_V), lambda b, h, i: (b, 0, h)),
            pl.BlockSpec((None, lp, LANES), lambda b, h, i: (b, 0, 0)),
        ],
        out_specs=pl.BlockSpec((None, tq, LANES), lambda b, h, i: (b, i, h)),
        out_shape=jax.ShapeDtypeStruct((bsz, lp, MLA_WIDTH), BF16),
        scratch_shapes=[pltpu.VMEM((tq, tq), F32),
                        pltpu.VMEM((tq, LANES), F32), pltpu.VMEM((tq, LANES), F32),
                        pltpu.VMEM((tq, MLA_V), F32)],
        compiler_params=_cparams(("parallel", "parallel", "arbitrary")),
        name="mla_attention",
    )(q3, kv3, kr3)
    return out.reshape(t, MLA_WIDTH)


def _dsa_attn_kernel(q_ref, k_ref, v_ref, bias_ref, o_ref, qg_ref, s_ref, m_ref, l_ref, acc_ref, *, tk):
    qi = pl.program_id(1)
    rep = A_HEADS // A_KV_HEADS
    nkt = tk // BLOCK
    m_ref[...] = jnp.full(m_ref.shape, -jnp.inf, F32)
    l_ref[...] = jnp.zeros(l_ref.shape, F32)
    acc_ref[...] = jnp.zeros(acc_ref.shape, F32)
    for g in range(A_KV_HEADS):
        for r in range(rep):
            h = g * rep + r
            qg_ref[g, r * BLOCK:(r + 1) * BLOCK, :] = q_ref[:, h * A_HEAD_DIM:(h + 1) * A_HEAD_DIM]

    def rows_of(kt):
        return pl.ds(pl.multiple_of(kt * tk, tk), tk)

    def scores(kt):
        rows = rows_of(kt)
        for g in range(A_KV_HEADS):
            s_ref[g] = _dot_nt(qg_ref[g], k_ref[rows, g * A_HEAD_DIM:(g + 1) * A_HEAD_DIM])

    def consume(kt, s):
        bias = jnp.concatenate([bias_ref[kt * nkt + j] for j in range(nkt)], axis=1)
        bias = jnp.concatenate([bias] * rep, axis=0)
        for g in range(A_KV_HEADS):
            _softmax_update(s[g] + bias, v_ref[rows_of(kt), g * A_HEAD_DIM:(g + 1) * A_HEAD_DIM],
                            m_ref.at[g], l_ref.at[g], acc_ref.at[g])

    scores(0)

    def body(kt, carry):
        s = [s_ref[g] for g in range(A_KV_HEADS)]
        scores(kt + 1)
        consume(kt, s)
        return carry

    last = (qi * BLOCK) // tk
    lax.fori_loop(0, last, body, 0)
    consume(last, [s_ref[g] for g in range(A_KV_HEADS)])
    for g in range(A_KV_HEADS):
        o = acc_ref[g] / l_ref[g]
        for r in range(rep):
            h = g * rep + r
            o_ref[:, h * A_HEAD_DIM:(h + 1) * A_HEAD_DIM] = (
                o[r * BLOCK:(r + 1) * BLOCK]).astype(o_ref.dtype)


def _dsa_attention(q, k, v, bias, bsz, lp):
    t = q.shape[0]
    nb = lp // BLOCK
    tk = _pick(lp, (384, 128))
    kvw = A_KV_HEADS * A_HEAD_DIM
    q3 = q.reshape(bsz, lp, A_WIDTH)
    k3 = k.reshape(bsz, lp, kvw)
    v3 = v.reshape(bsz, lp, kvw)
    rows = (A_HEADS // A_KV_HEADS) * BLOCK
    out = pl.pallas_call(
        functools.partial(_dsa_attn_kernel, tk=tk),
        grid=(bsz, nb),
        in_specs=[
            pl.BlockSpec((None, BLOCK, A_WIDTH), lambda b, i: (b, i, 0)),
            pl.BlockSpec((None, lp, kvw), lambda b, i: (b, 0, 0)),
            pl.BlockSpec((None, lp, kvw), lambda b, i: (b, 0, 0)),
            pl.BlockSpec((None, None, nb, BLOCK, BLOCK), lambda b, i: (b, i, 0, 0, 0)),
        ],
        out_specs=pl.BlockSpec((None, BLOCK, A_WIDTH), lambda b, i: (b, i, 0)),
        out_shape=jax.ShapeDtypeStruct((bsz, lp, A_WIDTH), BF16),
        scratch_shapes=[pltpu.VMEM((A_KV_HEADS, rows, A_HEAD_DIM), BF16),
                        pltpu.VMEM((A_KV_HEADS, rows, tk), F32),
                        pltpu.VMEM((A_KV_HEADS, rows, LANES), F32),
                        pltpu.VMEM((A_KV_HEADS, rows, LANES), F32),
                        pltpu.VMEM((A_KV_HEADS, rows, A_HEAD_DIM), F32)],
        compiler_params=_cparams(("parallel", "arbitrary")),
        name="dsa_attention",
    )(q3, k3, v3, bias)
    return out.reshape(t, A_WIDTH)


def _indexer_kernel(iq_ref, kw_ref, o_ref, kcat_ref, qcat_ref, wb_ref, key_ref, *, n_sel):
    qb = pl.program_id(0)
    bsz, nb = o_ref.shape[0], o_ref.shape[1]
    batch = range(bsz)
    half = IDX_DIM
    lane = lax.broadcasted_iota(I32, (BLOCK, LANES), 1)
    row = lax.broadcasted_iota(I32, (BLOCK, LANES), 0)

    @pl.when(qb == 0)
    def _():
        for b in batch:
            kf = kw_ref[b]
            klane = lax.broadcasted_iota(I32, kf.shape, 1)
            kz = jnp.where(klane < half, kf, 0.0)
            hi = kz.astype(BF16).astype(F32)
            lo = kz - hi
            lp = kf.shape[0]
            kcat_ref[b, 0:lp, :] = jnp.concatenate(
                [(hi + pltpu.roll(hi, half, 1)).astype(BF16), lo.astype(BF16)], axis=1)
            kcat_ref[b, lp:lp + BLOCK, :] = jnp.zeros((BLOCK, 2 * LANES), BF16)

    for b in batch:
        wq = kw_ref[b, pl.ds(pl.multiple_of(qb * BLOCK, BLOCK), BLOCK), :]
        for h in range(IDX_HEADS):
            chunk = iq_ref[b, :, (h // 2) * LANES:(h // 2 + 1) * LANES]
            if h % 2 == 0:
                a = jnp.where(lane < half, chunk, 0.0)
            else:
                a = pltpu.roll(jnp.where(lane >= half, chunk, 0.0), half, 1)
            hi = a.astype(BF16).astype(F32)
            lo = a - hi
            qcat_ref[b, h * BLOCK:(h + 1) * BLOCK, :] = jnp.concatenate(
                [(hi + pltpu.roll(lo, half, 1)).astype(BF16), hi.astype(BF16)], axis=1)
            wb_ref[b, h] = jnp.broadcast_to(wq[:, half + h:half + h + 1], (BLOCK, LANES))

    qpos = qb * BLOCK + row

    def sortable(x):
        b = pltpu.bitcast(x, I32)
        return b ^ ((b >> 31) & 0x7FFFFFFF)

    n_pair = (qb + 2) // 2

    def score_pair(kp, carry):
        for b in batch:
            kblk = kcat_ref[b, pl.ds(pl.multiple_of(kp * 2 * BLOCK, 2 * BLOCK), 2 * BLOCK), :]
            s = _dot_nt(qcat_ref[b], kblk)
            for j in range(2):
                sc = jnp.zeros((BLOCK, LANES), F32)
                for h in range(IDX_HEADS):
                    sc = sc + (jnp.maximum(s[h * BLOCK:(h + 1) * BLOCK, j * LANES:(j + 1) * LANES], 0.0)
                               * wb_ref[b, h])
                kpos = (kp * 2 + j) * BLOCK + lane
                sc = jnp.where(kpos < N_META, jnp.inf, sc)
                sc = jnp.where(kpos <= qpos, sc, -jnp.inf)
                key_ref[b, kp * 2 + j] = sortable(sc)
        return carry

    lax.fori_loop(0, n_pair, score_pair, 0)

    def count(pred):
        def body(kp, cs):
            return tuple(c + jnp.where(pred(key_ref[b, kp * 2], b), 1.0, 0.0)
                         + jnp.where(pred(key_ref[b, kp * 2 + 1], b), 1.0, 0.0)
                         for b, c in zip(batch, cs))
        cs = lax.fori_loop(0, n_pair, body, tuple(jnp.zeros((BLOCK, LANES), F32) for _ in batch))
        return tuple(jnp.sum(c, axis=1, keepdims=True) for c in cs)

    def bit_step(i, thrs):
        bit = 31 - i
        cands = tuple(jnp.where(bit == 31, thr ^ INT_MIN, thr | (1 << jnp.minimum(bit, 30)))
                      for thr in thrs)
        counts = count(lambda k, b: k >= cands[b])
        return tuple(jnp.where(n >= n_sel, cand, thr) for n, cand, thr in zip(counts, cands, thrs))

    thrs = lax.fori_loop(0, 32, bit_step,
                         tuple(jnp.full((BLOCK, LANES), INT_MIN, I32) for _ in batch))
    n_ge = count(lambda k, b: k >= thrs[b])
    all_ties_fit = functools.reduce(jnp.maximum, [jnp.max(n) for n in n_ge]) <= n_sel

    @pl.when(all_ties_fit)
    def _():
        def emit(kt, carry):
            kpos = kt * BLOCK + lane
            for b in batch:
                o_ref[b, kt] = jnp.where((key_ref[b, kt] >= thrs[b]) & (kpos <= qpos), 0.0, -jnp.inf)
            return carry

        lax.fori_loop(0, qb + 1, emit, 0)

    @pl.when(jnp.logical_not(all_ties_fit))
    def _():
        n_gt = count(lambda k, b: k > thrs[b])
        tri = (lax.broadcasted_iota(I32, (LANES, LANES), 0)
               <= lax.broadcasted_iota(I32, (LANES, LANES), 1)).astype(BF16)

        def emit(kt, takens):
            out = []
            kpos = kt * BLOCK + lane
            for b in batch:
                key = key_ref[b, kt]
                eq = key == thrs[b]
                rank = _dot(jnp.where(eq, 1.0, 0.0).astype(BF16), tri)
                sel = (key > thrs[b]) | (eq & (takens[b] + rank <= n_sel - n_gt[b]))
                o_ref[b, kt] = jnp.where(sel & (kpos <= qpos), 0.0, -jnp.inf)
                out.append(takens[b] + rank[:, LANES - 1:LANES])
            return tuple(out)

        lax.fori_loop(0, qb + 1, emit, tuple(jnp.zeros((BLOCK, 1), F32) for _ in batch))

    def fill(kt, carry):
        for b in batch:
            o_ref[b, kt] = jnp.full((BLOCK, LANES), -jnp.inf, F32)
        return carry

    lax.fori_loop(qb + 1, nb, fill, 0)


def _indexer(idx, bsz, lp, n_sel):
    nb = lp // BLOCK
    idx3 = idx.reshape(bsz, lp, idx.shape[1])
    nq = IDX_HEADS * IDX_DIM
    kw_col = nq // LANES
    return pl.pallas_call(
        functools.partial(_indexer_kernel, n_sel=n_sel),
        grid=(nb,),
        in_specs=[
            pl.BlockSpec((bsz, BLOCK, nq), lambda i: (0, i, 0)),
            pl.BlockSpec((bsz, lp, LANES), lambda i: (0, 0, kw_col)),
        ],
        out_specs=pl.BlockSpec((bsz, None, nb, BLOCK, BLOCK), lambda i: (0, i, 0, 0, 0)),
        out_shape=jax.ShapeDtypeStruct((bsz, nb, nb, BLOCK, BLOCK), F32),
        scratch_shapes=[pltpu.VMEM((bsz, lp + BLOCK, 2 * LANES), BF16),
                        pltpu.VMEM((bsz, IDX_HEADS * BLOCK, 2 * LANES), BF16),
                        pltpu.VMEM((bsz, IDX_HEADS, BLOCK, LANES), F32),
                        pltpu.VMEM((bsz, nb + 1, BLOCK, LANES), I32)],
        compiler_params=_cparams(("arbitrary",)),
        name="indexer",
    )(idx3, idx3)


def _s5_kernel(x_ref, mw_ref, vre_ref, vim_ref, are_ref, aim_ref, o_ref,
               perm_ref, y_ref, yb_ref, sre_ref, sim_ref, pre_ref, pim_ref, *, bsz):
    tc, c, p = SSM_CHUNK, SSM_GROUP, SSM_STATE
    ng = LANES // c
    nm = tc * c
    rows = x_ref.shape[0] // tc
    per_b = rows // bsz
    width = tc * LANES

    @pl.when(pl.program_id(0) == 0)
    def _():
        src = lax.broadcasted_iota(I32, (width, width), 0)
        dst = lax.broadcasted_iota(I32, (width, width), 1)
        group = (src & (LANES - 1)) >> (c.bit_length() - 1)
        step_in_chunk = src >> (LANES.bit_length() - 1)
        want = group * nm + step_in_chunk * c + (src & (c - 1))
        perm_ref[...] = jnp.where(dst == want, 1.0, 0.0).astype(BF16)

    x_all = jnp.concatenate(
        [x_ref[pl.ds(j, rows, stride=tc), :].astype(BF16) for j in range(tc)], axis=1)
    u_all = _dot(x_all, perm_ref[...]).astype(BF16)
    for g in range(ng):
        r = _dot(u_all[:, g * nm:(g + 1) * nm], mw_ref[g])
        y_ref[:, g * nm:(g + 1) * nm] = r[:, :nm]
        sre_ref[:, g * p:(g + 1) * p] = r[:, nm:nm + p]
        sim_ref[:, g * p:(g + 1) * p] = r[:, nm + p:nm + 2 * p]
    ar = are_ref[...]
    ai = aim_ref[...]

    def step(ti, carry):
        out = []
        for b in range(bsz):
            cr, ci = carry[b]
            rs = pl.ds(pl.multiple_of(b * per_b + ti * SUBLANES, SUBLANES), SUBLANES)
            lr = sre_ref[rs, :]
            li = sim_ref[rs, :]
            before_r, before_i = [], []
            for k in range(SUBLANES):
                before_r.append(cr)
                before_i.append(ci)
                cr, ci = (ar * cr - ai * ci + lr[k:k + 1], ar * ci + ai * cr + li[k:k + 1])
            pre_ref[rs, :] = jnp.concatenate(before_r, axis=0)
            pim_ref[rs, :] = jnp.concatenate(before_i, axis=0)
            out.append((cr, ci))
        return tuple(out)

    zero = jnp.zeros((1, ng * p), F32)
    lax.fori_loop(0, per_b // SUBLANES, step, tuple((zero, zero) for _ in range(bsz)))
    for g in range(ng):
        y = (y_ref[:, g * nm:(g + 1) * nm]
             + _dot(pre_ref[:, g * p:(g + 1) * p].astype(BF16), vre_ref[g])
             + _dot(pim_ref[:, g * p:(g + 1) * p].astype(BF16), vim_ref[g]))
        yb_ref[:, g * nm:(g + 1) * nm] = y.astype(BF16)
    y_all = _dot_nt(yb_ref[...], perm_ref[...])
    for j in range(tc):
        o_ref[pl.ds(j, rows, stride=tc), :] = y_all[:, j * LANES:(j + 1) * LANES]


def _s5_weights(a_re, a_im, log_dt, b_re, b_im, c_re, c_im, d_skip):
    hp = lax.Precision.HIGHEST
    g, p, c, tc = SSM_GROUPS, SSM_STATE, SSM_GROUP, SSM_CHUNK
    dt = jnp.exp(log_dt)[:, None]
    lam_re, lam_im = dt * a_re, dt * a_im
    mag = jnp.exp(lam_re)
    ab_re, ab_im = mag * jnp.cos(lam_im), mag * jnp.sin(lam_im)
    den = a_re * a_re + a_im * a_im
    f_re = ((ab_re - 1.0) * a_re + ab_im * a_im) / den
    f_im = (ab_im * a_re - (ab_re - 1.0) * a_im) / den
    bb_re = f_re[..., None] * b_re - f_im[..., None] * b_im
    bb_im = f_re[..., None] * b_im + f_im[..., None] * b_re
    d = jnp.arange(tc + 1, dtype=F32)[:, None, None]
    pmag = jnp.exp(d * lam_re)
    pw_re, pw_im = pmag * jnp.cos(d * lam_im), pmag * jnp.sin(d * lam_im)
    z_re = pw_re[:tc, :, :, None] * bb_re - pw_im[:tc, :, :, None] * bb_im
    z_im = pw_re[:tc, :, :, None] * bb_im + pw_im[:tc, :, :, None] * bb_re
    kmat = (jnp.einsum('gop,dgpi->gdio', c_re, z_re, precision=hp)
            - jnp.einsum('gop,dgpi->gdio', c_im, z_im, precision=hp))
    ti = jnp.arange(tc)
    lag = ti[None, :] - ti[:, None]
    m5 = kmat[:, jnp.clip(lag, 0, tc - 1)]
    m5 = jnp.where((lag >= 0)[None, :, :, None, None], m5, 0.0)
    m = jnp.transpose(m5, (0, 1, 3, 2, 4)).reshape(g, tc * c, tc * c)
    m = m + jnp.eye(tc * c, dtype=F32) * jnp.tile(d_skip.reshape(g, 1, c), (1, tc, 1)).reshape(g, 1, tc * c)
    w_re = jnp.transpose(z_re[::-1], (1, 0, 3, 2)).reshape(g, tc * c, p)
    w_im = jnp.transpose(z_im[::-1], (1, 0, 3, 2)).reshape(g, tc * c, p)
    mw = jnp.concatenate([m, w_re, w_im], axis=2).astype(BF16)
    q_re, q_im = pw_re[1:], pw_im[1:]
    v_re = c_re[None] * q_re[:, :, None, :] - c_im[None] * q_im[:, :, None, :]
    v_im = c_re[None] * q_im[:, :, None, :] + c_im[None] * q_re[:, :, None, :]
    v_re = jnp.transpose(v_re, (1, 3, 0, 2)).reshape(g, p, tc * c).astype(BF16)
    v_im = jnp.transpose(-v_im, (1, 3, 0, 2)).reshape(g, p, tc * c).astype(BF16)
    return mw, v_re, v_im, pw_re[tc].reshape(1, g * p), pw_im[tc].reshape(1, g * p)


def _s5(su, weights, bsz, lp):
    del lp
    mw, v_re, v_im, a_re, a_im = weights
    t = su.shape[0]
    c, tc, p = SSM_GROUP, SSM_CHUNK, SSM_STATE
    ng = LANES // c
    rows = t // tc
    return pl.pallas_call(
        functools.partial(_s5_kernel, bsz=bsz),
        grid=(SSM_WIDTH // LANES,),
        in_specs=[
            pl.BlockSpec((t, LANES), lambda i: (0, i)),
            pl.BlockSpec((ng, tc * c, tc * c + 2 * p), lambda i: (i, 0, 0)),
            pl.BlockSpec((ng, p, tc * c), lambda i: (i, 0, 0)),
            pl.BlockSpec((ng, p, tc * c), lambda i: (i, 0, 0)),
            pl.BlockSpec((1, ng * p), lambda i: (0, i)),
            pl.BlockSpec((1, ng * p), lambda i: (0, i)),
        ],
        out_specs=pl.BlockSpec((t, LANES), lambda i: (0, i)),
        out_shape=jax.ShapeDtypeStruct((t, SSM_WIDTH), F32),
        scratch_shapes=[pltpu.VMEM((tc * LANES, tc * LANES), BF16),
                        pltpu.VMEM((rows, tc * LANES), F32), pltpu.VMEM((rows, tc * LANES), BF16),
                        pltpu.VMEM((rows, ng * p), F32), pltpu.VMEM((rows, ng * p), F32),
                        pltpu.VMEM((rows, ng * p), F32), pltpu.VMEM((rows, ng * p), F32)],
        compiler_params=_cparams(("arbitrary",)),
        name="s5",
    )(su, mw, v_re, v_im, a_re, a_im)


def _swiglu_accumulate(x, w1_ref, w3_ref, w2_ref, o_ref):
    a = _dot(x, w1_ref[0].astype(BF16))
    b = _dot(x, w3_ref[0].astype(BF16))
    act = (a * jax.nn.sigmoid(a) * b).astype(BF16)
    o_ref[...] += _dot(act, w2_ref[0].astype(BF16))


def _ffn_kernel(be_ref, nu_ref, x_ref, w1_ref, w3_ref, w2_ref, o_ref):
    i = pl.program_id(0)
    f = pl.program_id(1)

    @pl.when(f == 0)
    def _():
        o_ref[...] = jnp.zeros(o_ref.shape, F32)

    @pl.when(i < nu_ref[0])
    def _():
        _swiglu_accumulate(x_ref[...].astype(BF16), w1_ref, w3_ref, w2_ref, o_ref)


def _ffn_gather_kernel(be_ref, nu_ref, tok_ref, h_ref, w1_ref, w3_ref, w2_ref, o_ref, x_buf, sem):
    i = pl.program_id(0)
    f = pl.program_id(1)
    tm = o_ref.shape[0]
    nu = nu_ref[0]
    slot = i % 2

    def row_copy(buf, tok, r):
        return pltpu.make_async_copy(h_ref.at[pl.ds(tok, 1), :], x_buf.at[buf, pl.ds(r, 1), :],
                                     sem.at[buf])

    def start_block(blk, buf):
        def body(r8, carry):
            for k in range(GATHER_UNROLL):
                r = r8 * GATHER_UNROLL + k
                row_copy(buf, tok_ref[blk * tm + r], r).start()
            return carry
        lax.fori_loop(0, tm // GATHER_UNROLL, body, 0)

    def wait_block(buf):
        def body(r8, carry):
            for k in range(GATHER_UNROLL):
                row_copy(buf, 0, r8 * GATHER_UNROLL + k).wait()
            return carry
        lax.fori_loop(0, tm // GATHER_UNROLL, body, 0)

    @pl.when(f == 0)
    def _():
        o_ref[...] = jnp.zeros(o_ref.shape, F32)

        @pl.when((i == 0) & (nu > 0))
        def _():
            start_block(0, 0)

        @pl.when(i < nu)
        def _():
            wait_block(slot)

        @pl.when(i + 1 < nu)
        def _():
            start_block(i + 1, 1 - slot)

    @pl.when(i < nu)
    def _():
        _swiglu_accumulate(x_buf[slot].astype(BF16), w1_ref, w3_ref, w2_ref, o_ref)


def _ffn(x, w1, w3, w2, blk_expert, n_used, tm, row_token=None):
    d = x.shape[1]
    n = x.shape[0] if row_token is None else row_token.shape[0]
    dff = w1.shape[2]
    tf = _pick(dff, (256, 128))
    nf = dff // tf

    def live(i, nu):
        return jnp.minimum(i, nu[0] - 1)

    def fidx(i, f, nu):
        return jnp.where(i < nu[0], f, nf - 1)

    w_specs = [
        pl.BlockSpec((1, d, tf), lambda i, f, be, nu, *_: (be[live(i, nu)], 0, fidx(i, f, nu))),
        pl.BlockSpec((1, d, tf), lambda i, f, be, nu, *_: (be[live(i, nu)], 0, fidx(i, f, nu))),
        pl.BlockSpec((1, tf, d), lambda i, f, be, nu, *_: (be[live(i, nu)], fidx(i, f, nu), 0)),
    ]
    if row_token is None:
        kern, prefetch, scratch = _ffn_kernel, (blk_expert, n_used), []
        x_spec = pl.BlockSpec((tm, d), lambda i, f, be, nu: (live(i, nu), 0))
    else:
        kern, prefetch = _ffn_gather_kernel, (blk_expert, n_used, row_token)
        scratch = [pltpu.VMEM((2, tm, d), x.dtype), pltpu.SemaphoreType.DMA((2,))]
        x_spec = pl.BlockSpec(memory_space=pl.ANY)
    grid_spec = pltpu.PrefetchScalarGridSpec(
        num_scalar_prefetch=len(prefetch),
        grid=(n // tm, nf),
        in_specs=[x_spec] + w_specs,
        out_specs=pl.BlockSpec((tm, d), lambda i, f, *_: (i, 0)),
        scratch_shapes=scratch,
    )
    return pl.pallas_call(
        kern,
        grid_spec=grid_spec,
        out_shape=jax.ShapeDtypeStruct((n, d), F32),
        compiler_params=_cparams(("arbitrary", "arbitrary")),
        name="swiglu",
    )(*prefetch, x, w1, w3, w2)


def _router_kernel(h_ref, w_ref, idx_ref, gate_ref):
    hh, hl = _split_bf16(h_ref[...])
    wh, wl = _split_bf16(w_ref[...])
    logits = _dot_nt(wh, hh) + _dot_nt(wl, hh) + _dot_nt(wh, hl)
    e = lax.broadcasted_iota(I32, logits.shape, 0).astype(F32)
    m1 = jnp.max(logits, axis=0, keepdims=True)
    i1 = jnp.min(jnp.where(logits == m1, e, float(N_EXPERTS)), axis=0, keepdims=True)
    rest = jnp.where(e == i1, -jnp.inf, logits)
    m2 = jnp.max(rest, axis=0, keepdims=True)
    i2 = jnp.min(jnp.where(rest == m2, e, float(N_EXPERTS)), axis=0, keepdims=True)
    e2 = jnp.exp(m2 - m1)
    den = 1.0 + e2
    idx_ref[...] = jnp.concatenate([i1, i2], axis=0).astype(I32)
    gate_ref[...] = jnp.concatenate([1.0 / den, e2 / den], axis=0)


def _router(h, w_router_t):
    t, d = h.shape
    tm = _pick(t, (384, 256, 128))
    return pl.pallas_call(
        _router_kernel,
        grid=(t // tm,),
        in_specs=[pl.BlockSpec((tm, d), lambda i: (i, 0)), pl.BlockSpec((N_EXPERTS, d), lambda i: (0, 0))],
        out_specs=[pl.BlockSpec((TOP_K_EXPERTS, tm), lambda i: (0, i)),
                   pl.BlockSpec((TOP_K_EXPERTS, tm), lambda i: (0, i))],
        out_shape=[jax.ShapeDtypeStruct((TOP_K_EXPERTS, t), I32),
                   jax.ShapeDtypeStruct((TOP_K_EXPERTS, t), F32)],
        compiler_params=_cparams(("parallel",)),
        name="router",
    )(h, w_router_t)


def _gather_kernel(idx_ref, src_ref, o_ref, sem, *, tg):
    base = pl.program_id(0) * tg

    def row_copy(src_row, r):
        return pltpu.make_async_copy(src_ref.at[pl.ds(src_row, 1), :], o_ref.at[pl.ds(r, 1), :], sem)

    def issue(r8, carry):
        for k in range(GATHER_UNROLL):
            r = r8 * GATHER_UNROLL + k
            row_copy(idx_ref[base + r], r).start()
        return carry

    lax.fori_loop(0, tg // GATHER_UNROLL, issue, 0)

    def wait(r8, carry):
        for k in range(GATHER_UNROLL):
            row_copy(0, r8 * GATHER_UNROLL + k).wait()
        return carry

    lax.fori_loop(0, tg // GATHER_UNROLL, wait, 0)


def _gather_rows(src, idx, tg):
    n = idx.shape[0]
    d = src.shape[1]
    grid_spec = pltpu.PrefetchScalarGridSpec(
        num_scalar_prefetch=1,
        grid=(n // tg,),
        in_specs=[pl.BlockSpec(memory_space=pl.ANY)],
        out_specs=pl.BlockSpec((tg, d), lambda i, idx_ref: (i, 0)),
        scratch_shapes=[pltpu.SemaphoreType.DMA(())],
    )
    return pl.pallas_call(
        functools.partial(_gather_kernel, tg=tg),
        grid_spec=grid_spec,
        out_shape=jax.ShapeDtypeStruct((n, d), src.dtype),
        compiler_params=_cparams(("arbitrary",)),
        name="gather_rows",
    )(idx, src)


def _moe(h, w_router, w1, w3, w2, e_base, ln_g, ln_b):
    t, d = h.shape
    tm = MOE_ROWS
    top_idx, gates = _router(h, w_router.T)
    n_assign = t * TOP_K_EXPERTS
    expert = top_idx.reshape(-1)
    onehot = (expert[:, None] == jnp.arange(N_EXPERTS)[None, :]).astype(I32)
    rank = jnp.sum((jnp.cumsum(onehot, axis=0) - onehot) * onehot, axis=1)
    counts = jnp.sum(onehot, axis=0)
    padded = (counts + tm - 1) // tm * tm
    pad_end = jnp.cumsum(padded)
    dest = (pad_end - padded)[expert] + rank
    n_blk = -(-(n_assign + N_EXPERTS * (tm - 1)) // tm)
    n_rows = n_blk * tm
    token = jnp.arange(n_assign, dtype=I32) % t
    row_token = jnp.zeros((n_rows,), I32).at[dest].set(token)
    blk_expert = jnp.minimum(
        jnp.searchsorted(pad_end, jnp.arange(n_blk) * tm, side='right'), N_EXPERTS - 1).astype(I32)
    n_used = (pad_end[-1:] // tm).astype(I32)
    ys = _ffn(h, w1, w3, w2, blk_expert + e_base, n_used, tm, row_token=row_token)
    y01 = _gather_rows(ys, dest.astype(I32), 256)
    return _moe_combine_ln(h, y01, gates.T, ln_g, ln_b)


def _rope_tables(lp, n_heads, head_dim, off, rot_dim, scale=1.0):
    r = rot_dim // 2
    pos = jnp.arange(lp)
    inv = ROPE_THETA ** (-jnp.arange(0, rot_dim, 2, dtype=F32) / rot_dim)
    ang = pos.astype(F32)[:, None] * inv[None, :]
    cos, sin = jnp.cos(ang), jnp.sin(ang)
    c = jnp.ones((lp, head_dim), F32).at[:, off:off + r].set(cos).at[:, off + r:off + 2 * r].set(cos)
    sa = jnp.zeros((lp, head_dim), F32).at[:, off:off + r].set(-sin)
    sb = jnp.zeros((lp, head_dim), F32).at[:, off + r:off + 2 * r].set(sin)
    return tuple(jnp.tile(x * scale, (1, n_heads)) for x in (c, sa, sb)) + (r,)


def _pad_cols(w, n):
    return jnp.pad(w, ((0, 0), (0, n - w.shape[1])))


def kernel(x, meta, ln_in_g, ln_in_b, w_in, b_gate, mla_q_norm, mla_kv_norm, w_uq, w_ukv, ssm_a_re, ssm_a_im, ssm_log_dt, ssm_b_re, ssm_b_im, ssm_c_re, ssm_c_im, ssm_d, w_glu, w_branch_a, w_branch_b, w_branch_c, w_o, ln1_g, ln1_b, ffn_w1, ffn_w3, ffn_w2, w_router, moe_w1, moe_w3, moe_w2, ln2_g, ln2_b):
    bsz, seq, _ = x.shape
    n_tok = seq + N_META
    lp = -(-n_tok // BLOCK) * BLOCK
    t = bsz * lp
    n_sel = min(TOPK_MAX, seq // 4)
    meta_b = jnp.broadcast_to(meta[None].astype(x.dtype), (bsz, N_META, D_MODEL))
    pad = jnp.zeros((bsz, lp - n_tok, D_MODEL), x.dtype)
    h, hb = _layernorm(jnp.concatenate([meta_b, x, pad], axis=1).reshape(t, D_MODEL), ln_in_g, ln_in_b)

    a_scale = A_HEAD_DIM ** -0.5 * LOG2E
    rope_q = _rope_tables(lp, 1, A_HEAD_DIM, 0, A_ROT, a_scale)
    rope_k = _rope_tables(lp, 1, A_HEAD_DIM, 0, A_ROT)
    iq_tab = _rope_tables(lp, IDX_HEADS, IDX_DIM, 0, IDX_ROT)
    ik_tab = _rope_tables(lp, 1, LANES, 0, IDX_ROT)
    w_scale = (jnp.zeros((LANES,), F32).at[:IDX_DIM].set(1.0)
               .at[IDX_DIM:IDX_DIM + IDX_HEADS].set((IDX_HEADS * IDX_DIM) ** -0.5))
    rope_i = tuple(jnp.concatenate([a, b * w_scale[None, :], jnp.zeros((lp, LANES), F32)], axis=1)
                   for a, b in zip(iq_tab[:3], ik_tab[:3])) + (IDX_ROT // 2,)
    n_idx = IDX_HEADS * IDX_DIM + 2 * LANES
    rope_kr = _rope_tables(lp, 1, LANES, 0, MLA_ROPE)
    m_scale = (MLA_NOPE + MLA_ROPE) ** -0.5 * LOG2E
    rope_mq = _rope_tables(lp, 1, 2 * LANES, MLA_NOPE, MLA_ROPE, m_scale)

    offs = [0]
    for s in IN_SIZES:
        offs.append(offs[-1] + s)
    o_aq, o_ak, o_av, o_iq, o_ik, o_iw, o_dq, o_dkv, o_kr, o_su, o_gl, o_end = offs

    for layer in range(DEPTH):
        w_md = _pad_cols(w_in[layer, :, o_dq:o_su], MLA_Q_LORA + MLA_KV_LORA + LANES)
        w_su = w_in[layer, :, o_su:o_gl]
        w_gl = w_in[layer, :, o_gl:o_end]

        q_a = _proj(hb, w_in, w_cols=(layer, o_aq, o_ak - o_aq), out_dtype=BF16, lp=lp, rope=rope_q,
                    name="proj_aq")
        k_a = _proj(hb, w_in, w_cols=(layer, o_ak, o_av - o_ak), out_dtype=BF16, lp=lp, rope=rope_k,
                    name="proj_ak")
        v_a = _proj(hb, w_in, w_cols=(layer, o_av, o_iq - o_av), out_dtype=BF16, lp=lp, name="proj_av")
        idx = _proj(h, w_in, w_cols=(layer, o_iq, n_idx), out_dtype=F32, lp=lp, rope=rope_i, hi_prec=True,
                    tm_prefs=(384, 128), tn_prefs=(n_idx,), name="proj_idx")
        bias = _indexer(idx, bsz, lp, n_sel)
        out_a = _dsa_attention(q_a, k_a, v_a, bias, bsz, lp)

        dqkv, kr = _mla_down(hb, w_md, rope_kr, lp)
        wq = w_uq[layer].reshape(MLA_Q_LORA, MLA_HEADS, MLA_NOPE + MLA_ROPE)
        wq = jnp.pad(wq, ((0, 0), (0, 0), (0, 2 * LANES - MLA_NOPE - MLA_ROPE))).reshape(MLA_Q_LORA, -1)
        q_m = _proj(dqkv, wq, out_dtype=BF16, lp=lp, x_col=0, kdim=MLA_Q_LORA, rope=rope_mq,
                    rms_gain=mla_q_norm[layer], name="proj_mq")
        kv_m = _proj(dqkv, w_ukv, w_cols=(layer, 0, w_ukv.shape[2]), out_dtype=BF16, lp=lp, x_col=1,
                     kdim=MLA_KV_LORA, rms_gain=mla_kv_norm[layer], name="proj_mkv")
        out_b = _mla_attention(q_m, kv_m, kr, bsz, lp)

        su = _proj(hb, w_su, out_dtype=F32, lp=lp, name="proj_su")
        s5w = _s5_weights(ssm_a_re[layer], ssm_a_im[layer], ssm_log_dt[layer], ssm_b_re[layer],
                          ssm_b_im[layer], ssm_c_re[layer], ssm_c_im[layer], ssm_d[layer])
        y = _s5(su, s5w, bsz, lp)
        out_c = _glu(y, w_glu, layer, lp)

        gates = _proj(hb, w_gl, out_dtype=F32, lp=lp, bias=b_gate[layer], sigmoid=True, name="proj_gates")
        merged = _merge(out_a, out_b, out_c, w_branch_a, w_branch_b, w_branch_c, layer, gates, lp)
        mix = _proj(merged, w_o, w_cols=(layer, 0, D_MODEL), out_dtype=F32, lp=lp, name="proj_o")
        h, hb = _layernorm(h, ln1_g[layer], ln1_b[layer], res=mix)

        i = layer // 2
        if layer % 2 == 0:
            tm = _pick(t, (1056, 384, 128))
            nblk = t // tm
            f = _ffn(hb, ffn_w1, ffn_w3, ffn_w2, jnp.full((nblk,), i, I32), jnp.full((1,), nblk, I32), tm)
            h, hb = _layernorm(h, ln2_g[layer], ln2_b[layer], res=f)
        else:
            n_all = moe_w1.shape[0] * N_EXPERTS
            h, hb = _moe(h, w_router[i], moe_w1.reshape((n_all,) + moe_w1.shape[2:]),
                         moe_w3.reshape((n_all,) + moe_w3.shape[2:]),
                         moe_w2.reshape((n_all,) + moe_w2.shape[2:]), i * N_EXPERTS,
                         ln2_g[layer], ln2_b[layer])

    return h.reshape(bsz, lp, D_MODEL)[:, N_META:N_META + seq]
```

```python
import functools
import math

import jax
import jax.numpy as jnp
from jax import lax
from jax.experimental import pallas as pl
from jax.experimental.pallas import tpu as pltpu

F32 = jnp.float32
BF16 = jnp.bfloat16
I32 = jnp.int32

D_MODEL = 2048
DEPTH = 4
N_META = 16
BLOCK = 128
ROPE_THETA = 500000.0
LN_EPS = 1e-5
RMS_EPS = 1e-6
ALPHA = (2 * DEPTH) ** 0.25

A_HEADS = 8
A_KV_HEADS = 2
A_HEAD_DIM = 128
A_ROT = A_HEAD_DIM // 4
IDX_HEADS = 8
IDX_DIM = 64
IDX_ROT = IDX_DIM // 4
TOPK_MAX = 256

MLA_HEADS = 8
MLA_Q_LORA = 512
MLA_KV_LORA = 512
MLA_NOPE = 128
MLA_ROPE = 64
MLA_V = 128

SSM_WIDTH = 1024
SSM_GROUP = 16
SSM_GROUPS = SSM_WIDTH // SSM_GROUP
SSM_STATE = 64
SSM_CHUNK = 16

N_BRANCH = 3
A_WIDTH = A_HEADS * A_HEAD_DIM
MLA_WIDTH = MLA_HEADS * MLA_V
IN_SIZES = (A_HEADS * A_HEAD_DIM, A_KV_HEADS * A_HEAD_DIM, A_KV_HEADS * A_HEAD_DIM,
            IDX_HEADS * IDX_DIM, IDX_DIM, IDX_HEADS,
            MLA_Q_LORA, MLA_KV_LORA, MLA_ROPE,
            SSM_WIDTH, N_BRANCH * D_MODEL)

D_FF = 5632
N_EXPERTS = 8
TOP_K_EXPERTS = 2

LANES = 128
SUBLANES = 8
VMEM_LIMIT = 56 * 1024 * 1024
INT_MIN = -2 ** 31
LOG2E = math.log2(math.e)
GATHER_UNROLL = 8
MOE_ROWS = 768


def _pick(n, prefs):
    for p in prefs:
        if n % p == 0:
            return p
    raise ValueError(f"no tile in {prefs} divides {n}")


def _cparams(sem):
    return pltpu.CompilerParams(dimension_semantics=sem, vmem_limit_bytes=VMEM_LIMIT)


def _dot(a, b):
    return jnp.dot(a, b, preferred_element_type=F32)


def _dot_nt(a, b):
    return lax.dot_general(a, b, (((1,), (1,)), ((), ())), preferred_element_type=F32)


def _split_bf16(x):
    hi = x.astype(BF16)
    lo = (x - hi.astype(F32)).astype(BF16)
    return hi, lo


def _ln_math(x, g, b):
    mu = jnp.mean(x, axis=-1, keepdims=True)
    xc = x - mu
    var = jnp.mean(xc * xc, axis=-1, keepdims=True)
    return xc * lax.rsqrt(var + LN_EPS) * g + b


def _ln_kernel(x_ref, g_ref, b_ref, o_ref, ob_ref):
    y = _ln_math(x_ref[...], g_ref[...], b_ref[...])
    o_ref[...] = y
    ob_ref[...] = y.astype(BF16)


def _ln_res_kernel(h_ref, r_ref, g_ref, b_ref, o_ref, ob_ref):
    y = _ln_math(ALPHA * h_ref[...] + r_ref[...], g_ref[...], b_ref[...])
    o_ref[...] = y
    ob_ref[...] = y.astype(BF16)


def _moe_ln_kernel(h_ref, y_ref, gt_ref, g_ref, b_ref, o_ref, ob_ref):
    gt = gt_ref[...]
    f = y_ref[0] * gt[:, 0:1] + y_ref[1] * gt[:, 1:2]
    y = _ln_math(ALPHA * h_ref[...] + f, g_ref[...], b_ref[...])
    o_ref[...] = y
    ob_ref[...] = y.astype(BF16)


def _layernorm(x, g, b, res=None):
    t, d = x.shape
    tm = _pick(t, (384, 256, 128))
    row = pl.BlockSpec((tm, d), lambda i: (i, 0))
    vec = pl.BlockSpec((1, d), lambda i: (0, 0))
    ins = [x] if res is None else [x, res]
    return pl.pallas_call(
        _ln_kernel if res is None else _ln_res_kernel,
        grid=(t // tm,),
        in_specs=[row] * len(ins) + [vec, vec],
        out_specs=[row, row],
        out_shape=[jax.ShapeDtypeStruct((t, d), F32), jax.ShapeDtypeStruct((t, d), BF16)],
        compiler_params=_cparams(("parallel",)),
        name="layernorm",
    )(*ins, g.reshape(1, d), b.reshape(1, d))


def _proj_ln_kernel(x_ref, w_ref, h_ref, g_ref, b_ref, o_ref, ob_ref):
    y = _ln_math(ALPHA * h_ref[...] + _dot(x_ref[...], w_ref[...]), g_ref[...], b_ref[...])
    o_ref[...] = y
    ob_ref[...] = y.astype(BF16)


def _proj_ln(x, w, layer, h, g, b):
    t, k = x.shape
    d = w.shape[2]
    tm = _pick(t, (384, 256, 128))
    row = pl.BlockSpec((tm, d), lambda i: (i, 0))
    vec = pl.BlockSpec((1, d), lambda i: (0, 0))
    return pl.pallas_call(
        _proj_ln_kernel,
        grid=(t // tm,),
        in_specs=[pl.BlockSpec((tm, k), lambda i: (i, 0)),
                  pl.BlockSpec((None, k, d), lambda i: (layer, 0, 0)), row, vec, vec],
        out_specs=[row, row],
        out_shape=[jax.ShapeDtypeStruct((t, d), F32), jax.ShapeDtypeStruct((t, d), BF16)],
        compiler_params=_cparams(("parallel",)),
        name="proj_o_ln",
    )(x, w, h, g.reshape(1, d), b.reshape(1, d))


def _moe_combine_ln(h, y01, gates_t, g, b):
    t, d = h.shape
    tm = _pick(t, (384, 256, 128))
    nb = t // tm
    row = pl.BlockSpec((tm, d), lambda i: (i, 0))
    vec = pl.BlockSpec((1, d), lambda i: (0, 0))
    return pl.pallas_call(
        _moe_ln_kernel,
        grid=(nb,),
        in_specs=[row, pl.BlockSpec((TOP_K_EXPERTS, tm, d), lambda i: (0, i, 0)),
                  pl.BlockSpec((tm, TOP_K_EXPERTS), lambda i: (i, 0)), vec, vec],
        out_specs=[row, row],
        out_shape=[jax.ShapeDtypeStruct((t, d), F32), jax.ShapeDtypeStruct((t, d), BF16)],
        compiler_params=_cparams(("parallel",)),
        name="moe_combine_ln",
    )(h, y01.reshape(TOP_K_EXPERTS, t, d), gates_t, g.reshape(1, d), b.reshape(1, d))


def _rope_chunk(a, c, sa, sb, r):
    return a * c + pltpu.roll(a, LANES - r, 1) * sa + pltpu.roll(a, r, 1) * sb


def _proj_kernel(*refs, rope_r, has_bias, sigmoid, hi_prec, rms):
    it = iter(refs)
    x_ref = next(it)
    w_ref = next(it)
    n_ref = next(it) if rms else None
    tabs = (next(it), next(it), next(it)) if rope_r else None
    b_ref = next(it) if has_bias else None
    o_ref = next(it)
    x = x_ref[...]
    if rms:
        ms = jnp.mean(x * x, axis=-1, keepdims=True)
        x = x * lax.rsqrt(ms + RMS_EPS) * n_ref[...]
    if hi_prec:
        xh, xl = _split_bf16(x)
        wh, wl = _split_bf16(w_ref[...])
        acc = _dot(xh, wh) + _dot(xl, wh) + _dot(xh, wl)
    else:
        acc = _dot(x.astype(BF16), w_ref[...].astype(BF16))
    if has_bias:
        acc = acc + b_ref[...]
    if sigmoid:
        acc = jax.nn.sigmoid(acc)
    if rope_r:
        c_ref, sa_ref, sb_ref = tabs
        pw = c_ref.shape[1]
        for c in range(acc.shape[1] // LANES):
            sl = slice(c * LANES, (c + 1) * LANES)
            ts = slice((c * LANES) % pw, (c * LANES) % pw + LANES)
            o_ref[:, sl] = _rope_chunk(acc[:, sl], c_ref[:, ts], sa_ref[:, ts], sb_ref[:, ts],
                                       rope_r).astype(o_ref.dtype)
    else:
        o_ref[...] = acc.astype(o_ref.dtype)


def _proj(x, w, *, out_dtype, lp, x_col=0, kdim=None, w_cols=None, rope=None, bias=None, sigmoid=False,
          hi_prec=False, rms_gain=None, tm_prefs=(1056, 768, 384, 128), tn_prefs=(512, 384, 256, 128),
          name="proj"):
    t = x.shape[0]
    kdim = x.shape[1] if kdim is None else kdim
    tm = _pick(lp, tm_prefs)
    nrow = lp // tm
    if w_cols is None:
        n = w.shape[1]
        tn = _pick(n, tn_prefs)
        w_spec = pl.BlockSpec((kdim, tn), lambda i, j: (0, j))
    else:
        layer, col0, n = w_cols
        tn = _pick(math.gcd(n, col0) if col0 else n, tn_prefs)
        cb0 = col0 // tn
        w_spec = pl.BlockSpec((None, kdim, tn), lambda i, j: (layer, 0, cb0 + j))
    in_specs = [pl.BlockSpec((tm, kdim), lambda i, j: (i, x_col)), w_spec]
    ins = [x, w]
    if rms_gain is not None:
        in_specs.append(pl.BlockSpec((1, kdim), lambda i, j: (0, 0)))
        ins.append(rms_gain.reshape(1, kdim))
    if rope is not None:
        pw = rope[0].shape[1]
        if pw == n:
            tab = pl.BlockSpec((tm, tn), lambda i, j: (i % nrow, j))
        else:
            assert tn % pw == 0
            tab = pl.BlockSpec((tm, pw), lambda i, j: (i % nrow, 0))
        in_specs += [tab, tab, tab]
        ins += list(rope[:3])
    if bias is not None:
        in_specs.append(pl.BlockSpec((1, tn), lambda i, j: (0, j)))
        ins.append(bias.reshape(1, n))
    kern = functools.partial(_proj_kernel, rope_r=rope[3] if rope is not None else 0,
                             has_bias=bias is not None, sigmoid=sigmoid, hi_prec=hi_prec,
                             rms=rms_gain is not None)
    return pl.pallas_call(
        kern,
        grid=(t // tm, n // tn),
        in_specs=in_specs,
        out_specs=pl.BlockSpec((tm, tn), lambda i, j: (i, j)),
        out_shape=jax.ShapeDtypeStruct((t, n), out_dtype),
        compiler_params=_cparams(("parallel", "arbitrary")),
        name=name,
    )(*ins)


def _mla_down_kernel(x_ref, w_ref, c_ref, sa_ref, sb_ref, o_ref, kr_ref):
    acc = _dot(x_ref[...], w_ref[...].astype(BF16))
    nq = o_ref.shape[1]
    o_ref[...] = acc[:, :nq]
    kr_ref[...] = _rope_chunk(acc[:, nq:], c_ref[...], sa_ref[...], sb_ref[...],
                              MLA_ROPE // 2).astype(BF16)


def _mla_down(hb, w, rope, lp):
    t, d = hb.shape
    n = w.shape[1]
    nq = MLA_Q_LORA + MLA_KV_LORA
    tm = _pick(lp, (704, 384, 128))
    nrow = lp // tm
    tab = pl.BlockSpec((tm, LANES), lambda i: (i % nrow, 0))
    return pl.pallas_call(
        _mla_down_kernel,
        grid=(t // tm,),
        in_specs=[pl.BlockSpec((tm, d), lambda i: (i, 0)), pl.BlockSpec((d, n), lambda i: (0, 0)),
                  tab, tab, tab],
        out_specs=[pl.BlockSpec((tm, nq), lambda i: (i, 0)), pl.BlockSpec((tm, LANES), lambda i: (i, 0))],
        out_shape=[jax.ShapeDtypeStruct((t, nq), F32), jax.ShapeDtypeStruct((t, LANES), BF16)],
        compiler_params=_cparams(("parallel",)),
        name="mla_down",
    )(hb, w, *rope[:3])


def _glu_kernel(y_ref, w_ref, o_ref):
    n = o_ref.shape[1]
    y = y_ref[...].astype(BF16)
    ga = _dot(y, w_ref[:, :n].astype(BF16))
    gb = _dot(y, w_ref[:, n:].astype(BF16))
    o_ref[...] = (ga * jax.nn.sigmoid(gb)).astype(o_ref.dtype)


def _glu(y, w_glu, layer, lp):
    t, k = y.shape
    n = w_glu.shape[2] // 2
    tm = _pick(lp, (704, 384, 128))
    return pl.pallas_call(
        _glu_kernel,
        grid=(t // tm,),
        in_specs=[pl.BlockSpec((tm, k), lambda i: (i, 0)),
                  pl.BlockSpec((None, k, 2 * n), lambda i: (layer, 0, 0))],
        out_specs=pl.BlockSpec((tm, n), lambda i: (i, 0)),
        out_shape=jax.ShapeDtypeStruct((t, n), BF16),
        compiler_params=_cparams(("parallel",)),
        name="glu",
    )(y, w_glu)


def _merge_kernel(h_ref, a_ref, b_ref, c_ref, wg0_ref, wg1_ref, wg2_ref, bg_ref,
                  wa_ref, wb_ref, wc_ref, o_ref):
    h = h_ref[...]
    m = None
    for k, (wg_ref, x_ref, w_ref) in enumerate(((wg0_ref, a_ref, wa_ref), (wg1_ref, b_ref, wb_ref),
                                                (wg2_ref, c_ref, wc_ref))):
        gate = jax.nn.sigmoid(_dot(h, wg_ref[...].astype(BF16)) + bg_ref[k])
        term = gate * _dot(x_ref[...], w_ref[...].astype(BF16))
        m = term if m is None else m + term
    o_ref[...] = m.astype(o_ref.dtype)


def _merge(hb, out_a, out_b, out_c, w_gates, b_gate, wb_a, wb_b, wb_c, layer, lp):
    t = out_a.shape[0]
    n = wb_a.shape[2]
    tm = _pick(lp, (704, 384, 128))
    tn = _pick(n, (256, 128))

    def act(arr):
        return pl.BlockSpec((tm, arr.shape[1]), lambda i, j: (i, 0))

    def wgt(arr):
        return pl.BlockSpec((None, arr.shape[1], tn), lambda i, j: (layer, 0, j))

    gate_w = pl.BlockSpec((hb.shape[1], tn), lambda i, j: (0, j))
    return pl.pallas_call(
        _merge_kernel,
        grid=(t // tm, n // tn),
        in_specs=[act(hb), act(out_a), act(out_b), act(out_c), gate_w, gate_w, gate_w,
                  pl.BlockSpec((N_BRANCH, 1, tn), lambda i, j: (0, 0, j)),
                  wgt(wb_a), wgt(wb_b), wgt(wb_c)],
        out_specs=pl.BlockSpec((tm, tn), lambda i, j: (i, j)),
        out_shape=jax.ShapeDtypeStruct((t, n), BF16),
        compiler_params=_cparams(("parallel", "arbitrary")),
        name="merge",
    )(hb, out_a, out_b, out_c, *w_gates, b_gate.reshape(N_BRANCH, 1, n), wb_a, wb_b, wb_c)


def _softmax_update(s, v, m_ref, l_ref, acc_ref):
    tk = s.shape[1]
    m_prev = m_ref[...]
    m_new = jnp.maximum(m_prev, jnp.max(s, axis=1, keepdims=True))
    alpha = jnp.exp2(m_prev - m_new)
    p = jnp.exp2(s - jnp.concatenate([m_new] * (tk // LANES), axis=1))
    l_ref[...] = alpha * l_ref[...] + jnp.sum(p, axis=1, keepdims=True)
    acc_ref[...] = alpha * acc_ref[...] + _dot(p.astype(BF16), v)
    m_ref[...] = m_new


def _mla_attn_kernel(q_ref, kv_ref, kr_ref, o_ref, s_ref, m_ref, l_ref, acc_ref):
    qi = pl.program_id(2)
    tq = q_ref.shape[0]
    m_ref[...] = jnp.full(m_ref.shape, -jnp.inf, F32)
    l_ref[...] = jnp.zeros(l_ref.shape, F32)
    acc_ref[...] = jnp.zeros(acc_ref.shape, F32)
    q = q_ref[...]

    def rows_of(kt):
        return pl.ds(pl.multiple_of(kt * tq, tq), tq)

    def scores(kt):
        rows = rows_of(kt)
        return _dot_nt(q, jnp.concatenate([kv_ref[rows, :MLA_NOPE], kr_ref[rows, :]], axis=1))

    s_ref[...] = scores(0)

    def body(kt, carry):
        s = s_ref[...]
        s_ref[...] = scores(kt + 1)
        _softmax_update(s, kv_ref[rows_of(kt), MLA_NOPE:], m_ref, l_ref, acc_ref)
        return carry

    lax.fori_loop(0, qi, body, 0)
    row = lax.broadcasted_iota(I32, (tq, tq), 0)
    col = lax.broadcasted_iota(I32, (tq, tq), 1)
    s = jnp.where(col <= row, s_ref[...], -jnp.inf)
    _softmax_update(s, kv_ref[rows_of(qi), MLA_NOPE:], m_ref, l_ref, acc_ref)
    o_ref[...] = (acc_ref[...] / l_ref[...]).astype(o_ref.dtype)


def _mla_attention(q, kv, kr, bsz, lp):
    t = q.shape[0]
    tq = _pick(lp, (384, 128))
    nq = lp // tq
    q3 = q.reshape(bsz, lp, q.shape[1])
    kv3 = kv.reshape(bsz, lp, kv.shape[1])
    kr3 = kr.reshape(bsz, lp, LANES)
    out = pl.pallas_call(
        _mla_attn_kernel,
        grid=(bsz, MLA_HEADS, nq),
        in_specs=[
            pl.BlockSpec((None, tq, 2 * LANES), lambda b, h, i: (b, i, h)),
            pl.BlockSpec((None, lp, MLA_NOPE + MLA_V), lambda b, h, i: (b, 0, h)),
            pl.BlockSpec((None, lp, LANES), lambda b, h, i: (b, 0, 0)),
        ],
        out_specs=pl.BlockSpec((None, tq, LANES), lambda b, h, i: (b, i, h)),
        out_shape=jax.ShapeDtypeStruct((bsz, lp, MLA_WIDTH), BF16),
        scratch_shapes=[pltpu.VMEM((tq, tq), F32),
                        pltpu.VMEM((tq, LANES), F32), pltpu.VMEM((tq, LANES), F32),
                        pltpu.VMEM((tq, MLA_V), F32)],
        compiler_params=_cparams(("parallel", "parallel", "arbitrary")),
        name="mla_attention",
    )(q3, kv3, kr3)
    return out.reshape(t, MLA_WIDTH)


def _dsa_attn_kernel(q_ref, k_ref, v_ref, bias_ref, o_ref, qg_ref, s_ref, m_ref, l_ref, acc_ref, *, tk):
    qi = pl.program_id(1)
    rep = A_HEADS // A_KV_HEADS
    nkt = tk // BLOCK
    m_ref[...] = jnp.full(m_ref.shape, -jnp.inf, F32)
    l_ref[...] = jnp.zeros(l_ref.shape, F32)
    acc_ref[...] = jnp.zeros(acc_ref.shape, F32)
    for g in range(A_KV_HEADS):
        for r in range(rep):
            h = g * rep + r
            qg_ref[g, r * BLOCK:(r + 1) * BLOCK, :] = q_ref[:, h * A_HEAD_DIM:(h + 1) * A_HEAD_DIM]

    def rows_of(kt):
        return pl.ds(pl.multiple_of(kt * tk, tk), tk)

    def scores(kt):
        rows = rows_of(kt)
        for g in range(A_KV_HEADS):
            s_ref[g] = _dot_nt(qg_ref[g], k_ref[rows, g * A_HEAD_DIM:(g + 1) * A_HEAD_DIM])

    def consume(kt, s):
        bias = jnp.concatenate([bias_ref[kt * nkt + j] for j in range(nkt)], axis=1)
        bias = jnp.concatenate([bias] * rep, axis=0)
        for g in range(A_KV_HEADS):
            _softmax_update(s[g] + bias, v_ref[rows_of(kt), g * A_HEAD_DIM:(g + 1) * A_HEAD_DIM],
                            m_ref.at[g], l_ref.at[g], acc_ref.at[g])

    scores(0)

    def body(kt, carry):
        s = [s_ref[g] for g in range(A_KV_HEADS)]
        scores(kt + 1)
        consume(kt, s)
        return carry

    last = (qi * BLOCK) // tk
    lax.fori_loop(0, last, body, 0)
    consume(last, [s_ref[g] for g in range(A_KV_HEADS)])
    for g in range(A_KV_HEADS):
        o = acc_ref[g] / l_ref[g]
        for r in range(rep):
            h = g * rep + r
            o_ref[:, h * A_HEAD_DIM:(h + 1) * A_HEAD_DIM] = (
                o[r * BLOCK:(r + 1) * BLOCK]).astype(o_ref.dtype)


def _dsa_attention(q, k, v, bias, bsz, lp):
    t = q.shape[0]
    nb = lp // BLOCK
    tk = _pick(lp, (384, 128))
    kvw = A_KV_HEADS * A_HEAD_DIM
    q3 = q.reshape(bsz, lp, A_WIDTH)
    k3 = k.reshape(bsz, lp, kvw)
    v3 = v.reshape(bsz, lp, kvw)
    rows = (A_HEADS // A_KV_HEADS) * BLOCK
    out = pl.pallas_call(
        functools.partial(_dsa_attn_kernel, tk=tk),
        grid=(bsz, nb),
        in_specs=[
            pl.BlockSpec((None, BLOCK, A_WIDTH), lambda b, i: (b, i, 0)),
            pl.BlockSpec((None, lp, kvw), lambda b, i: (b, 0, 0)),
            pl.BlockSpec((None, lp, kvw), lambda b, i: (b, 0, 0)),
            pl.BlockSpec((None, None, nb, BLOCK, BLOCK), lambda b, i: (b, i, 0, 0, 0)),
        ],
        out_specs=pl.BlockSpec((None, BLOCK, A_WIDTH), lambda b, i: (b, i, 0)),
        out_shape=jax.ShapeDtypeStruct((bsz, lp, A_WIDTH), BF16),
        scratch_shapes=[pltpu.VMEM((A_KV_HEADS, rows, A_HEAD_DIM), BF16),
                        pltpu.VMEM((A_KV_HEADS, rows, tk), F32),
                        pltpu.VMEM((A_KV_HEADS, rows, LANES), F32),
                        pltpu.VMEM((A_KV_HEADS, rows, LANES), F32),
                        pltpu.VMEM((A_KV_HEADS, rows, A_HEAD_DIM), F32)],
        compiler_params=_cparams(("parallel", "arbitrary")),
        name="dsa_attention",
    )(q3, k3, v3, bias)
    return out.reshape(t, A_WIDTH)


def _indexer_kernel(iq_ref, kw_ref, o_ref, kcat_ref, qcat_ref, wb_ref, key_ref, *, n_sel):
    qb = pl.program_id(0)
    bsz, nb = o_ref.shape[0], o_ref.shape[1]
    batch = range(bsz)
    half = IDX_DIM
    lane = lax.broadcasted_iota(I32, (BLOCK, LANES), 1)
    row = lax.broadcasted_iota(I32, (BLOCK, LANES), 0)

    @pl.when(qb == 0)
    def _():
        for b in batch:
            kf = kw_ref[b]
            klane = lax.broadcasted_iota(I32, kf.shape, 1)
            kz = jnp.where(klane < half, kf, 0.0)
            hi = kz.astype(BF16).astype(F32)
            lo = kz - hi
            lp = kf.shape[0]
            kcat_ref[b, 0:lp, :] = jnp.concatenate(
                [(hi + pltpu.roll(hi, half, 1)).astype(BF16), lo.astype(BF16)], axis=1)
            kcat_ref[b, lp:lp + BLOCK, :] = jnp.zeros((BLOCK, 2 * LANES), BF16)

    for b in batch:
        wq = kw_ref[b, pl.ds(pl.multiple_of(qb * BLOCK, BLOCK), BLOCK), :]
        for h in range(IDX_HEADS):
            chunk = iq_ref[b, :, (h // 2) * LANES:(h // 2 + 1) * LANES]
            if h % 2 == 0:
                a = jnp.where(lane < half, chunk, 0.0)
            else:
                a = pltpu.roll(jnp.where(lane >= half, chunk, 0.0), half, 1)
            hi = a.astype(BF16).astype(F32)
            lo = a - hi
            qcat_ref[b, h * BLOCK:(h + 1) * BLOCK, :] = jnp.concatenate(
                [(hi + pltpu.roll(lo, half, 1)).astype(BF16), hi.astype(BF16)], axis=1)
            wb_ref[b, h] = jnp.broadcast_to(wq[:, half + h:half + h + 1], (BLOCK, LANES))

    qpos = qb * BLOCK + row

    def sortable(x):
        b = pltpu.bitcast(x, I32)
        return b ^ ((b >> 31) & 0x7FFFFFFF)

    n_pair = (qb + 2) // 2

    def score_pair(kp, carry):
        for b in batch:
            kblk = kcat_ref[b, pl.ds(pl.multiple_of(kp * 2 * BLOCK, 2 * BLOCK), 2 * BLOCK), :]
            s = _dot_nt(qcat_ref[b], kblk)
            for j in range(2):
                sc = jnp.zeros((BLOCK, LANES), F32)
                for h in range(IDX_HEADS):
                    sc = sc + (jnp.maximum(s[h * BLOCK:(h + 1) * BLOCK, j * LANES:(j + 1) * LANES], 0.0)
                               * wb_ref[b, h])
                kpos = (kp * 2 + j) * BLOCK + lane
                sc = jnp.where(kpos < N_META, jnp.inf, sc)
                sc = jnp.where(kpos <= qpos, sc, -jnp.inf)
                key_ref[b, kp * 2 + j] = sortable(sc)
        return carry

    lax.fori_loop(0, n_pair, score_pair, 0)

    def count(pred):
        def body(kp, cs):
            return tuple(c + jnp.where(pred(key_ref[b, kp * 2], b), 1.0, 0.0)
                         + jnp.where(pred(key_ref[b, kp * 2 + 1], b), 1.0, 0.0)
                         for b, c in zip(batch, cs))
        cs = lax.fori_loop(0, n_pair, body, tuple(jnp.zeros((BLOCK, LANES), F32) for _ in batch))
        return tuple(jnp.sum(c, axis=1, keepdims=True) for c in cs)

    def bit_step(i, thrs):
        bit = 31 - i
        cands = tuple(jnp.where(bit == 31, thr ^ INT_MIN, thr | (1 << jnp.minimum(bit, 30)))
                      for thr in thrs)
        counts = count(lambda k, b: k >= cands[b])
        return tuple(jnp.where(n >= n_sel, cand, thr) for n, cand, thr in zip(counts, cands, thrs))

    thrs = lax.fori_loop(0, 32, bit_step,
                         tuple(jnp.full((BLOCK, LANES), INT_MIN, I32) for _ in batch))
    n_ge = count(lambda k, b: k >= thrs[b])
    all_ties_fit = functools.reduce(jnp.maximum, [jnp.max(n) for n in n_ge]) <= n_sel

    @pl.when(all_ties_fit)
    def _():
        def emit(kt, carry):
            kpos = kt * BLOCK + lane
            for b in batch:
                o_ref[b, kt] = jnp.where((key_ref[b, kt] >= thrs[b]) & (kpos <= qpos), 0.0, -jnp.inf)
            return carry

        lax.fori_loop(0, qb + 1, emit, 0)

    @pl.when(jnp.logical_not(all_ties_fit))
    def _():
        n_gt = count(lambda k, b: k > thrs[b])
        tri = (lax.broadcasted_iota(I32, (LANES, LANES), 0)
               <= lax.broadcasted_iota(I32, (LANES, LANES), 1)).astype(BF16)

        def emit(kt, takens):
            out = []
            kpos = kt * BLOCK + lane
            for b in batch:
                key = key_ref[b, kt]
                eq = key == thrs[b]
                rank = _dot(jnp.where(eq, 1.0, 0.0).astype(BF16), tri)
                sel = (key > thrs[b]) | (eq & (takens[b] + rank <= n_sel - n_gt[b]))
                o_ref[b, kt] = jnp.where(sel & (kpos <= qpos), 0.0, -jnp.inf)
                out.append(takens[b] + rank[:, LANES - 1:LANES])
            return tuple(out)

        lax.fori_loop(0, qb + 1, emit, tuple(jnp.zeros((BLOCK, 1), F32) for _ in batch))

    def fill(kt, carry):
        for b in batch:
            o_ref[b, kt] = jnp.full((BLOCK, LANES), -jnp.inf, F32)
        return carry

    lax.fori_loop(qb + 1, nb, fill, 0)


def _indexer(idx, bsz, lp, n_sel):
    nb = lp // BLOCK
    idx3 = idx.reshape(bsz, lp, idx.shape[1])
    nq = IDX_HEADS * IDX_DIM
    kw_col = nq // LANES
    return pl.pallas_call(
        functools.partial(_indexer_kernel, n_sel=n_sel),
        grid=(nb,),
        in_specs=[
            pl.BlockSpec((bsz, BLOCK, nq), lambda i: (0, i, 0)),
            pl.BlockSpec((bsz, lp, LANES), lambda i: (0, 0, kw_col)),
        ],
        out_specs=pl.BlockSpec((bsz, None, nb, BLOCK, BLOCK), lambda i: (0, i, 0, 0, 0)),
        out_shape=jax.ShapeDtypeStruct((bsz, nb, nb, BLOCK, BLOCK), F32),
        scratch_shapes=[pltpu.VMEM((bsz, lp + BLOCK, 2 * LANES), BF16),
                        pltpu.VMEM((bsz, IDX_HEADS * BLOCK, 2 * LANES), BF16),
                        pltpu.VMEM((bsz, IDX_HEADS, BLOCK, LANES), F32),
                        pltpu.VMEM((bsz, nb + 1, BLOCK, LANES), I32)],
        compiler_params=_cparams(("arbitrary",)),
        name="indexer",
    )(idx3, idx3)


def _s5_kernel(x_ref, mw_ref, vre_ref, vim_ref, are_ref, aim_ref, o_ref,
               perm_ref, y_ref, yb_ref, sre_ref, sim_ref, pre_ref, pim_ref, *, bsz):
    tc, c, p = SSM_CHUNK, SSM_GROUP, SSM_STATE
    ng = LANES // c
    nm = tc * c
    rows = x_ref.shape[0] // tc
    per_b = rows // bsz
    width = tc * LANES

    @pl.when(pl.program_id(0) == 0)
    def _():
        src = lax.broadcasted_iota(I32, (width, width), 0)
        dst = lax.broadcasted_iota(I32, (width, width), 1)
        group = (src & (LANES - 1)) >> (c.bit_length() - 1)
        step_in_chunk = src >> (LANES.bit_length() - 1)
        want = group * nm + step_in_chunk * c + (src & (c - 1))
        perm_ref[...] = jnp.where(dst == want, 1.0, 0.0).astype(BF16)

    x_all = jnp.concatenate(
        [x_ref[pl.ds(j, rows, stride=tc), :].astype(BF16) for j in range(tc)], axis=1)
    u_all = _dot(x_all, perm_ref[...]).astype(BF16)
    for g in range(ng):
        r = _dot(u_all[:, g * nm:(g + 1) * nm], mw_ref[g])
        y_ref[:, g * nm:(g + 1) * nm] = r[:, :nm]
        sre_ref[:, g * p:(g + 1) * p] = r[:, nm:nm + p]
        sim_ref[:, g * p:(g + 1) * p] = r[:, nm + p:nm + 2 * p]
    ar = are_ref[...]
    ai = aim_ref[...]

    def step(ti, carry):
        out = []
        for b in range(bsz):
            cr, ci = carry[b]
            rs = pl.ds(pl.multiple_of(b * per_b + ti * SUBLANES, SUBLANES), SUBLANES)
            lr = sre_ref[rs, :]
            li = sim_ref[rs, :]
            before_r, before_i = [], []
            for k in range(SUBLANES):
                before_r.append(cr)
                before_i.append(ci)
                cr, ci = (ar * cr - ai * ci + lr[k:k + 1], ar * ci + ai * cr + li[k:k + 1])
            pre_ref[rs, :] = jnp.concatenate(before_r, axis=0)
            pim_ref[rs, :] = jnp.concatenate(before_i, axis=0)
            out.append((cr, ci))
        return tuple(out)

    zero = jnp.zeros((1, ng * p), F32)
    lax.fori_loop(0, per_b // SUBLANES, step, tuple((zero, zero) for _ in range(bsz)))
    for g in range(ng):
        y = (y_ref[:, g * nm:(g + 1) * nm]
             + _dot(pre_ref[:, g * p:(g + 1) * p].astype(BF16), vre_ref[g])
             + _dot(pim_ref[:, g * p:(g + 1) * p].astype(BF16), vim_ref[g]))
        yb_ref[:, g * nm:(g + 1) * nm] = y.astype(BF16)
    y_all = _dot_nt(yb_ref[...], perm_ref[...])
    for j in range(tc):
        o_ref[pl.ds(j, rows, stride=tc), :] = y_all[:, j * LANES:(j + 1) * LANES]


def _s5_weights(a_re, a_im, log_dt, b_re, b_im, c_re, c_im, d_skip):
    hp = lax.Precision.HIGHEST
    g, p, c, tc = SSM_GROUPS, SSM_STATE, SSM_GROUP, SSM_CHUNK
    dt = jnp.exp(log_dt)[:, None]
    lam_re, lam_im = dt * a_re, dt * a_im
    mag = jnp.exp(lam_re)
    ab_re, ab_im = mag * jnp.cos(lam_im), mag * jnp.sin(lam_im)
    den = a_re * a_re + a_im * a_im
    f_re = ((ab_re - 1.0) * a_re + ab_im * a_im) / den
    f_im = (ab_im * a_re - (ab_re - 1.0) * a_im) / den
    bb_re = f_re[..., None] * b_re - f_im[..., None] * b_im
    bb_im = f_re[..., None] * b_im + f_im[..., None] * b_re
    d = jnp.arange(tc + 1, dtype=F32)[:, None, None]
    pmag = jnp.exp(d * lam_re)
    pw_re, pw_im = pmag * jnp.cos(d * lam_im), pmag * jnp.sin(d * lam_im)
    z_re = pw_re[:tc, :, :, None] * bb_re - pw_im[:tc, :, :, None] * bb_im
    z_im = pw_re[:tc, :, :, None] * bb_im + pw_im[:tc, :, :, None] * bb_re
    kmat = (jnp.einsum('gop,dgpi->gdio', c_re, z_re, precision=hp)
            - jnp.einsum('gop,dgpi->gdio', c_im, z_im, precision=hp))
    ti = jnp.arange(tc)
    lag = ti[None, :] - ti[:, None]
    m5 = kmat[:, jnp.clip(lag, 0, tc - 1)]
    m5 = jnp.where((lag >= 0)[None, :, :, None, None], m5, 0.0)
    m = jnp.transpose(m5, (0, 1, 3, 2, 4)).reshape(g, tc * c, tc * c)
    m = m + jnp.eye(tc * c, dtype=F32) * jnp.tile(d_skip.reshape(g, 1, c), (1, tc, 1)).reshape(g, 1, tc * c)
    w_re = jnp.transpose(z_re[::-1], (1, 0, 3, 2)).reshape(g, tc * c, p)
    w_im = jnp.transpose(z_im[::-1], (1, 0, 3, 2)).reshape(g, tc * c, p)
    mw = jnp.concatenate([m, w_re, w_im], axis=2).astype(BF16)
    q_re, q_im = pw_re[1:], pw_im[1:]
    v_re = c_re[None] * q_re[:, :, None, :] - c_im[None] * q_im[:, :, None, :]
    v_im = c_re[None] * q_im[:, :, None, :] + c_im[None] * q_re[:, :, None, :]
    v_re = jnp.transpose(v_re, (1, 3, 0, 2)).reshape(g, p, tc * c).astype(BF16)
    v_im = jnp.transpose(-v_im, (1, 3, 0, 2)).reshape(g, p, tc * c).astype(BF16)
    return mw, v_re, v_im, pw_re[tc].reshape(1, g * p), pw_im[tc].reshape(1, g * p)


def _s5(su, weights, bsz, lp):
    del lp
    mw, v_re, v_im, a_re, a_im = weights
    t = su.shape[0]
    c, tc, p = SSM_GROUP, SSM_CHUNK, SSM_STATE
    ng = LANES // c
    rows = t // tc
    return pl.pallas_call(
        functools.partial(_s5_kernel, bsz=bsz),
        grid=(SSM_WIDTH // LANES,),
        in_specs=[
            pl.BlockSpec((t, LANES), lambda i: (0, i)),
            pl.BlockSpec((ng, tc * c, tc * c + 2 * p), lambda i: (i, 0, 0)),
            pl.BlockSpec((ng, p, tc * c), lambda i: (i, 0, 0)),
            pl.BlockSpec((ng, p, tc * c), lambda i: (i, 0, 0)),
            pl.BlockSpec((1, ng * p), lambda i: (0, i)),
            pl.BlockSpec((1, ng * p), lambda i: (0, i)),
        ],
        out_specs=pl.BlockSpec((t, LANES), lambda i: (0, i)),
        out_shape=jax.ShapeDtypeStruct((t, SSM_WIDTH), F32),
        scratch_shapes=[pltpu.VMEM((tc * LANES, tc * LANES), BF16),
                        pltpu.VMEM((rows, tc * LANES), F32), pltpu.VMEM((rows, tc * LANES), BF16),
                        pltpu.VMEM((rows, ng * p), F32), pltpu.VMEM((rows, ng * p), F32),
                        pltpu.VMEM((rows, ng * p), F32), pltpu.VMEM((rows, ng * p), F32)],
        compiler_params=_cparams(("arbitrary",)),
        name="s5",
    )(su, mw, v_re, v_im, a_re, a_im)


def _swiglu_accumulate(x, w1_ref, w3_ref, w2_ref, o_ref):
    a = _dot(x, w1_ref[0].astype(BF16))
    b = _dot(x, w3_ref[0].astype(BF16))
    act = (a * jax.nn.sigmoid(a) * b).astype(BF16)
    o_ref[...] += _dot(act, w2_ref[0].astype(BF16))


def _ffn_kernel(be_ref, nu_ref, x_ref, w1_ref, w3_ref, w2_ref, o_ref):
    i = pl.program_id(0)
    f = pl.program_id(1)

    @pl.when(f == 0)
    def _():
        o_ref[...] = jnp.zeros(o_ref.shape, F32)

    @pl.when(i < nu_ref[0])
    def _():
        _swiglu_accumulate(x_ref[...].astype(BF16), w1_ref, w3_ref, w2_ref, o_ref)


def _ffn_gather_kernel(be_ref, nu_ref, tok_ref, h_ref, w1_ref, w3_ref, w2_ref, o_ref, x_buf, sem):
    i = pl.program_id(0)
    f = pl.program_id(1)
    tm = o_ref.shape[0]
    nu = nu_ref[0]
    slot = i % 2

    def row_copy(buf, tok, r):
        return pltpu.make_async_copy(h_ref.at[pl.ds(tok, 1), :], x_buf.at[buf, pl.ds(r, 1), :],
                                     sem.at[buf])

    def start_block(blk, buf):
        def body(r8, carry):
            for k in range(GATHER_UNROLL):
                r = r8 * GATHER_UNROLL + k
                row_copy(buf, tok_ref[blk * tm + r], r).start()
            return carry
        lax.fori_loop(0, tm // GATHER_UNROLL, body, 0)

    def wait_block(buf):
        def body(r8, carry):
            for k in range(GATHER_UNROLL):
                row_copy(buf, 0, r8 * GATHER_UNROLL + k).wait()
            return carry
        lax.fori_loop(0, tm // GATHER_UNROLL, body, 0)

    @pl.when(f == 0)
    def _():
        o_ref[...] = jnp.zeros(o_ref.shape, F32)

        @pl.when((i == 0) & (nu > 0))
        def _():
            start_block(0, 0)

        @pl.when(i < nu)
        def _():
            wait_block(slot)

        @pl.when(i + 1 < nu)
        def _():
            start_block(i + 1, 1 - slot)

    @pl.when(i < nu)
    def _():
        _swiglu_accumulate(x_buf[slot].astype(BF16), w1_ref, w3_ref, w2_ref, o_ref)


def _ffn(x, w1, w3, w2, blk_expert, n_used, tm, row_token=None):
    d = x.shape[1]
    n = x.shape[0] if row_token is None else row_token.shape[0]
    dff = w1.shape[2]
    tf = _pick(dff, (256, 128))
    nf = dff // tf

    def live(i, nu):
        return jnp.minimum(i, nu[0] - 1)

    def fidx(i, f, nu):
        return jnp.where(i < nu[0], f, nf - 1)

    w_specs = [
        pl.BlockSpec((1, d, tf), lambda i, f, be, nu, *_: (be[live(i, nu)], 0, fidx(i, f, nu))),
        pl.BlockSpec((1, d, tf), lambda i, f, be, nu, *_: (be[live(i, nu)], 0, fidx(i, f, nu))),
        pl.BlockSpec((1, tf, d), lambda i, f, be, nu, *_: (be[live(i, nu)], fidx(i, f, nu), 0)),
    ]
    if row_token is None:
        kern, prefetch, scratch = _ffn_kernel, (blk_expert, n_used), []
        x_spec = pl.BlockSpec((tm, d), lambda i, f, be, nu: (live(i, nu), 0))
    else:
        kern, prefetch = _ffn_gather_kernel, (blk_expert, n_used, row_token)
        scratch = [pltpu.VMEM((2, tm, d), x.dtype), pltpu.SemaphoreType.DMA((2,))]
        x_spec = pl.BlockSpec(memory_space=pl.ANY)
    grid_spec = pltpu.PrefetchScalarGridSpec(
        num_scalar_prefetch=len(prefetch),
        grid=(n // tm, nf),
        in_specs=[x_spec] + w_specs,
        out_specs=pl.BlockSpec((tm, d), lambda i, f, *_: (i, 0)),
        scratch_shapes=scratch,
    )
    return pl.pallas_call(
        kern,
        grid_spec=grid_spec,
        out_shape=jax.ShapeDtypeStruct((n, d), F32),
        compiler_params=_cparams(("arbitrary", "arbitrary")),
        name="swiglu",
    )(*prefetch, x, w1, w3, w2)


def _router_kernel(h_ref, w_ref, idx_ref, gate_ref):
    hh, hl = _split_bf16(h_ref[...])
    wh, wl = _split_bf16(w_ref[...])
    logits = _dot_nt(wh, hh) + _dot_nt(wl, hh) + _dot_nt(wh, hl)
    e = lax.broadcasted_iota(I32, logits.shape, 0).astype(F32)
    m1 = jnp.max(logits, axis=0, keepdims=True)
    i1 = jnp.min(jnp.where(logits == m1, e, float(N_EXPERTS)), axis=0, keepdims=True)
    rest = jnp.where(e == i1, -jnp.inf, logits)
    m2 = jnp.max(rest, axis=0, keepdims=True)
    i2 = jnp.min(jnp.where(rest == m2, e, float(N_EXPERTS)), axis=0, keepdims=True)
    e2 = jnp.exp(m2 - m1)
    den = 1.0 + e2
    idx_ref[...] = jnp.concatenate([i1, i2], axis=0).astype(I32)
    gate_ref[...] = jnp.concatenate([1.0 / den, e2 / den], axis=0)


def _router(h, w_router_t):
    t, d = h.shape
    tm = _pick(t, (384, 256, 128))
    return pl.pallas_call(
        _router_kernel,
        grid=(t // tm,),
        in_specs=[pl.BlockSpec((tm, d), lambda i: (i, 0)), pl.BlockSpec((N_EXPERTS, d), lambda i: (0, 0))],
        out_specs=[pl.BlockSpec((TOP_K_EXPERTS, tm), lambda i: (0, i)),
                   pl.BlockSpec((TOP_K_EXPERTS, tm), lambda i: (0, i))],
        out_shape=[jax.ShapeDtypeStruct((TOP_K_EXPERTS, t), I32),
                   jax.ShapeDtypeStruct((TOP_K_EXPERTS, t), F32)],
        compiler_params=_cparams(("parallel",)),
        name="router",
    )(h, w_router_t)


def _gather_kernel(idx_ref, src_ref, o_ref, sem, *, tg):
    base = pl.program_id(0) * tg

    def row_copy(src_row, r):
        return pltpu.make_async_copy(src_ref.at[pl.ds(src_row, 1), :], o_ref.at[pl.ds(r, 1), :], sem)

    def issue(r8, carry):
        for k in range(GATHER_UNROLL):
            r = r8 * GATHER_UNROLL + k
            row_copy(idx_ref[base + r], r).start()
        return carry

    lax.fori_loop(0, tg // GATHER_UNROLL, issue, 0)

    def wait(r8, carry):
        for k in range(GATHER_UNROLL):
            row_copy(0, r8 * GATHER_UNROLL + k).wait()
        return carry

    lax.fori_loop(0, tg // GATHER_UNROLL, wait, 0)


def _gather_rows(src, idx, tg):
    n = idx.shape[0]
    d = src.shape[1]
    grid_spec = pltpu.PrefetchScalarGridSpec(
        num_scalar_prefetch=1,
        grid=(n // tg,),
        in_specs=[pl.BlockSpec(memory_space=pl.ANY)],
        out_specs=pl.BlockSpec((tg, d), lambda i, idx_ref: (i, 0)),
        scratch_shapes=[pltpu.SemaphoreType.DMA(())],
    )
    return pl.pallas_call(
        functools.partial(_gather_kernel, tg=tg),
        grid_spec=grid_spec,
        out_shape=jax.ShapeDtypeStruct((n, d), src.dtype),
        compiler_params=_cparams(("arbitrary",)),
        name="gather_rows",
    )(idx, src)


def _moe(h, w_router, w1, w3, w2, e_base, ln_g, ln_b):
    t, d = h.shape
    tm = MOE_ROWS
    top_idx, gates = _router(h, w_router.T)
    n_assign = t * TOP_K_EXPERTS
    expert = top_idx.reshape(-1)
    onehot = (expert[:, None] == jnp.arange(N_EXPERTS)[None, :]).astype(I32)
    counts = jnp.sum(onehot, axis=0)
    padded = (counts + tm - 1) // tm * tm
    pad_end = jnp.cumsum(padded)
    dest = jnp.sum((jnp.cumsum(onehot, axis=0) - onehot + (pad_end - padded)[None, :]) * onehot, axis=1)
    n_blk = -(-(n_assign + N_EXPERTS * (tm - 1)) // tm)
    n_rows = n_blk * tm
    token = jnp.arange(n_assign, dtype=I32) % t
    row_token = jnp.zeros((n_rows,), I32).at[dest].set(token)
    blk_expert = jnp.minimum(
        jnp.sum((jnp.arange(n_blk)[:, None] * tm >= pad_end[None, :]).astype(I32), axis=1), N_EXPERTS - 1)
    n_used = (pad_end[-1:] // tm).astype(I32)
    ys = _ffn(h, w1, w3, w2, blk_expert + e_base, n_used, tm, row_token=row_token)
    y01 = _gather_rows(ys, dest.astype(I32), 256)
    return _moe_combine_ln(h, y01, gates.T, ln_g, ln_b)


def _rope_tables(lp, n_heads, head_dim, off, rot_dim, scale=1.0):
    r = rot_dim // 2
    pos = jnp.arange(lp)
    inv = ROPE_THETA ** (-jnp.arange(0, rot_dim, 2, dtype=F32) / rot_dim)
    ang = pos.astype(F32)[:, None] * inv[None, :]
    cos, sin = jnp.cos(ang), jnp.sin(ang)
    c = jnp.ones((lp, head_dim), F32).at[:, off:off + r].set(cos).at[:, off + r:off + 2 * r].set(cos)
    sa = jnp.zeros((lp, head_dim), F32).at[:, off:off + r].set(-sin)
    sb = jnp.zeros((lp, head_dim), F32).at[:, off + r:off + 2 * r].set(sin)
    return tuple(jnp.tile(x * scale, (1, n_heads)) for x in (c, sa, sb)) + (r,)


def _pad_cols(w, n):
    return jnp.pad(w, ((0, 0), (0, n - w.shape[1])))


def kernel(x, meta, ln_in_g, ln_in_b, w_in, b_gate, mla_q_norm, mla_kv_norm, w_uq, w_ukv, ssm_a_re, ssm_a_im, ssm_log_dt, ssm_b_re, ssm_b_im, ssm_c_re, ssm_c_im, ssm_d, w_glu, w_branch_a, w_branch_b, w_branch_c, w_o, ln1_g, ln1_b, ffn_w1, ffn_w3, ffn_w2, w_router, moe_w1, moe_w3, moe_w2, ln2_g, ln2_b):
    bsz, seq, _ = x.shape
    n_tok = seq + N_META
    lp = -(-n_tok // BLOCK) * BLOCK
    t = bsz * lp
    n_sel = min(TOPK_MAX, seq // 4)
    meta_b = jnp.broadcast_to(meta[None].astype(x.dtype), (bsz, N_META, D_MODEL))
    pad = jnp.zeros((bsz, lp - n_tok, D_MODEL), x.dtype)
    h, hb = _layernorm(jnp.concatenate([meta_b, x, pad], axis=1).reshape(t, D_MODEL), ln_in_g, ln_in_b)

    a_scale = A_HEAD_DIM ** -0.5 * LOG2E
    rope_q = _rope_tables(lp, 1, A_HEAD_DIM, 0, A_ROT, a_scale)
    rope_k = _rope_tables(lp, 1, A_HEAD_DIM, 0, A_ROT)
    iq_tab = _rope_tables(lp, IDX_HEADS, IDX_DIM, 0, IDX_ROT)
    ik_tab = _rope_tables(lp, 1, LANES, 0, IDX_ROT)
    w_scale = (jnp.zeros((LANES,), F32).at[:IDX_DIM].set(1.0)
               .at[IDX_DIM:IDX_DIM + IDX_HEADS].set((IDX_HEADS * IDX_DIM) ** -0.5))
    rope_i = tuple(jnp.concatenate([a, b * w_scale[None, :], jnp.zeros((lp, LANES), F32)], axis=1)
                   for a, b in zip(iq_tab[:3], ik_tab[:3])) + (IDX_ROT // 2,)
    n_idx = IDX_HEADS * IDX_DIM + 2 * LANES
    rope_kr = _rope_tables(lp, 1, LANES, 0, MLA_ROPE)
    m_scale = (MLA_NOPE + MLA_ROPE) ** -0.5 * LOG2E
    rope_mq = _rope_tables(lp, 1, 2 * LANES, MLA_NOPE, MLA_ROPE, m_scale)

    offs = [0]
    for s in IN_SIZES:
        offs.append(offs[-1] + s)
    o_aq, o_ak, o_av, o_iq, o_ik, o_iw, o_dq, o_dkv, o_kr, o_su, o_gl, o_end = offs

    w_o_bf16 = w_o.astype(BF16)

    for layer in range(DEPTH):
        w_md = _pad_cols(w_in[layer, :, o_dq:o_su], MLA_Q_LORA + MLA_KV_LORA + LANES)
        w_su = w_in[layer, :, o_su:o_gl]

        q_a = _proj(hb, w_in, w_cols=(layer, o_aq, o_ak - o_aq), out_dtype=BF16, lp=lp, rope=rope_q,
                    name="proj_aq")
        k_a = _proj(hb, w_in, w_cols=(layer, o_ak, o_av - o_ak), out_dtype=BF16, lp=lp, rope=rope_k,
                    name="proj_ak")
        v_a = _proj(hb, w_in, w_cols=(layer, o_av, o_iq - o_av), out_dtype=BF16, lp=lp, name="proj_av")
        idx = _proj(h, w_in, w_cols=(layer, o_iq, n_idx), out_dtype=F32, lp=lp, rope=rope_i, hi_prec=True,
                    tm_prefs=(384, 128), tn_prefs=(n_idx,), name="proj_idx")
        bias = _indexer(idx, bsz, lp, n_sel)
        out_a = _dsa_attention(q_a, k_a, v_a, bias, bsz, lp)

        dqkv, kr = _mla_down(hb, w_md, rope_kr, lp)
        wq = w_uq[layer].reshape(MLA_Q_LORA, MLA_HEADS, MLA_NOPE + MLA_ROPE)
        wq = jnp.pad(wq, ((0, 0), (0, 0), (0, 2 * LANES - MLA_NOPE - MLA_ROPE))).reshape(MLA_Q_LORA, -1)
        q_m = _proj(dqkv, wq, out_dtype=BF16, lp=lp, x_col=0, kdim=MLA_Q_LORA, rope=rope_mq,
                    rms_gain=mla_q_norm[layer], name="proj_mq")
        kv_m = _proj(dqkv, w_ukv, w_cols=(layer, 0, w_ukv.shape[2]), out_dtype=BF16, lp=lp, x_col=1,
                     kdim=MLA_KV_LORA, rms_gain=mla_kv_norm[layer], name="proj_mkv")
        out_b = _mla_attention(q_m, kv_m, kr, bsz, lp)

        su = _proj(hb, w_su, out_dtype=F32, lp=lp, name="proj_su")
        s5w = _s5_weights(ssm_a_re[layer], ssm_a_im[layer], ssm_log_dt[layer], ssm_b_re[layer],
                          ssm_b_im[layer], ssm_c_re[layer], ssm_c_im[layer], ssm_d[layer])
        y = _s5(su, s5w, bsz, lp)
        out_c = _glu(y, w_glu, layer, lp)

        w_gates = [w_in[layer, :, o_gl + k * D_MODEL:o_gl + (k + 1) * D_MODEL] for k in range(N_BRANCH)]
        merged = _merge(hb, out_a, out_b, out_c, w_gates, b_gate[layer], w_branch_a, w_branch_b,
                        w_branch_c, layer, lp)
        h, hb = _proj_ln(merged, w_o_bf16, layer, h, ln1_g[layer], ln1_b[layer])

        i = layer // 2
        if layer % 2 == 0:
            tm = _pick(t, (1056, 384, 128))
            nblk = t // tm
            f = _ffn(hb, ffn_w1, ffn_w3, ffn_w2, jnp.full((nblk,), i, I32), jnp.full((1,), nblk, I32), tm)
            h, hb = _layernorm(h, ln2_g[layer], ln2_b[layer], res=f)
        else:
            n_all = moe_w1.shape[0] * N_EXPERTS
            h, hb = _moe(h, w_router[i], moe_w1.reshape((n_all,) + moe_w1.shape[2:]),
                         moe_w3.reshape((n_all,) + moe_w3.shape[2:]),
                         moe_w2.reshape((n_all,) + moe_w2.shape[2:]), i * N_EXPERTS,
                         ln2_g[layer], ln2_b[layer])

    return h.reshape(bsz, lp, D_MODEL)[:, N_META:N_META + seq]
```

```python
import functools
import math

import jax
import jax.numpy as jnp
from jax import lax
from jax.experimental import pallas as pl
from jax.experimental.pallas import tpu as pltpu

F32 = jnp.float32
BF16 = jnp.bfloat16
I32 = jnp.int32

D_MODEL = 2048
DEPTH = 4
N_META = 16
BLOCK = 128
ROPE_THETA = 500000.0
LN_EPS = 1e-5
RMS_EPS = 1e-6
ALPHA = (2 * DEPTH) ** 0.25

A_HEADS = 8
A_KV_HEADS = 2
A_HEAD_DIM = 128
A_ROT = A_HEAD_DIM // 4
IDX_HEADS = 8
IDX_DIM = 64
IDX_ROT = IDX_DIM // 4
TOPK_MAX = 256

MLA_HEADS = 8
MLA_Q_LORA = 512
MLA_KV_LORA = 512
MLA_NOPE = 128
MLA_ROPE = 64
MLA_V = 128

SSM_WIDTH = 1024
SSM_GROUP = 16
SSM_GROUPS = SSM_WIDTH // SSM_GROUP
SSM_STATE = 64
SSM_CHUNK = 16

N_BRANCH = 3
A_WIDTH = A_HEADS * A_HEAD_DIM
MLA_WIDTH = MLA_HEADS * MLA_V
IN_SIZES = (A_HEADS * A_HEAD_DIM, A_KV_HEADS * A_HEAD_DIM, A_KV_HEADS * A_HEAD_DIM,
            IDX_HEADS * IDX_DIM, IDX_DIM, IDX_HEADS,
            MLA_Q_LORA, MLA_KV_LORA, MLA_ROPE,
            SSM_WIDTH, N_BRANCH * D_MODEL)

D_FF = 5632
N_EXPERTS = 8
TOP_K_EXPERTS = 2

LANES = 128
SUBLANES = 8
VMEM_LIMIT = 56 * 1024 * 1024
INT_MIN = -2 ** 31
LOG2E = math.log2(math.e)
GATHER_UNROLL = 8
MOE_ROWS = 768


def _pick(n, prefs):
    for p in prefs:
        if n % p == 0:
            return p
    raise ValueError(f"no tile in {prefs} divides {n}")


def _cparams(sem):
    return pltpu.CompilerParams(dimension_semantics=sem, vmem_limit_bytes=VMEM_LIMIT)


def _dot(a, b):
    return jnp.dot(a, b, preferred_element_type=F32)


def _dot_nt(a, b):
    return lax.dot_general(a, b, (((1,), (1,)), ((), ())), preferred_element_type=F32)


def _split_bf16(x):
    hi = x.astype(BF16)
    lo = (x - hi.astype(F32)).astype(BF16)
    return hi, lo


def _ln_math(x, g, b):
    mu = jnp.mean(x, axis=-1, keepdims=True)
    xc = x - mu
    var = jnp.mean(xc * xc, axis=-1, keepdims=True)
    return xc * lax.rsqrt(var + LN_EPS) * g + b


def _ln_kernel(x_ref, g_ref, b_ref, o_ref, ob_ref):
    y = _ln_math(x_ref[...], g_ref[...], b_ref[...])
    o_ref[...] = y
    ob_ref[...] = y.astype(BF16)


def _ln_res_kernel(h_ref, r_ref, g_ref, b_ref, o_ref, ob_ref):
    y = _ln_math(ALPHA * h_ref[...] + r_ref[...], g_ref[...], b_ref[...])
    o_ref[...] = y
    ob_ref[...] = y.astype(BF16)


def _moe_ln_kernel(pos_ref, h_ref, ys_ref, gt_ref, g_ref, b_ref, o_ref, ob_ref, y_buf, sem):
    i = pl.program_id(0)
    tm, t = h_ref.shape[0], pos_ref.shape[0] // TOP_K_EXPERTS
    slot = i % 2

    def row_copy(buf, k, src_row, r):
        return pltpu.make_async_copy(ys_ref.at[pl.ds(src_row, 1), :],
                                     y_buf.at[buf, k, pl.ds(r, 1), :], sem.at[buf])

    def start_block(blk, buf):
        def body(r8, carry):
            for k in range(TOP_K_EXPERTS):
                for u in range(GATHER_UNROLL):
                    r = r8 * GATHER_UNROLL + u
                    row_copy(buf, k, pos_ref[k * t + blk * tm + r], r).start()
            return carry
        lax.fori_loop(0, tm // GATHER_UNROLL, body, 0)

    def wait_block(buf):
        def body(r8, carry):
            for k in range(TOP_K_EXPERTS):
                for u in range(GATHER_UNROLL):
                    row_copy(buf, k, 0, r8 * GATHER_UNROLL + u).wait()
            return carry
        lax.fori_loop(0, tm // GATHER_UNROLL, body, 0)

    @pl.when(i == 0)
    def _():
        start_block(0, 0)

    wait_block(slot)

    @pl.when(i + 1 < pl.num_programs(0))
    def _():
        start_block(i + 1, 1 - slot)

    gt = gt_ref[...]
    f = y_buf[slot, 0] * gt[:, 0:1] + y_buf[slot, 1] * gt[:, 1:2]
    y = _ln_math(ALPHA * h_ref[...] + f, g_ref[...], b_ref[...])
    o_ref[...] = y
    ob_ref[...] = y.astype(BF16)


def _layernorm(x, g, b, res=None):
    t, d = x.shape
    tm = _pick(t, (384, 256, 128))
    row = pl.BlockSpec((tm, d), lambda i: (i, 0))
    vec = pl.BlockSpec((1, d), lambda i: (0, 0))
    ins = [x] if res is None else [x, res]
    return pl.pallas_call(
        _ln_kernel if res is None else _ln_res_kernel,
        grid=(t // tm,),
        in_specs=[row] * len(ins) + [vec, vec],
        out_specs=[row, row],
        out_shape=[jax.ShapeDtypeStruct((t, d), F32), jax.ShapeDtypeStruct((t, d), BF16)],
        compiler_params=_cparams(("parallel",)),
        name="layernorm",
    )(*ins, g.reshape(1, d), b.reshape(1, d))


def _proj_ln_kernel(x_ref, w_ref, h_ref, g_ref, b_ref, o_ref, ob_ref):
    y = _ln_math(ALPHA * h_ref[...] + _dot(x_ref[...], w_ref[...]), g_ref[...], b_ref[...])
    o_ref[...] = y
    ob_ref[...] = y.astype(BF16)


def _proj_ln(x, w, layer, h, g, b):
    t, k = x.shape
    d = w.shape[2]
    tm = _pick(t, (384, 256, 128))
    row = pl.BlockSpec((tm, d), lambda i: (i, 0))
    vec = pl.BlockSpec((1, d), lambda i: (0, 0))
    return pl.pallas_call(
        _proj_ln_kernel,
        grid=(t // tm,),
        in_specs=[pl.BlockSpec((tm, k), lambda i: (i, 0)),
                  pl.BlockSpec((None, k, d), lambda i: (layer, 0, 0)), row, vec, vec],
        out_specs=[row, row],
        out_shape=[jax.ShapeDtypeStruct((t, d), F32), jax.ShapeDtypeStruct((t, d), BF16)],
        compiler_params=_cparams(("parallel",)),
        name="proj_o_ln",
    )(x, w, h, g.reshape(1, d), b.reshape(1, d))


def _moe_combine_ln(h, ys, pos, gates_t, g, b):
    t, d = h.shape
    tm = _pick(t, (384, 256, 128))
    row = pl.BlockSpec((tm, d), lambda i, pos_ref: (i, 0))
    vec = pl.BlockSpec((1, d), lambda i, pos_ref: (0, 0))
    grid_spec = pltpu.PrefetchScalarGridSpec(
        num_scalar_prefetch=1,
        grid=(t // tm,),
        in_specs=[row, pl.BlockSpec(memory_space=pl.ANY),
                  pl.BlockSpec((tm, TOP_K_EXPERTS), lambda i, pos_ref: (i, 0)), vec, vec],
        out_specs=[row, row],
        scratch_shapes=[pltpu.VMEM((2, TOP_K_EXPERTS, tm, d), ys.dtype), pltpu.SemaphoreType.DMA((2,))],
    )
    return pl.pallas_call(
        _moe_ln_kernel,
        grid_spec=grid_spec,
        out_shape=[jax.ShapeDtypeStruct((t, d), F32), jax.ShapeDtypeStruct((t, d), BF16)],
        compiler_params=_cparams(("arbitrary",)),
        name="moe_combine_ln",
    )(pos, h, ys, gates_t, g.reshape(1, d), b.reshape(1, d))


def _rope_chunk(a, c, sa, sb, r):
    return a * c + pltpu.roll(a, LANES - r, 1) * sa + pltpu.roll(a, r, 1) * sb


def _proj_kernel(*refs, rope_r, has_bias, sigmoid, hi_prec, rms):
    it = iter(refs)
    x_ref = next(it)
    w_ref = next(it)
    n_ref = next(it) if rms else None
    tabs = (next(it), next(it), next(it)) if rope_r else None
    b_ref = next(it) if has_bias else None
    o_ref = next(it)
    x = x_ref[...]
    if rms:
        ms = jnp.mean(x * x, axis=-1, keepdims=True)
        x = x * lax.rsqrt(ms + RMS_EPS) * n_ref[...]
    if hi_prec:
        xh, xl = _split_bf16(x)
        wh, wl = _split_bf16(w_ref[...])
        acc = _dot(xh, wh) + _dot(xl, wh) + _dot(xh, wl)
    else:
        acc = _dot(x.astype(BF16), w_ref[...].astype(BF16))
    if has_bias:
        acc = acc + b_ref[...]
    if sigmoid:
        acc = jax.nn.sigmoid(acc)
    if rope_r:
        c_ref, sa_ref, sb_ref = tabs
        pw = c_ref.shape[1]
        for c in range(acc.shape[1] // LANES):
            sl = slice(c * LANES, (c + 1) * LANES)
            ts = slice((c * LANES) % pw, (c * LANES) % pw + LANES)
            o_ref[:, sl] = _rope_chunk(acc[:, sl], c_ref[:, ts], sa_ref[:, ts], sb_ref[:, ts],
                                       rope_r).astype(o_ref.dtype)
    else:
        o_ref[...] = acc.astype(o_ref.dtype)


def _proj(x, w, *, out_dtype, lp, x_col=0, kdim=None, w_cols=None, rope=None, bias=None, sigmoid=False,
          hi_prec=False, rms_gain=None, tm_prefs=(1056, 768, 384, 128), tn_prefs=(512, 384, 256, 128),
          name="proj"):
    t = x.shape[0]
    kdim = x.shape[1] if kdim is None else kdim
    tm = _pick(lp, tm_prefs)
    nrow = lp // tm
    if w_cols is None:
        n = w.shape[1]
        tn = _pick(n, tn_prefs)
        w_spec = pl.BlockSpec((kdim, tn), lambda i, j: (0, j))
    else:
        layer, col0, n = w_cols
        tn = _pick(math.gcd(n, col0) if col0 else n, tn_prefs)
        cb0 = col0 // tn
        w_spec = pl.BlockSpec((None, kdim, tn), lambda i, j: (layer, 0, cb0 + j))
    in_specs = [pl.BlockSpec((tm, kdim), lambda i, j: (i, x_col)), w_spec]
    ins = [x, w]
    if rms_gain is not None:
        in_specs.append(pl.BlockSpec((1, kdim), lambda i, j: (0, 0)))
        ins.append(rms_gain.reshape(1, kdim))
    if rope is not None:
        pw = rope[0].shape[1]
        if pw == n:
            tab = pl.BlockSpec((tm, tn), lambda i, j: (i % nrow, j))
        else:
            assert tn % pw == 0
            tab = pl.BlockSpec((tm, pw), lambda i, j: (i % nrow, 0))
        in_specs += [tab, tab, tab]
        ins += list(rope[:3])
    if bias is not None:
        in_specs.append(pl.BlockSpec((1, tn), lambda i, j: (0, j)))
        ins.append(bias.reshape(1, n))
    kern = functools.partial(_proj_kernel, rope_r=rope[3] if rope is not None else 0,
                             has_bias=bias is not None, sigmoid=sigmoid, hi_prec=hi_prec,
                             rms=rms_gain is not None)
    return pl.pallas_call(
        kern,
        grid=(t // tm, n // tn),
        in_specs=in_specs,
        out_specs=pl.BlockSpec((tm, tn), lambda i, j: (i, j)),
        out_shape=jax.ShapeDtypeStruct((t, n), out_dtype),
        compiler_params=_cparams(("parallel", "arbitrary")),
        name=name,
    )(*ins)


def _mla_down_kernel(x_ref, w_ref, c_ref, sa_ref, sb_ref, o_ref, kr_ref):
    acc = _dot(x_ref[...], w_ref[...].astype(BF16))
    nq = o_ref.shape[1]
    o_ref[...] = acc[:, :nq]
    kr_ref[...] = _rope_chunk(acc[:, nq:], c_ref[...], sa_ref[...], sb_ref[...],
                              MLA_ROPE // 2).astype(BF16)


def _mla_down(hb, w, rope, lp):
    t, d = hb.shape
    n = w.shape[1]
    nq = MLA_Q_LORA + MLA_KV_LORA
    tm = _pick(lp, (704, 384, 128))
    nrow = lp // tm
    tab = pl.BlockSpec((tm, LANES), lambda i: (i % nrow, 0))
    return pl.pallas_call(
        _mla_down_kernel,
        grid=(t // tm,),
        in_specs=[pl.BlockSpec((tm, d), lambda i: (i, 0)), pl.BlockSpec((d, n), lambda i: (0, 0)),
                  tab, tab, tab],
        out_specs=[pl.BlockSpec((tm, nq), lambda i: (i, 0)), pl.BlockSpec((tm, LANES), lambda i: (i, 0))],
        out_shape=[jax.ShapeDtypeStruct((t, nq), F32), jax.ShapeDtypeStruct((t, LANES), BF16)],
        compiler_params=_cparams(("parallel",)),
        name="mla_down",
    )(hb, w, *rope[:3])


def _glu_kernel(y_ref, w_ref, o_ref):
    n = o_ref.shape[1]
    y = y_ref[...].astype(BF16)
    ga = _dot(y, w_ref[:, :n].astype(BF16))
    gb = _dot(y, w_ref[:, n:].astype(BF16))
    o_ref[...] = (ga * jax.nn.sigmoid(gb)).astype(o_ref.dtype)


def _glu(y, w_glu, layer, lp):
    t, k = y.shape
    n = w_glu.shape[2] // 2
    tm = _pick(lp, (704, 384, 128))
    return pl.pallas_call(
        _glu_kernel,
        grid=(t // tm,),
        in_specs=[pl.BlockSpec((tm, k), lambda i: (i, 0)),
                  pl.BlockSpec((None, k, 2 * n), lambda i: (layer, 0, 0))],
        out_specs=pl.BlockSpec((tm, n), lambda i: (i, 0)),
        out_shape=jax.ShapeDtypeStruct((t, n), BF16),
        compiler_params=_cparams(("parallel",)),
        name="glu",
    )(y, w_glu)


def _merge_kernel(h_ref, a_ref, b_ref, c_ref, wg0_ref, wg1_ref, wg2_ref, bg_ref,
                  wa_ref, wb_ref, wc_ref, o_ref):
    h = h_ref[...]
    m = None
    for k, (wg_ref, x_ref, w_ref) in enumerate(((wg0_ref, a_ref, wa_ref), (wg1_ref, b_ref, wb_ref),
                                                (wg2_ref, c_ref, wc_ref))):
        gate = jax.nn.sigmoid(_dot(h, wg_ref[...].astype(BF16)) + bg_ref[k])
        term = gate * _dot(x_ref[...], w_ref[...].astype(BF16))
        m = term if m is None else m + term
    o_ref[...] = m.astype(o_ref.dtype)


def _merge(hb, out_a, out_b, out_c, w_gates, b_gate, wb_a, wb_b, wb_c, layer, lp):
    t = out_a.shape[0]
    n = wb_a.shape[2]
    tm = _pick(lp, (704, 384, 128))
    tn = _pick(n, (256, 128))

    def act(arr):
        return pl.BlockSpec((tm, arr.shape[1]), lambda i, j: (i, 0))

    def wgt(arr):
        return pl.BlockSpec((None, arr.shape[1], tn), lambda i, j: (layer, 0, j))

    gate_w = pl.BlockSpec((hb.shape[1], tn), lambda i, j: (0, j))
    return pl.pallas_call(
        _merge_kernel,
        grid=(t // tm, n // tn),
        in_specs=[act(hb), act(out_a), act(out_b), act(out_c), gate_w, gate_w, gate_w,
                  pl.BlockSpec((N_BRANCH, 1, tn), lambda i, j: (0, 0, j)),
                  wgt(wb_a), wgt(wb_b), wgt(wb_c)],
        out_specs=pl.BlockSpec((tm, tn), lambda i, j: (i, j)),
        out_shape=jax.ShapeDtypeStruct((t, n), BF16),
        compiler_params=_cparams(("parallel", "arbitrary")),
        name="merge",
    )(hb, out_a, out_b, out_c, *w_gates, b_gate.reshape(N_BRANCH, 1, n), wb_a, wb_b, wb_c)


def _softmax_update(s, v, m_ref, l_ref, acc_ref):
    tk = s.shape[1]
    m_prev = m_ref[...]
    m_new = jnp.maximum(m_prev, jnp.max(s, axis=1, keepdims=True))
    alpha = jnp.exp2(m_prev - m_new)
    p = jnp.exp2(s - jnp.concatenate([m_new] * (tk // LANES), axis=1))
    l_ref[...] = alpha * l_ref[...] + jnp.sum(p, axis=1, keepdims=True)
    acc_ref[...] = alpha * acc_ref[...] + _dot(p.astype(BF16), v)
    m_ref[...] = m_new


def _mla_attn_kernel(q_ref, kv_ref, kr_ref, o_ref, s_ref, m_ref, l_ref, acc_ref):
    qi = pl.program_id(2)
    tq = q_ref.shape[0]
    m_ref[...] = jnp.full(m_ref.shape, -jnp.inf, F32)
    l_ref[...] = jnp.zeros(l_ref.shape, F32)
    acc_ref[...] = jnp.zeros(acc_ref.shape, F32)
    q = q_ref[...]

    def rows_of(kt):
        return pl.ds(pl.multiple_of(kt * tq, tq), tq)

    def produce(buf, kt):
        rows = rows_of(jnp.minimum(kt, qi))
        s_ref[buf] = _dot_nt(q, jnp.concatenate([kv_ref[rows, :MLA_NOPE], kr_ref[rows, :]], axis=1))

    def consume(s, kt):
        _softmax_update(s, kv_ref[rows_of(kt), MLA_NOPE:], m_ref, l_ref, acc_ref)

    produce(0, 0)
    produce(1, 1)

    def body(kp, carry):
        s0 = s_ref[0]
        s1 = s_ref[1]
        produce(0, 2 * kp + 2)
        produce(1, 2 * kp + 3)
        consume(s0, 2 * kp)
        consume(s1, 2 * kp + 1)
        return carry

    lax.fori_loop(0, qi // 2, body, 0)

    @pl.when(qi % 2 == 1)
    def _():
        consume(s_ref[0], qi - 1)

    row = lax.broadcasted_iota(I32, (tq, tq), 0)
    col = lax.broadcasted_iota(I32, (tq, tq), 1)
    consume(jnp.where(col <= row, s_ref[qi % 2], -jnp.inf), qi)
    o_ref[...] = (acc_ref[...] / l_ref[...]).astype(o_ref.dtype)


def _mla_attention(q, kv, kr, bsz, lp):
    t = q.shape[0]
    tq = _pick(lp, (384, 128))
    nq = lp // tq
    q3 = q.reshape(bsz, lp, q.shape[1])
    kv3 = kv.reshape(bsz, lp, kv.shape[1])
    kr3 = kr.reshape(bsz, lp, LANES)
    out = pl.pallas_call(
        _mla_attn_kernel,
        grid=(bsz, MLA_HEADS, nq),
        in_specs=[
            pl.BlockSpec((None, tq, 2 * LANES), lambda b, h, i: (b, i, h)),
            pl.BlockSpec((None, lp, MLA_NOPE + MLA_V), lambda b, h, i: (b, 0, h)),
            pl.BlockSpec((None, lp, LANES), lambda b, h, i: (b, 0, 0)),
        ],
        out_specs=pl.BlockSpec((None, tq, LANES), lambda b, h, i: (b, i, h)),
        out_shape=jax.ShapeDtypeStruct((bsz, lp, MLA_WIDTH), BF16),
        scratch_shapes=[pltpu.VMEM((2, tq, tq), F32),
                        pltpu.VMEM((tq, LANES), F32), pltpu.VMEM((tq, LANES), F32),
                        pltpu.VMEM((tq, MLA_V), F32)],
        compiler_params=_cparams(("parallel", "parallel", "arbitrary")),
        name="mla_attention",
    )(q3, kv3, kr3)
    return out.reshape(t, MLA_WIDTH)


def _dsa_attn_kernel(q_ref, k_ref, v_ref, bias_ref, o_ref, qg_ref, s_ref, m_ref, l_ref, acc_ref, *, tk):
    qi = pl.program_id(1)
    rep = A_HEADS // A_KV_HEADS
    nkt = tk // BLOCK
    m_ref[...] = jnp.full(m_ref.shape, -jnp.inf, F32)
    l_ref[...] = jnp.zeros(l_ref.shape, F32)
    acc_ref[...] = jnp.zeros(acc_ref.shape, F32)
    for g in range(A_KV_HEADS):
        for r in range(rep):
            h = g * rep + r
            qg_ref[g, r * BLOCK:(r + 1) * BLOCK, :] = q_ref[:, h * A_HEAD_DIM:(h + 1) * A_HEAD_DIM]

    def rows_of(kt):
        return pl.ds(pl.multiple_of(kt * tk, tk), tk)

    def scores(kt):
        rows = rows_of(kt)
        for g in range(A_KV_HEADS):
            s_ref[g] = _dot_nt(qg_ref[g], k_ref[rows, g * A_HEAD_DIM:(g + 1) * A_HEAD_DIM])

    def consume(kt, s):
        bias = jnp.concatenate([bias_ref[kt * nkt + j] for j in range(nkt)], axis=1)
        bias = jnp.concatenate([bias.astype(F32)] * rep, axis=0)
        for g in range(A_KV_HEADS):
            _softmax_update(s[g] + bias, v_ref[rows_of(kt), g * A_HEAD_DIM:(g + 1) * A_HEAD_DIM],
                            m_ref.at[g], l_ref.at[g], acc_ref.at[g])

    scores(0)

    def body(kt, carry):
        s = [s_ref[g] for g in range(A_KV_HEADS)]
        scores(kt + 1)
        consume(kt, s)
        return carry

    last = (qi * BLOCK) // tk
    lax.fori_loop(0, last, body, 0)
    consume(last, [s_ref[g] for g in range(A_KV_HEADS)])
    for g in range(A_KV_HEADS):
        o = acc_ref[g] / l_ref[g]
        for r in range(rep):
            h = g * rep + r
            o_ref[:, h * A_HEAD_DIM:(h + 1) * A_HEAD_DIM] = (
                o[r * BLOCK:(r + 1) * BLOCK]).astype(o_ref.dtype)


def _dsa_attention(q, k, v, bias, bsz, lp):
    t = q.shape[0]
    nb = lp // BLOCK
    tk = _pick(lp, (384, 128))
    kvw = A_KV_HEADS * A_HEAD_DIM
    q3 = q.reshape(bsz, lp, A_WIDTH)
    k3 = k.reshape(bsz, lp, kvw)
    v3 = v.reshape(bsz, lp, kvw)
    rows = (A_HEADS // A_KV_HEADS) * BLOCK
    out = pl.pallas_call(
        functools.partial(_dsa_attn_kernel, tk=tk),
        grid=(bsz, nb),
        in_specs=[
            pl.BlockSpec((None, BLOCK, A_WIDTH), lambda b, i: (b, i, 0)),
            pl.BlockSpec((None, lp, kvw), lambda b, i: (b, 0, 0)),
            pl.BlockSpec((None, lp, kvw), lambda b, i: (b, 0, 0)),
            pl.BlockSpec((None, None, nb, BLOCK, BLOCK), lambda b, i: (b, i, 0, 0, 0)),
        ],
        out_specs=pl.BlockSpec((None, BLOCK, A_WIDTH), lambda b, i: (b, i, 0)),
        out_shape=jax.ShapeDtypeStruct((bsz, lp, A_WIDTH), BF16),
        scratch_shapes=[pltpu.VMEM((A_KV_HEADS, rows, A_HEAD_DIM), BF16),
                        pltpu.VMEM((A_KV_HEADS, rows, tk), F32),
                        pltpu.VMEM((A_KV_HEADS, rows, LANES), F32),
                        pltpu.VMEM((A_KV_HEADS, rows, LANES), F32),
                        pltpu.VMEM((A_KV_HEADS, rows, A_HEAD_DIM), F32)],
        compiler_params=_cparams(("parallel", "arbitrary")),
        name="dsa_attention",
    )(q3, k3, v3, bias)
    return out.reshape(t, A_WIDTH)


def _indexer_kernel(iq_ref, kw_ref, o_ref, kcat_ref, qcat_ref, wb_ref, key_ref, *, n_sel):
    qb = pl.program_id(0)
    bsz, nb = o_ref.shape[0], o_ref.shape[1]
    batch = range(bsz)
    half = IDX_DIM
    lane = lax.broadcasted_iota(I32, (BLOCK, LANES), 1)
    row = lax.broadcasted_iota(I32, (BLOCK, LANES), 0)

    @pl.when(qb == 0)
    def _():
        for b in batch:
            kf = kw_ref[b]
            klane = lax.broadcasted_iota(I32, kf.shape, 1)
            kz = jnp.where(klane < half, kf, 0.0)
            hi = kz.astype(BF16).astype(F32)
            lo = kz - hi
            lp = kf.shape[0]
            kcat_ref[b, 0:lp, :] = jnp.concatenate(
                [(hi + pltpu.roll(hi, half, 1)).astype(BF16), lo.astype(BF16)], axis=1)
            kcat_ref[b, lp:lp + BLOCK, :] = jnp.zeros((BLOCK, 2 * LANES), BF16)

    for b in batch:
        wq = kw_ref[b, pl.ds(pl.multiple_of(qb * BLOCK, BLOCK), BLOCK), :]
        for h in range(IDX_HEADS):
            chunk = iq_ref[b, :, (h // 2) * LANES:(h // 2 + 1) * LANES]
            if h % 2 == 0:
                a = jnp.where(lane < half, chunk, 0.0)
            else:
                a = pltpu.roll(jnp.where(lane >= half, chunk, 0.0), half, 1)
            hi = a.astype(BF16).astype(F32)
            lo = a - hi
            qcat_ref[b, h * BLOCK:(h + 1) * BLOCK, :] = jnp.concatenate(
                [(hi + pltpu.roll(lo, half, 1)).astype(BF16), hi.astype(BF16)], axis=1)
            wb_ref[b, h] = jnp.broadcast_to(wq[:, half + h:half + h + 1], (BLOCK, LANES))

    qpos = qb * BLOCK + row

    def sortable(x):
        b = pltpu.bitcast(x, I32)
        return b ^ ((b >> 31) & 0x7FFFFFFF)

    n_pair = (qb + 2) // 2

    def score_pair(kp, carry):
        for b in batch:
            kblk = kcat_ref[b, pl.ds(pl.multiple_of(kp * 2 * BLOCK, 2 * BLOCK), 2 * BLOCK), :]
            s = _dot_nt(qcat_ref[b], kblk)
            for j in range(2):
                sc = jnp.zeros((BLOCK, LANES), F32)
                for h in range(IDX_HEADS):
                    sc = sc + (jnp.maximum(s[h * BLOCK:(h + 1) * BLOCK, j * LANES:(j + 1) * LANES], 0.0)
                               * wb_ref[b, h])
                kpos = (kp * 2 + j) * BLOCK + lane
                sc = jnp.where(kpos < N_META, jnp.inf, sc)
                sc = jnp.where(kpos <= qpos, sc, -jnp.inf)
                key_ref[b, kp * 2 + j] = sortable(sc)
        return carry

    lax.fori_loop(0, n_pair, score_pair, 0)

    def count(pred):
        def body(kp, cs):
            return tuple(c + jnp.where(pred(key_ref[b, kp * 2], b), 1.0, 0.0)
                         + jnp.where(pred(key_ref[b, kp * 2 + 1], b), 1.0, 0.0)
                         for b, c in zip(batch, cs))
        cs = lax.fori_loop(0, n_pair, body, tuple(jnp.zeros((BLOCK, LANES), F32) for _ in batch))
        return tuple(jnp.sum(c, axis=1, keepdims=True) for c in cs)

    def bit_step(i, thrs):
        bit = 31 - i
        cands = tuple(jnp.where(bit == 31, thr ^ INT_MIN, thr | (1 << jnp.minimum(bit, 30)))
                      for thr in thrs)
        counts = count(lambda k, b: k >= cands[b])
        return tuple(jnp.where(n >= n_sel, cand, thr) for n, cand, thr in zip(counts, cands, thrs))

    thrs = lax.fori_loop(0, 32, bit_step,
                         tuple(jnp.full((BLOCK, LANES), INT_MIN, I32) for _ in batch))
    n_ge = count(lambda k, b: k >= thrs[b])
    all_ties_fit = functools.reduce(jnp.maximum, [jnp.max(n) for n in n_ge]) <= n_sel

    @pl.when(all_ties_fit)
    def _():
        def emit(kt, carry):
            kpos = kt * BLOCK + lane
            for b in batch:
                o_ref[b, kt] = jnp.where((key_ref[b, kt] >= thrs[b]) & (kpos <= qpos),
                                         0.0, -jnp.inf).astype(o_ref.dtype)
            return carry

        lax.fori_loop(0, qb + 1, emit, 0)

    @pl.when(jnp.logical_not(all_ties_fit))
    def _():
        n_gt = count(lambda k, b: k > thrs[b])
        tri = (lax.broadcasted_iota(I32, (LANES, LANES), 0)
               <= lax.broadcasted_iota(I32, (LANES, LANES), 1)).astype(BF16)

        def emit(kt, takens):
            out = []
            kpos = kt * BLOCK + lane
            for b in batch:
                key = key_ref[b, kt]
                eq = key == thrs[b]
                rank = _dot(jnp.where(eq, 1.0, 0.0).astype(BF16), tri)
                sel = (key > thrs[b]) | (eq & (takens[b] + rank <= n_sel - n_gt[b]))
                o_ref[b, kt] = jnp.where(sel & (kpos <= qpos), 0.0, -jnp.inf).astype(o_ref.dtype)
                out.append(takens[b] + rank[:, LANES - 1:LANES])
            return tuple(out)

        lax.fori_loop(0, qb + 1, emit, tuple(jnp.zeros((BLOCK, 1), F32) for _ in batch))

    def fill(kt, carry):
        for b in batch:
            o_ref[b, kt] = jnp.full((BLOCK, LANES), -jnp.inf, o_ref.dtype)
        return carry

    lax.fori_loop(qb + 1, nb, fill, 0)


def _indexer(idx, bsz, lp, n_sel):
    nb = lp // BLOCK
    idx3 = idx.reshape(bsz, lp, idx.shape[1])
    nq = IDX_HEADS * IDX_DIM
    kw_col = nq // LANES
    return pl.pallas_call(
        functools.partial(_indexer_kernel, n_sel=n_sel),
        grid=(nb,),
        in_specs=[
            pl.BlockSpec((bsz, BLOCK, nq), lambda i: (0, i, 0)),
            pl.BlockSpec((bsz, lp, LANES), lambda i: (0, 0, kw_col)),
        ],
        out_specs=pl.BlockSpec((bsz, None, nb, BLOCK, BLOCK), lambda i: (0, i, 0, 0, 0)),
        out_shape=jax.ShapeDtypeStruct((bsz, nb, nb, BLOCK, BLOCK), BF16),
        scratch_shapes=[pltpu.VMEM((bsz, lp + BLOCK, 2 * LANES), BF16),
                        pltpu.VMEM((bsz, IDX_HEADS * BLOCK, 2 * LANES), BF16),
                        pltpu.VMEM((bsz, IDX_HEADS, BLOCK, LANES), F32),
                        pltpu.VMEM((bsz, nb + 1, BLOCK, LANES), I32)],
        compiler_params=_cparams(("arbitrary",)),
        name="indexer",
    )(idx3, idx3)


def _s5_kernel(x_ref, mw_ref, vre_ref, vim_ref, are_ref, aim_ref, o_ref,
               perm_ref, y_ref, yb_ref, sre_ref, sim_ref, pre_ref, pim_ref, *, bsz):
    tc, c, p = SSM_CHUNK, SSM_GROUP, SSM_STATE
    ng = LANES // c
    nm = tc * c
    rows = x_ref.shape[0] // tc
    per_b = rows // bsz
    width = tc * LANES

    @pl.when(pl.program_id(0) == 0)
    def _():
        src = lax.broadcasted_iota(I32, (width, width), 0)
        dst = lax.broadcasted_iota(I32, (width, width), 1)
        group = (src & (LANES - 1)) >> (c.bit_length() - 1)
        step_in_chunk = src >> (LANES.bit_length() - 1)
        want = group * nm + step_in_chunk * c + (src & (c - 1))
        perm_ref[...] = jnp.where(dst == want, 1.0, 0.0).astype(BF16)

    x_all = jnp.concatenate(
        [x_ref[pl.ds(j, rows, stride=tc), :].astype(BF16) for j in range(tc)], axis=1)
    u_all = _dot(x_all, perm_ref[...]).astype(BF16)
    for g in range(ng):
        r = _dot(u_all[:, g * nm:(g + 1) * nm], mw_ref[g])
        y_ref[:, g * nm:(g + 1) * nm] = r[:, :nm]
        sre_ref[:, g * p:(g + 1) * p] = r[:, nm:nm + p]
        sim_ref[:, g * p:(g + 1) * p] = r[:, nm + p:nm + 2 * p]
    ar = are_ref[...]
    ai = aim_ref[...]

    def step(ti, carry):
        out = []
        for b in range(bsz):
            cr, ci = carry[b]
            rs = pl.ds(pl.multiple_of(b * per_b + ti * SUBLANES, SUBLANES), SUBLANES)
            lr = sre_ref[rs, :]
            li = sim_ref[rs, :]
            before_r, before_i = [], []
            for k in range(SUBLANES):
                before_r.append(cr)
                before_i.append(ci)
                cr, ci = (ar * cr - ai * ci + lr[k:k + 1], ar * ci + ai * cr + li[k:k + 1])
            pre_ref[rs, :] = jnp.concatenate(before_r, axis=0)
            pim_ref[rs, :] = jnp.concatenate(before_i, axis=0)
            out.append((cr, ci))
        return tuple(out)

    zero = jnp.zeros((1, ng * p), F32)
    lax.fori_loop(0, per_b // SUBLANES, step, tuple((zero, zero) for _ in range(bsz)))
    for g in range(ng):
        y = (y_ref[:, g * nm:(g + 1) * nm]
             + _dot(pre_ref[:, g * p:(g + 1) * p].astype(BF16), vre_ref[g])
             + _dot(pim_ref[:, g * p:(g + 1) * p].astype(BF16), vim_ref[g]))
        yb_ref[:, g * nm:(g + 1) * nm] = y.astype(BF16)
    y_all = _dot_nt(yb_ref[...], perm_ref[...])
    for j in range(tc):
        o_ref[pl.ds(j, rows, stride=tc), :] = y_all[:, j * LANES:(j + 1) * LANES]


def _s5_weights(a_re, a_im, log_dt, b_re, b_im, c_re, c_im, d_skip):
    hp = lax.Precision.HIGHEST
    g, p, c, tc = SSM_GROUPS, SSM_STATE, SSM_GROUP, SSM_CHUNK
    dt = jnp.exp(log_dt)[:, None]
    lam_re, lam_im = dt * a_re, dt * a_im
    mag = jnp.exp(lam_re)
    ab_re, ab_im = mag * jnp.cos(lam_im), mag * jnp.sin(lam_im)
    den = a_re * a_re + a_im * a_im
    f_re = ((ab_re - 1.0) * a_re + ab_im * a_im) / den
    f_im = (ab_im * a_re - (ab_re - 1.0) * a_im) / den
    bb_re = f_re[..., None] * b_re - f_im[..., None] * b_im
    bb_im = f_re[..., None] * b_im + f_im[..., None] * b_re
    d = jnp.arange(tc + 1, dtype=F32)[:, None, None]
    pmag = jnp.exp(d * lam_re)
    pw_re, pw_im = pmag * jnp.cos(d * lam_im), pmag * jnp.sin(d * lam_im)
    z_re = pw_re[:tc, :, :, None] * bb_re - pw_im[:tc, :, :, None] * bb_im
    z_im = pw_re[:tc, :, :, None] * bb_im + pw_im[:tc, :, :, None] * bb_re
    kmat = (jnp.einsum('gop,dgpi->gdio', c_re, z_re, precision=hp)
            - jnp.einsum('gop,dgpi->gdio', c_im, z_im, precision=hp))
    ti = jnp.arange(tc)
    lag = ti[None, :] - ti[:, None]
    place = (lag[None] == ti[:, None, None]).astype(F32)
    m = jnp.einsum('dij,gdxy->gixjy', place, kmat, precision=hp).reshape(g, tc * c, tc * c)
    m = m + jnp.eye(tc * c, dtype=F32) * jnp.tile(d_skip.reshape(g, 1, c), (1, tc, 1)).reshape(g, 1, tc * c)
    w_re = jnp.transpose(z_re[::-1], (1, 0, 3, 2)).reshape(g, tc * c, p)
    w_im = jnp.transpose(z_im[::-1], (1, 0, 3, 2)).reshape(g, tc * c, p)
    mw = jnp.concatenate([m, w_re, w_im], axis=2).astype(BF16)
    q_re, q_im = pw_re[1:], pw_im[1:]
    v_re = c_re[None] * q_re[:, :, None, :] - c_im[None] * q_im[:, :, None, :]
    v_im = c_re[None] * q_im[:, :, None, :] + c_im[None] * q_re[:, :, None, :]
    v_re = jnp.transpose(v_re, (1, 3, 0, 2)).reshape(g, p, tc * c).astype(BF16)
    v_im = jnp.transpose(-v_im, (1, 3, 0, 2)).reshape(g, p, tc * c).astype(BF16)
    return mw, v_re, v_im, pw_re[tc].reshape(1, g * p), pw_im[tc].reshape(1, g * p)


def _s5(su, weights, bsz, lp):
    del lp
    mw, v_re, v_im, a_re, a_im = weights
    t = su.shape[0]
    c, tc, p = SSM_GROUP, SSM_CHUNK, SSM_STATE
    ng = LANES // c
    rows = t // tc
    return pl.pallas_call(
        functools.partial(_s5_kernel, bsz=bsz),
        grid=(SSM_WIDTH // LANES,),
        in_specs=[
            pl.BlockSpec((t, LANES), lambda i: (0, i)),
            pl.BlockSpec((ng, tc * c, tc * c + 2 * p), lambda i: (i, 0, 0)),
            pl.BlockSpec((ng, p, tc * c), lambda i: (i, 0, 0)),
            pl.BlockSpec((ng, p, tc * c), lambda i: (i, 0, 0)),
            pl.BlockSpec((1, ng * p), lambda i: (0, i)),
            pl.BlockSpec((1, ng * p), lambda i: (0, i)),
        ],
        out_specs=pl.BlockSpec((t, LANES), lambda i: (0, i)),
        out_shape=jax.ShapeDtypeStruct((t, SSM_WIDTH), F32),
        scratch_shapes=[pltpu.VMEM((tc * LANES, tc * LANES), BF16),
                        pltpu.VMEM((rows, tc * LANES), F32), pltpu.VMEM((rows, tc * LANES), BF16),
                        pltpu.VMEM((rows, ng * p), F32), pltpu.VMEM((rows, ng * p), F32),
                        pltpu.VMEM((rows, ng * p), F32), pltpu.VMEM((rows, ng * p), F32)],
        compiler_params=_cparams(("arbitrary",)),
        name="s5",
    )(su, mw, v_re, v_im, a_re, a_im)


def _swiglu_accumulate(x, w1_ref, w3_ref, w2_ref, o_ref):
    a = _dot(x, w1_ref[0].astype(BF16))
    b = _dot(x, w3_ref[0].astype(BF16))
    act = (a * jax.nn.sigmoid(a) * b).astype(BF16)
    o_ref[...] += _dot(act, w2_ref[0].astype(BF16))


def _ffn_kernel(be_ref, nu_ref, x_ref, w1_ref, w3_ref, w2_ref, o_ref):
    i = pl.program_id(0)
    f = pl.program_id(1)

    @pl.when(f == 0)
    def _():
        o_ref[...] = jnp.zeros(o_ref.shape, F32)

    @pl.when(i < nu_ref[0])
    def _():
        _swiglu_accumulate(x_ref[...].astype(BF16), w1_ref, w3_ref, w2_ref, o_ref)


def _ffn_gather_kernel(be_ref, nu_ref, tok_ref, h_ref, w1_ref, w3_ref, w2_ref, o_ref, x_buf, sem):
    i = pl.program_id(0)
    f = pl.program_id(1)
    tm = o_ref.shape[0]
    nu = nu_ref[0]
    slot = i % 2

    def row_copy(buf, tok, r):
        return pltpu.make_async_copy(h_ref.at[pl.ds(tok, 1), :], x_buf.at[buf, pl.ds(r, 1), :],
                                     sem.at[buf])

    def start_block(blk, buf):
        def body(r8, carry):
            for k in range(GATHER_UNROLL):
                r = r8 * GATHER_UNROLL + k
                row_copy(buf, tok_ref[blk * tm + r], r).start()
            return carry
        lax.fori_loop(0, tm // GATHER_UNROLL, body, 0)

    def wait_block(buf):
        def body(r8, carry):
            for k in range(GATHER_UNROLL):
                row_copy(buf, 0, r8 * GATHER_UNROLL + k).wait()
            return carry
        lax.fori_loop(0, tm // GATHER_UNROLL, body, 0)

    @pl.when(f == 0)
    def _():
        o_ref[...] = jnp.zeros(o_ref.shape, F32)

        @pl.when((i == 0) & (nu > 0))
        def _():
            start_block(0, 0)

        @pl.when(i < nu)
        def _():
            wait_block(slot)

        @pl.when(i + 1 < nu)
        def _():
            start_block(i + 1, 1 - slot)

    @pl.when(i < nu)
    def _():
        _swiglu_accumulate(x_buf[slot].astype(BF16), w1_ref, w3_ref, w2_ref, o_ref)


def _ffn(x, w1, w3, w2, blk_expert, n_used, tm, row_token=None):
    d = x.shape[1]
    n = x.shape[0] if row_token is None else row_token.shape[0]
    dff = w1.shape[2]
    tf = _pick(dff, (256, 128))
    nf = dff // tf

    def live(i, nu):
        return jnp.minimum(i, nu[0] - 1)

    def fidx(i, f, nu):
        return jnp.where(i < nu[0], f, nf - 1)

    w_specs = [
        pl.BlockSpec((1, d, tf), lambda i, f, be, nu, *_: (be[live(i, nu)], 0, fidx(i, f, nu))),
        pl.BlockSpec((1, d, tf), lambda i, f, be, nu, *_: (be[live(i, nu)], 0, fidx(i, f, nu))),
        pl.BlockSpec((1, tf, d), lambda i, f, be, nu, *_: (be[live(i, nu)], fidx(i, f, nu), 0)),
    ]
    if row_token is None:
        kern, prefetch, scratch = _ffn_kernel, (blk_expert, n_used), []
        x_spec = pl.BlockSpec((tm, d), lambda i, f, be, nu: (live(i, nu), 0))
    else:
        kern, prefetch = _ffn_gather_kernel, (blk_expert, n_used, row_token)
        scratch = [pltpu.VMEM((2, tm, d), x.dtype), pltpu.SemaphoreType.DMA((2,))]
        x_spec = pl.BlockSpec(memory_space=pl.ANY)
    grid_spec = pltpu.PrefetchScalarGridSpec(
        num_scalar_prefetch=len(prefetch),
        grid=(n // tm, nf),
        in_specs=[x_spec] + w_specs,
        out_specs=pl.BlockSpec((tm, d), lambda i, f, *_: (i, 0)),
        scratch_shapes=scratch,
    )
    return pl.pallas_call(
        kern,
        grid_spec=grid_spec,
        out_shape=jax.ShapeDtypeStruct((n, d), F32),
        compiler_params=_cparams(("arbitrary", "arbitrary")),
        name="swiglu",
    )(*prefetch, x, w1, w3, w2)


def _router_kernel(h_ref, w_ref, idx_ref, gate_ref):
    hh, hl = _split_bf16(h_ref[...])
    wh, wl = _split_bf16(w_ref[...])
    logits = _dot_nt(wh, hh) + _dot_nt(wl, hh) + _dot_nt(wh, hl)
    e = lax.broadcasted_iota(I32, logits.shape, 0).astype(F32)
    m1 = jnp.max(logits, axis=0, keepdims=True)
    i1 = jnp.min(jnp.where(logits == m1, e, float(N_EXPERTS)), axis=0, keepdims=True)
    rest = jnp.where(e == i1, -jnp.inf, logits)
    m2 = jnp.max(rest, axis=0, keepdims=True)
    i2 = jnp.min(jnp.where(rest == m2, e, float(N_EXPERTS)), axis=0, keepdims=True)
    e2 = jnp.exp(m2 - m1)
    den = 1.0 + e2
    idx_ref[...] = jnp.concatenate([i1, i2], axis=0).astype(I32)
    gate_ref[...] = jnp.concatenate([1.0 / den, e2 / den], axis=0)


def _router(h, w_router_t):
    t, d = h.shape
    tm = _pick(t, (384, 256, 128))
    return pl.pallas_call(
        _router_kernel,
        grid=(t // tm,),
        in_specs=[pl.BlockSpec((tm, d), lambda i: (i, 0)), pl.BlockSpec((N_EXPERTS, d), lambda i: (0, 0))],
        out_specs=[pl.BlockSpec((TOP_K_EXPERTS, tm), lambda i: (0, i)),
                   pl.BlockSpec((TOP_K_EXPERTS, tm), lambda i: (0, i))],
        out_shape=[jax.ShapeDtypeStruct((TOP_K_EXPERTS, t), I32),
                   jax.ShapeDtypeStruct((TOP_K_EXPERTS, t), F32)],
        compiler_params=_cparams(("parallel",)),
        name="router",
    )(h, w_router_t)


def _moe(h, w_router, w1, w3, w2, e_base, ln_g, ln_b):
    t, d = h.shape
    tm = MOE_ROWS
    top_idx, gates = _router(h, w_router.T)
    n_assign = t * TOP_K_EXPERTS
    expert = top_idx.reshape(-1)
    onehot = (expert[:, None] == jnp.arange(N_EXPERTS)[None, :]).astype(I32)
    counts = jnp.sum(onehot, axis=0)
    padded = (counts + tm - 1) // tm * tm
    pad_end = jnp.cumsum(padded)
    dest = jnp.sum((jnp.cumsum(onehot, axis=0) - onehot + (pad_end - padded)[None, :]) * onehot, axis=1)
    n_blk = -(-(n_assign + N_EXPERTS * (tm - 1)) // tm)
    n_rows = n_blk * tm
    token = jnp.arange(n_assign, dtype=I32) % t
    row_token = jnp.zeros((n_rows,), I32).at[dest].set(token)
    blk_expert = jnp.minimum(
        jnp.sum((jnp.arange(n_blk)[:, None] * tm >= pad_end[None, :]).astype(I32), axis=1), N_EXPERTS - 1)
    n_used = (pad_end[-1:] // tm).astype(I32)
    ys = _ffn(h, w1, w3, w2, blk_expert + e_base, n_used, tm, row_token=row_token)
    return _moe_combine_ln(h, ys, dest.astype(I32), gates.T, ln_g, ln_b)


def _rope_tables(lp, n_heads, head_dim, off, rot_dim, scale=1.0):
    r = rot_dim // 2
    pos = jnp.arange(lp)
    inv = ROPE_THETA ** (-jnp.arange(0, rot_dim, 2, dtype=F32) / rot_dim)
    ang = pos.astype(F32)[:, None] * inv[None, :]
    cos, sin = jnp.cos(ang), jnp.sin(ang)
    c = jnp.ones((lp, head_dim), F32).at[:, off:off + r].set(cos).at[:, off + r:off + 2 * r].set(cos)
    sa = jnp.zeros((lp, head_dim), F32).at[:, off:off + r].set(-sin)
    sb = jnp.zeros((lp, head_dim), F32).at[:, off + r:off + 2 * r].set(sin)
    return tuple(jnp.tile(x * scale, (1, n_heads)) for x in (c, sa, sb)) + (r,)


def _pad_cols(w, n):
    return jnp.pad(w, ((0, 0), (0, n - w.shape[1])))


def kernel(x, meta, ln_in_g, ln_in_b, w_in, b_gate, mla_q_norm, mla_kv_norm, w_uq, w_ukv, ssm_a_re, ssm_a_im, ssm_log_dt, ssm_b_re, ssm_b_im, ssm_c_re, ssm_c_im, ssm_d, w_glu, w_branch_a, w_branch_b, w_branch_c, w_o, ln1_g, ln1_b, ffn_w1, ffn_w3, ffn_w2, w_router, moe_w1, moe_w3, moe_w2, ln2_g, ln2_b):
    bsz, seq, _ = x.shape
    n_tok = seq + N_META
    lp = -(-n_tok // BLOCK) * BLOCK
    t = bsz * lp
    n_sel = min(TOPK_MAX, seq // 4)
    meta_b = jnp.broadcast_to(meta[None].astype(x.dtype), (bsz, N_META, D_MODEL))
    pad = jnp.zeros((bsz, lp - n_tok, D_MODEL), x.dtype)
    h, hb = _layernorm(jnp.concatenate([meta_b, x, pad], axis=1).reshape(t, D_MODEL), ln_in_g, ln_in_b)

    a_scale = A_HEAD_DIM ** -0.5 * LOG2E
    rope_q = _rope_tables(lp, 1, A_HEAD_DIM, 0, A_ROT, a_scale)
    rope_k = _rope_tables(lp, 1, A_HEAD_DIM, 0, A_ROT)
    iq_tab = _rope_tables(lp, IDX_HEADS, IDX_DIM, 0, IDX_ROT)
    ik_tab = _rope_tables(lp, 1, LANES, 0, IDX_ROT)
    w_scale = (jnp.zeros((LANES,), F32).at[:IDX_DIM].set(1.0)
               .at[IDX_DIM:IDX_DIM + IDX_HEADS].set((IDX_HEADS * IDX_DIM) ** -0.5))
    rope_i = tuple(jnp.concatenate([a, b * w_scale[None, :], jnp.zeros((lp, LANES), F32)], axis=1)
                   for a, b in zip(iq_tab[:3], ik_tab[:3])) + (IDX_ROT // 2,)
    n_idx = IDX_HEADS * IDX_DIM + 2 * LANES
    rope_kr = _rope_tables(lp, 1, LANES, 0, MLA_ROPE)
    m_scale = (MLA_NOPE + MLA_ROPE) ** -0.5 * LOG2E
    rope_mq = _rope_tables(lp, 1, 2 * LANES, MLA_NOPE, MLA_ROPE, m_scale)

    offs = [0]
    for s in IN_SIZES:
        offs.append(offs[-1] + s)
    o_aq, o_ak, o_av, o_iq, o_ik, o_iw, o_dq, o_dkv, o_kr, o_su, o_gl, o_end = offs

    w_o_bf16 = w_o.astype(BF16)

    for layer in range(DEPTH):
        w_md = _pad_cols(w_in[layer, :, o_dq:o_su], MLA_Q_LORA + MLA_KV_LORA + LANES)
        w_su = w_in[layer, :, o_su:o_gl]

        q_a = _proj(hb, w_in, w_cols=(layer, o_aq, o_ak - o_aq), out_dtype=BF16, lp=lp, rope=rope_q,
                    name="proj_aq")
        k_a = _proj(hb, w_in, w_cols=(layer, o_ak, o_av - o_ak), out_dtype=BF16, lp=lp, rope=rope_k,
                    name="proj_ak")
        v_a = _proj(hb, w_in, w_cols=(layer, o_av, o_iq - o_av), out_dtype=BF16, lp=lp, name="proj_av")
        idx = _proj(h, w_in, w_cols=(layer, o_iq, n_idx), out_dtype=F32, lp=lp, rope=rope_i, hi_prec=True,
                    tm_prefs=(384, 128), tn_prefs=(n_idx,), name="proj_idx")
        bias = _indexer(idx, bsz, lp, n_sel)
        out_a = _dsa_attention(q_a, k_a, v_a, bias, bsz, lp)

        dqkv, kr = _mla_down(hb, w_md, rope_kr, lp)
        wq = w_uq[layer].reshape(MLA_Q_LORA, MLA_HEADS, MLA_NOPE + MLA_ROPE)
        wq = jnp.pad(wq, ((0, 0), (0, 0), (0, 2 * LANES - MLA_NOPE - MLA_ROPE))).reshape(MLA_Q_LORA, -1)
        q_m = _proj(dqkv, wq, out_dtype=BF16, lp=lp, x_col=0, kdim=MLA_Q_LORA, rope=rope_mq,
                    rms_gain=mla_q_norm[layer], name="proj_mq")
        kv_m = _proj(dqkv, w_ukv, w_cols=(layer, 0, w_ukv.shape[2]), out_dtype=BF16, lp=lp, x_col=1,
                     kdim=MLA_KV_LORA, rms_gain=mla_kv_norm[layer], name="proj_mkv")
        out_b = _mla_attention(q_m, kv_m, kr, bsz, lp)

        su = _proj(hb, w_su, out_dtype=F32, lp=lp, name="proj_su")
        s5w = _s5_weights(ssm_a_re[layer], ssm_a_im[layer], ssm_log_dt[layer], ssm_b_re[layer],
                          ssm_b_im[layer], ssm_c_re[layer], ssm_c_im[layer], ssm_d[layer])
        y = _s5(su, s5w, bsz, lp)
        out_c = _glu(y, w_glu, layer, lp)

        w_gates = [w_in[layer, :, o_gl + k * D_MODEL:o_gl + (k + 1) * D_MODEL] for k in range(N_BRANCH)]
        merged = _merge(hb, out_a, out_b, out_c, w_gates, b_gate[layer], w_branch_a, w_branch_b,
                        w_branch_c, layer, lp)
        h, hb = _proj_ln(merged, w_o_bf16, layer, h, ln1_g[layer], ln1_b[layer])

        i = layer // 2
        if layer % 2 == 0:
            tm = _pick(t, (1056, 384, 128))
            nblk = t // tm
            f = _ffn(hb, ffn_w1, ffn_w3, ffn_w2, jnp.full((nblk,), i, I32), jnp.full((1,), nblk, I32), tm)
            h, hb = _layernorm(h, ln2_g[layer], ln2_b[layer], res=f)
        else:
            n_all = moe_w1.shape[0] * N_EXPERTS
            h, hb = _moe(h, w_router[i], moe_w1.reshape((n_all,) + moe_w1.shape[2:]),
                         moe_w3.reshape((n_all,) + moe_w3.shape[2:]),
                         moe_w2.reshape((n_all,) + moe_w2.shape[2:]), i * N_EXPERTS,
                         ln2_g[layer], ln2_b[layer])

    return h.reshape(bsz, lp, D_MODEL)[:, N_META:N_META + seq]
```

```python
import functools
import math

import jax
import jax.numpy as jnp
from jax import lax
from jax.experimental import pallas as pl
from jax.experimental.pallas import tpu as pltpu

F32 = jnp.float32
BF16 = jnp.bfloat16
I32 = jnp.int32

D_MODEL = 2048
DEPTH = 4
N_META = 16
BLOCK = 128
ROPE_THETA = 500000.0
LN_EPS = 1e-5
RMS_EPS = 1e-6
ALPHA = (2 * DEPTH) ** 0.25

A_HEADS = 8
A_KV_HEADS = 2
A_HEAD_DIM = 128
A_ROT = A_HEAD_DIM // 4
IDX_HEADS = 8
IDX_DIM = 64
IDX_ROT = IDX_DIM // 4
TOPK_MAX = 256

MLA_HEADS = 8
MLA_Q_LORA = 512
MLA_KV_LORA = 512
MLA_NOPE = 128
MLA_ROPE = 64
MLA_V = 128

SSM_WIDTH = 1024
SSM_GROUP = 16
SSM_GROUPS = SSM_WIDTH // SSM_GROUP
SSM_STATE = 64
SSM_CHUNK = 16

N_BRANCH = 3
A_WIDTH = A_HEADS * A_HEAD_DIM
MLA_WIDTH = MLA_HEADS * MLA_V
IN_SIZES = (A_HEADS * A_HEAD_DIM, A_KV_HEADS * A_HEAD_DIM, A_KV_HEADS * A_HEAD_DIM,
            IDX_HEADS * IDX_DIM, IDX_DIM, IDX_HEADS,
            MLA_Q_LORA, MLA_KV_LORA, MLA_ROPE,
            SSM_WIDTH, N_BRANCH * D_MODEL)

D_FF = 5632
N_EXPERTS = 8
TOP_K_EXPERTS = 2

LANES = 128
SUBLANES = 8
VMEM_LIMIT = 56 * 1024 * 1024
INT_MIN = -2 ** 31
LOG2E = math.log2(math.e)
GATHER_UNROLL = 8
MOE_ROWS = 704


def _pick(n, prefs):
    for p in prefs:
        if n % p == 0:
            return p
    raise ValueError(f"no tile in {prefs} divides {n}")


def _cparams(sem):
    return pltpu.CompilerParams(dimension_semantics=sem, vmem_limit_bytes=VMEM_LIMIT)


def _dot(a, b):
    return jnp.dot(a, b, preferred_element_type=F32)


def _dot_nt(a, b):
    return lax.dot_general(a, b, (((1,), (1,)), ((), ())), preferred_element_type=F32)


def _split_bf16(x):
    hi = x.astype(BF16)
    lo = (x - hi.astype(F32)).astype(BF16)
    return hi, lo


def _ln_math(x, g, b):
    mu = jnp.mean(x, axis=-1, keepdims=True)
    xc = x - mu
    var = jnp.mean(xc * xc, axis=-1, keepdims=True)
    return xc * lax.rsqrt(var + LN_EPS) * g + b


def _ln_kernel(x_ref, g_ref, b_ref, o_ref, ob_ref):
    y = _ln_math(x_ref[...], g_ref[...], b_ref[...])
    o_ref[...] = y
    ob_ref[...] = y.astype(BF16)


def _ln_res_kernel(h_ref, r_ref, g_ref, b_ref, o_ref, ob_ref):
    y = _ln_math(ALPHA * h_ref[...] + r_ref[...], g_ref[...], b_ref[...])
    o_ref[...] = y
    ob_ref[...] = y.astype(BF16)


def _moe_ln_kernel(pos_ref, h_ref, ys_ref, gt_ref, g_ref, b_ref, o_ref, ob_ref, y_buf, sem):
    i = pl.program_id(0)
    tm, t = h_ref.shape[0], pos_ref.shape[0] // TOP_K_EXPERTS
    slot = i % 2

    def row_copy(buf, k, src_row, r):
        return pltpu.make_async_copy(ys_ref.at[pl.ds(src_row, 1), :],
                                     y_buf.at[buf, k, pl.ds(r, 1), :], sem.at[buf])

    def start_block(blk, buf):
        def body(r8, carry):
            for k in range(TOP_K_EXPERTS):
                for u in range(GATHER_UNROLL):
                    r = r8 * GATHER_UNROLL + u
                    row_copy(buf, k, pos_ref[k * t + blk * tm + r], r).start()
            return carry
        lax.fori_loop(0, tm // GATHER_UNROLL, body, 0)

    def wait_block(buf):
        def body(r8, carry):
            for k in range(TOP_K_EXPERTS):
                for u in range(GATHER_UNROLL):
                    row_copy(buf, k, 0, r8 * GATHER_UNROLL + u).wait()
            return carry
        lax.fori_loop(0, tm // GATHER_UNROLL, body, 0)

    @pl.when(i == 0)
    def _():
        start_block(0, 0)

    wait_block(slot)

    @pl.when(i + 1 < pl.num_programs(0))
    def _():
        start_block(i + 1, 1 - slot)

    gt = gt_ref[...]
    f = y_buf[slot, 0] * gt[:, 0:1] + y_buf[slot, 1] * gt[:, 1:2]
    y = _ln_math(ALPHA * h_ref[...] + f, g_ref[...], b_ref[...])
    o_ref[...] = y
    ob_ref[...] = y.astype(BF16)


def _layernorm(x, g, b, res=None):
    t, d = x.shape
    tm = _pick(t, (384, 256, 128))
    row = pl.BlockSpec((tm, d), lambda i: (i, 0))
    vec = pl.BlockSpec((1, d), lambda i: (0, 0))
    ins = [x] if res is None else [x, res]
    return pl.pallas_call(
        _ln_kernel if res is None else _ln_res_kernel,
        grid=(t // tm,),
        in_specs=[row] * len(ins) + [vec, vec],
        out_specs=[row, row],
        out_shape=[jax.ShapeDtypeStruct((t, d), F32), jax.ShapeDtypeStruct((t, d), BF16)],
        compiler_params=_cparams(("parallel",)),
        name="layernorm",
    )(*ins, g.reshape(1, d), b.reshape(1, d))


def _proj_ln_kernel(x_ref, w_ref, h_ref, g_ref, b_ref, o_ref, ob_ref):
    y = _ln_math(ALPHA * h_ref[...] + _dot(x_ref[...], w_ref[...]), g_ref[...], b_ref[...])
    o_ref[...] = y
    ob_ref[...] = y.astype(BF16)


def _proj_ln(x, w, layer, h, g, b):
    t, k = x.shape
    d = w.shape[2]
    tm = _pick(t, (384, 256, 128))
    row = pl.BlockSpec((tm, d), lambda i: (i, 0))
    vec = pl.BlockSpec((1, d), lambda i: (0, 0))
    return pl.pallas_call(
        _proj_ln_kernel,
        grid=(t // tm,),
        in_specs=[pl.BlockSpec((tm, k), lambda i: (i, 0)),
                  pl.BlockSpec((None, k, d), lambda i: (layer, 0, 0)), row, vec, vec],
        out_specs=[row, row],
        out_shape=[jax.ShapeDtypeStruct((t, d), F32), jax.ShapeDtypeStruct((t, d), BF16)],
        compiler_params=_cparams(("parallel",)),
        name="proj_o_ln",
    )(x, w, h, g.reshape(1, d), b.reshape(1, d))


def _moe_combine_ln(h, ys, pos, gates_t, g, b):
    t, d = h.shape
    tm = _pick(t, (384, 256, 128))
    row = pl.BlockSpec((tm, d), lambda i, pos_ref: (i, 0))
    vec = pl.BlockSpec((1, d), lambda i, pos_ref: (0, 0))
    grid_spec = pltpu.PrefetchScalarGridSpec(
        num_scalar_prefetch=1,
        grid=(t // tm,),
        in_specs=[row, pl.BlockSpec(memory_space=pl.ANY),
                  pl.BlockSpec((tm, TOP_K_EXPERTS), lambda i, pos_ref: (i, 0)), vec, vec],
        out_specs=[row, row],
        scratch_shapes=[pltpu.VMEM((2, TOP_K_EXPERTS, tm, d), ys.dtype), pltpu.SemaphoreType.DMA((2,))],
    )
    return pl.pallas_call(
        _moe_ln_kernel,
        grid_spec=grid_spec,
        out_shape=[jax.ShapeDtypeStruct((t, d), F32), jax.ShapeDtypeStruct((t, d), BF16)],
        compiler_params=_cparams(("arbitrary",)),
        name="moe_combine_ln",
    )(pos, h, ys, gates_t, g.reshape(1, d), b.reshape(1, d))


def _rope_chunk(a, c, sa, sb, r):
    return a * c + pltpu.roll(a, LANES - r, 1) * sa + pltpu.roll(a, r, 1) * sb


def _proj_kernel(*refs, rope_r, has_bias, sigmoid, hi_prec, rms):
    it = iter(refs)
    x_ref = next(it)
    w_ref = next(it)
    n_ref = next(it) if rms else None
    tabs = (next(it), next(it), next(it)) if rope_r else None
    b_ref = next(it) if has_bias else None
    o_ref = next(it)
    x = x_ref[...]
    if rms:
        ms = jnp.mean(x * x, axis=-1, keepdims=True)
        x = x * lax.rsqrt(ms + RMS_EPS) * n_ref[...]
    if hi_prec:
        xh, xl = _split_bf16(x)
        wh, wl = _split_bf16(w_ref[...])
        acc = _dot(xh, wh) + _dot(xl, wh) + _dot(xh, wl)
    else:
        acc = _dot(x.astype(BF16), w_ref[...].astype(BF16))
    if has_bias:
        acc = acc + b_ref[...]
    if sigmoid:
        acc = jax.nn.sigmoid(acc)
    if rope_r:
        c_ref, sa_ref, sb_ref = tabs
        pw = c_ref.shape[1]
        for c in range(acc.shape[1] // LANES):
            sl = slice(c * LANES, (c + 1) * LANES)
            ts = slice((c * LANES) % pw, (c * LANES) % pw + LANES)
            o_ref[:, sl] = _rope_chunk(acc[:, sl], c_ref[:, ts], sa_ref[:, ts], sb_ref[:, ts],
                                       rope_r).astype(o_ref.dtype)
    else:
        o_ref[...] = acc.astype(o_ref.dtype)


def _proj(x, w, *, out_dtype, lp, x_col=0, kdim=None, w_cols=None, rope=None, bias=None, sigmoid=False,
          hi_prec=False, rms_gain=None, tm_prefs=(1056, 768, 384, 128), tn_prefs=(512, 384, 256, 128),
          name="proj"):
    t = x.shape[0]
    kdim = x.shape[1] if kdim is None else kdim
    tm = _pick(lp, tm_prefs)
    nrow = lp // tm
    if w_cols is None:
        n = w.shape[1]
        tn = _pick(n, tn_prefs)
        w_spec = pl.BlockSpec((kdim, tn), lambda i, j: (0, j))
    else:
        layer, col0, n = w_cols
        tn = _pick(math.gcd(n, col0) if col0 else n, tn_prefs)
        cb0 = col0 // tn
        w_spec = pl.BlockSpec((None, kdim, tn), lambda i, j: (layer, 0, cb0 + j))
    in_specs = [pl.BlockSpec((tm, kdim), lambda i, j: (i, x_col)), w_spec]
    ins = [x, w]
    if rms_gain is not None:
        in_specs.append(pl.BlockSpec((1, kdim), lambda i, j: (0, 0)))
        ins.append(rms_gain.reshape(1, kdim))
    if rope is not None:
        pw = rope[0].shape[1]
        if pw == n:
            tab = pl.BlockSpec((tm, tn), lambda i, j: (i % nrow, j))
        else:
            assert tn % pw == 0
            tab = pl.BlockSpec((tm, pw), lambda i, j: (i % nrow, 0))
        in_specs += [tab, tab, tab]
        ins += list(rope[:3])
    if bias is not None:
        in_specs.append(pl.BlockSpec((1, tn), lambda i, j: (0, j)))
        ins.append(bias.reshape(1, n))
    kern = functools.partial(_proj_kernel, rope_r=rope[3] if rope is not None else 0,
                             has_bias=bias is not None, sigmoid=sigmoid, hi_prec=hi_prec,
                             rms=rms_gain is not None)
    return pl.pallas_call(
        kern,
        grid=(t // tm, n // tn),
        in_specs=in_specs,
        out_specs=pl.BlockSpec((tm, tn), lambda i, j: (i, j)),
        out_shape=jax.ShapeDtypeStruct((t, n), out_dtype),
        compiler_params=_cparams(("parallel", "arbitrary")),
        name=name,
    )(*ins)


def _mla_down_kernel(x_ref, w_ref, c_ref, sa_ref, sb_ref, o_ref, kr_ref):
    acc = _dot(x_ref[...], w_ref[...].astype(BF16))
    nq = o_ref.shape[1]
    o_ref[...] = acc[:, :nq]
    kr_ref[...] = _rope_chunk(acc[:, nq:], c_ref[...], sa_ref[...], sb_ref[...],
                              MLA_ROPE // 2).astype(BF16)


def _mla_down(hb, w, rope, lp):
    t, d = hb.shape
    n = w.shape[1]
    nq = MLA_Q_LORA + MLA_KV_LORA
    tm = _pick(lp, (704, 384, 128))
    nrow = lp // tm
    tab = pl.BlockSpec((tm, LANES), lambda i: (i % nrow, 0))
    return pl.pallas_call(
        _mla_down_kernel,
        grid=(t // tm,),
        in_specs=[pl.BlockSpec((tm, d), lambda i: (i, 0)), pl.BlockSpec((d, n), lambda i: (0, 0)),
                  tab, tab, tab],
        out_specs=[pl.BlockSpec((tm, nq), lambda i: (i, 0)), pl.BlockSpec((tm, LANES), lambda i: (i, 0))],
        out_shape=[jax.ShapeDtypeStruct((t, nq), F32), jax.ShapeDtypeStruct((t, LANES), BF16)],
        compiler_params=_cparams(("parallel",)),
        name="mla_down",
    )(hb, w, *rope[:3])


def _glu_kernel(y_ref, w_ref, o_ref):
    n = o_ref.shape[1]
    y = y_ref[...].astype(BF16)
    ga = _dot(y, w_ref[:, :n].astype(BF16))
    gb = _dot(y, w_ref[:, n:].astype(BF16))
    o_ref[...] = (ga * jax.nn.sigmoid(gb)).astype(o_ref.dtype)


def _glu(y, w_glu, layer, lp):
    t, k = y.shape
    n = w_glu.shape[2] // 2
    tm = _pick(lp, (704, 384, 128))
    return pl.pallas_call(
        _glu_kernel,
        grid=(t // tm,),
        in_specs=[pl.BlockSpec((tm, k), lambda i: (i, 0)),
                  pl.BlockSpec((None, k, 2 * n), lambda i: (layer, 0, 0))],
        out_specs=pl.BlockSpec((tm, n), lambda i: (i, 0)),
        out_shape=jax.ShapeDtypeStruct((t, n), BF16),
        compiler_params=_cparams(("parallel",)),
        name="glu",
    )(y, w_glu)


def _merge_kernel(h_ref, a_ref, b_ref, c_ref, wg0_ref, wg1_ref, wg2_ref, bg_ref,
                  wa_ref, wb_ref, wc_ref, o_ref):
    h = h_ref[...]
    m = None
    for k, (wg_ref, x_ref, w_ref) in enumerate(((wg0_ref, a_ref, wa_ref), (wg1_ref, b_ref, wb_ref),
                                                (wg2_ref, c_ref, wc_ref))):
        gate = jax.nn.sigmoid(_dot(h, wg_ref[...].astype(BF16)) + bg_ref[k])
        term = gate * _dot(x_ref[...], w_ref[...].astype(BF16))
        m = term if m is None else m + term
    o_ref[...] = m.astype(o_ref.dtype)


def _merge(hb, out_a, out_b, out_c, w_gates, b_gate, wb_a, wb_b, wb_c, layer, lp):
    t = out_a.shape[0]
    n = wb_a.shape[2]
    tm = _pick(lp, (1056, 384, 128))
    tn = _pick(n, (256, 128))

    def act(arr):
        return pl.BlockSpec((tm, arr.shape[1]), lambda i, j: (i, 0))

    def wgt(arr):
        return pl.BlockSpec((None, arr.shape[1], tn), lambda i, j: (layer, 0, j))

    gate_w = pl.BlockSpec((hb.shape[1], tn), lambda i, j: (0, j))
    return pl.pallas_call(
        _merge_kernel,
        grid=(t // tm, n // tn),
        in_specs=[act(hb), act(out_a), act(out_b), act(out_c), gate_w, gate_w, gate_w,
                  pl.BlockSpec((N_BRANCH, 1, tn), lambda i, j: (0, 0, j)),
                  wgt(wb_a), wgt(wb_b), wgt(wb_c)],
        out_specs=pl.BlockSpec((tm, tn), lambda i, j: (i, j)),
        out_shape=jax.ShapeDtypeStruct((t, n), BF16),
        compiler_params=_cparams(("parallel", "arbitrary")),
        name="merge",
    )(hb, out_a, out_b, out_c, *w_gates, b_gate.reshape(N_BRANCH, 1, n), wb_a, wb_b, wb_c)


def _softmax_update(s, v, m_ref, l_ref, acc_ref):
    tk = s.shape[1]
    m_prev = m_ref[...]
    m_new = jnp.maximum(m_prev, jnp.max(s, axis=1, keepdims=True))
    alpha = jnp.exp2(m_prev - m_new)
    p = jnp.exp2(s - jnp.concatenate([m_new] * (tk // LANES), axis=1))
    l_ref[...] = alpha * l_ref[...] + jnp.sum(p, axis=1, keepdims=True)
    acc_ref[...] = alpha * acc_ref[...] + _dot(p.astype(BF16), v)
    m_ref[...] = m_new


def _mla_attn_kernel(q_ref, kv_ref, kr_ref, o_ref, s_ref, m_ref, l_ref, acc_ref):
    qi = pl.program_id(2)
    tq = q_ref.shape[0]
    m_ref[...] = jnp.full(m_ref.shape, -jnp.inf, F32)
    l_ref[...] = jnp.zeros(l_ref.shape, F32)
    acc_ref[...] = jnp.zeros(acc_ref.shape, F32)
    q = q_ref[...]

    def rows_of(kt):
        return pl.ds(pl.multiple_of(kt * tq, tq), tq)

    def produce(buf, kt):
        rows = rows_of(jnp.minimum(kt, qi))
        s_ref[buf] = _dot_nt(q, jnp.concatenate([kv_ref[rows, :MLA_NOPE], kr_ref[rows, :]], axis=1))

    def consume(s, kt):
        _softmax_update(s, kv_ref[rows_of(kt), MLA_NOPE:], m_ref, l_ref, acc_ref)

    produce(0, 0)
    produce(1, 1)

    def body(kp, carry):
        s0 = s_ref[0]
        s1 = s_ref[1]
        produce(0, 2 * kp + 2)
        produce(1, 2 * kp + 3)
        consume(s0, 2 * kp)
        consume(s1, 2 * kp + 1)
        return carry

    lax.fori_loop(0, qi // 2, body, 0)

    @pl.when(qi % 2 == 1)
    def _():
        consume(s_ref[0], qi - 1)

    row = lax.broadcasted_iota(I32, (tq, tq), 0)
    col = lax.broadcasted_iota(I32, (tq, tq), 1)
    consume(jnp.where(col <= row, s_ref[qi % 2], -jnp.inf), qi)
    o_ref[...] = (acc_ref[...] / l_ref[...]).astype(o_ref.dtype)


def _mla_attention(q, kv, kr, bsz, lp):
    t = q.shape[0]
    tq = _pick(lp, (384, 128))
    nq = lp // tq
    q3 = q.reshape(bsz, lp, q.shape[1])
    kv3 = kv.reshape(bsz, lp, kv.shape[1])
    kr3 = kr.reshape(bsz, lp, LANES)
    out = pl.pallas_call(
        _mla_attn_kernel,
        grid=(bsz, MLA_HEADS, nq),
        in_specs=[
            pl.BlockSpec((None, tq, 2 * LANES), lambda b, h, i: (b, i, h)),
            pl.BlockSpec((None, lp, MLA_NOPE + MLA_V), lambda b, h, i: (b, 0, h)),
            pl.BlockSpec((None, lp, LANES), lambda b, h, i: (b, 0, 0)),
        ],
        out_specs=pl.BlockSpec((None, tq, LANES), lambda b, h, i: (b, i, h)),
        out_shape=jax.ShapeDtypeStruct((bsz, lp, MLA_WIDTH), BF16),
        scratch_shapes=[pltpu.VMEM((2, tq, tq), F32),
                        pltpu.VMEM((tq, LANES), F32), pltpu.VMEM((tq, LANES), F32),
                        pltpu.VMEM((tq, MLA_V), F32)],
        compiler_params=_cparams(("parallel", "parallel", "arbitrary")),
        name="mla_attention",
    )(q3, kv3, kr3)
    return out.reshape(t, MLA_WIDTH)


def _dsa_attn_kernel(q_ref, k_ref, v_ref, bias_ref, o_ref, qg_ref, s_ref, m_ref, l_ref, acc_ref, *, tk):
    qi = pl.program_id(1)
    rep = A_HEADS // A_KV_HEADS
    nkt = tk // BLOCK
    m_ref[...] = jnp.full(m_ref.shape, -jnp.inf, F32)
    l_ref[...] = jnp.zeros(l_ref.shape, F32)
    acc_ref[...] = jnp.zeros(acc_ref.shape, F32)
    for g in range(A_KV_HEADS):
        for r in range(rep):
            h = g * rep + r
            qg_ref[g, r * BLOCK:(r + 1) * BLOCK, :] = q_ref[:, h * A_HEAD_DIM:(h + 1) * A_HEAD_DIM]

    def rows_of(kt):
        return pl.ds(pl.multiple_of(kt * tk, tk), tk)

    def scores(kt):
        rows = rows_of(kt)
        for g in range(A_KV_HEADS):
            s_ref[g] = _dot_nt(qg_ref[g], k_ref[rows, g * A_HEAD_DIM:(g + 1) * A_HEAD_DIM])

    def consume(kt, s):
        bias = jnp.concatenate([bias_ref[kt * nkt + j] for j in range(nkt)], axis=1)
        bias = jnp.concatenate([bias.astype(F32)] * rep, axis=0)
        for g in range(A_KV_HEADS):
            _softmax_update(s[g] + bias, v_ref[rows_of(kt), g * A_HEAD_DIM:(g + 1) * A_HEAD_DIM],
                            m_ref.at[g], l_ref.at[g], acc_ref.at[g])

    scores(0)

    def body(kt, carry):
        s = [s_ref[g] for g in range(A_KV_HEADS)]
        scores(kt + 1)
        consume(kt, s)
        return carry

    last = (qi * BLOCK) // tk
    lax.fori_loop(0, last, body, 0)
    consume(last, [s_ref[g] for g in range(A_KV_HEADS)])
    for g in range(A_KV_HEADS):
        o = acc_ref[g] / l_ref[g]
        for r in range(rep):
            h = g * rep + r
            o_ref[:, h * A_HEAD_DIM:(h + 1) * A_HEAD_DIM] = (
                o[r * BLOCK:(r + 1) * BLOCK]).astype(o_ref.dtype)


def _dsa_attention(q, k, v, bias, bsz, lp):
    t = q.shape[0]
    nb = lp // BLOCK
    tk = _pick(lp, (384, 128))
    kvw = A_KV_HEADS * A_HEAD_DIM
    q3 = q.reshape(bsz, lp, A_WIDTH)
    k3 = k.reshape(bsz, lp, kvw)
    v3 = v.reshape(bsz, lp, kvw)
    rows = (A_HEADS // A_KV_HEADS) * BLOCK
    out = pl.pallas_call(
        functools.partial(_dsa_attn_kernel, tk=tk),
        grid=(bsz, nb),
        in_specs=[
            pl.BlockSpec((None, BLOCK, A_WIDTH), lambda b, i: (b, i, 0)),
            pl.BlockSpec((None, lp, kvw), lambda b, i: (b, 0, 0)),
            pl.BlockSpec((None, lp, kvw), lambda b, i: (b, 0, 0)),
            pl.BlockSpec((None, None, nb, BLOCK, BLOCK), lambda b, i: (b, i, 0, 0, 0)),
        ],
        out_specs=pl.BlockSpec((None, BLOCK, A_WIDTH), lambda b, i: (b, i, 0)),
        out_shape=jax.ShapeDtypeStruct((bsz, lp, A_WIDTH), BF16),
        scratch_shapes=[pltpu.VMEM((A_KV_HEADS, rows, A_HEAD_DIM), BF16),
                        pltpu.VMEM((A_KV_HEADS, rows, tk), F32),
                        pltpu.VMEM((A_KV_HEADS, rows, LANES), F32),
                        pltpu.VMEM((A_KV_HEADS, rows, LANES), F32),
                        pltpu.VMEM((A_KV_HEADS, rows, A_HEAD_DIM), F32)],
        compiler_params=_cparams(("parallel", "arbitrary")),
        name="dsa_attention",
    )(q3, k3, v3, bias)
    return out.reshape(t, A_WIDTH)


def _indexer_kernel(iq_ref, kw_ref, o_ref, kcat_ref, qcat_ref, wb_ref, key_ref, *, n_sel):
    qb = pl.program_id(0)
    bsz, nb = o_ref.shape[0], o_ref.shape[1]
    batch = range(bsz)
    half = IDX_DIM
    lane = lax.broadcasted_iota(I32, (BLOCK, LANES), 1)
    row = lax.broadcasted_iota(I32, (BLOCK, LANES), 0)

    @pl.when(qb == 0)
    def _():
        for b in batch:
            kf = kw_ref[b]
            klane = lax.broadcasted_iota(I32, kf.shape, 1)
            kz = jnp.where(klane < half, kf, 0.0)
            hi = kz.astype(BF16).astype(F32)
            lo = kz - hi
            lp = kf.shape[0]
            kcat_ref[b, 0:lp, :] = jnp.concatenate(
                [(hi + pltpu.roll(hi, half, 1)).astype(BF16), lo.astype(BF16)], axis=1)
            kcat_ref[b, lp:lp + BLOCK, :] = jnp.zeros((BLOCK, 2 * LANES), BF16)

    for b in batch:
        wq = kw_ref[b, pl.ds(pl.multiple_of(qb * BLOCK, BLOCK), BLOCK), :]
        for h in range(IDX_HEADS):
            chunk = iq_ref[b, :, (h // 2) * LANES:(h // 2 + 1) * LANES]
            if h % 2 == 0:
                a = jnp.where(lane < half, chunk, 0.0)
            else:
                a = pltpu.roll(jnp.where(lane >= half, chunk, 0.0), half, 1)
            hi = a.astype(BF16).astype(F32)
            lo = a - hi
            qcat_ref[b, h * BLOCK:(h + 1) * BLOCK, :] = jnp.concatenate(
                [(hi + pltpu.roll(lo, half, 1)).astype(BF16), hi.astype(BF16)], axis=1)
            wb_ref[b, h] = jnp.broadcast_to(wq[:, half + h:half + h + 1], (BLOCK, LANES))

    qpos = qb * BLOCK + row

    def sortable(x):
        b = pltpu.bitcast(x, I32)
        return b ^ ((b >> 31) & 0x7FFFFFFF)

    n_pair = (qb + 2) // 2

    def score_pair(kp, carry):
        for b in batch:
            kblk = kcat_ref[b, pl.ds(pl.multiple_of(kp * 2 * BLOCK, 2 * BLOCK), 2 * BLOCK), :]
            s = _dot_nt(qcat_ref[b], kblk)
            for j in range(2):
                sc = jnp.zeros((BLOCK, LANES), F32)
                for h in range(IDX_HEADS):
                    sc = sc + (jnp.maximum(s[h * BLOCK:(h + 1) * BLOCK, j * LANES:(j + 1) * LANES], 0.0)
                               * wb_ref[b, h])
                kpos = (kp * 2 + j) * BLOCK + lane
                sc = jnp.where(kpos < N_META, jnp.inf, sc)
                sc = jnp.where(kpos <= qpos, sc, -jnp.inf)
                key_ref[b, kp * 2 + j] = sortable(sc)
        return carry

    lax.fori_loop(0, n_pair, score_pair, 0)

    def count(pred):
        def body(kp, cs):
            return tuple(c + jnp.where(pred(key_ref[b, kp * 2], b), 1.0, 0.0)
                         + jnp.where(pred(key_ref[b, kp * 2 + 1], b), 1.0, 0.0)
                         for b, c in zip(batch, cs))
        cs = lax.fori_loop(0, n_pair, body, tuple(jnp.zeros((BLOCK, LANES), F32) for _ in batch))
        return tuple(jnp.sum(c, axis=1, keepdims=True) for c in cs)

    def bit_step(i, thrs):
        bit = 31 - i
        cands = tuple(jnp.where(bit == 31, thr ^ INT_MIN, thr | (1 << jnp.minimum(bit, 30)))
                      for thr in thrs)
        counts = count(lambda k, b: k >= cands[b])
        return tuple(jnp.where(n >= n_sel, cand, thr) for n, cand, thr in zip(counts, cands, thrs))

    thrs = lax.fori_loop(0, 32, bit_step,
                         tuple(jnp.full((BLOCK, LANES), INT_MIN, I32) for _ in batch))
    n_ge = count(lambda k, b: k >= thrs[b])
    all_ties_fit = functools.reduce(jnp.maximum, [jnp.max(n) for n in n_ge]) <= n_sel

    @pl.when(all_ties_fit)
    def _():
        def emit(kt, carry):
            kpos = kt * BLOCK + lane
            for b in batch:
                o_ref[b, kt] = jnp.where((key_ref[b, kt] >= thrs[b]) & (kpos <= qpos),
                                         0.0, -jnp.inf).astype(o_ref.dtype)
            return carry

        lax.fori_loop(0, qb + 1, emit, 0)

    @pl.when(jnp.logical_not(all_ties_fit))
    def _():
        n_gt = count(lambda k, b: k > thrs[b])
        tri = (lax.broadcasted_iota(I32, (LANES, LANES), 0)
               <= lax.broadcasted_iota(I32, (LANES, LANES), 1)).astype(BF16)

        def emit(kt, takens):
            out = []
            kpos = kt * BLOCK + lane
            for b in batch:
                key = key_ref[b, kt]
                eq = key == thrs[b]
                rank = _dot(jnp.where(eq, 1.0, 0.0).astype(BF16), tri)
                sel = (key > thrs[b]) | (eq & (takens[b] + rank <= n_sel - n_gt[b]))
                o_ref[b, kt] = jnp.where(sel & (kpos <= qpos), 0.0, -jnp.inf).astype(o_ref.dtype)
                out.append(takens[b] + rank[:, LANES - 1:LANES])
            return tuple(out)

        lax.fori_loop(0, qb + 1, emit, tuple(jnp.zeros((BLOCK, 1), F32) for _ in batch))

    def fill(kt, carry):
        for b in batch:
            o_ref[b, kt] = jnp.full((BLOCK, LANES), -jnp.inf, o_ref.dtype)
        return carry

    lax.fori_loop(qb + 1, nb, fill, 0)


def _indexer(idx, bsz, lp, n_sel):
    nb = lp // BLOCK
    idx3 = idx.reshape(bsz, lp, idx.shape[1])
    nq = IDX_HEADS * IDX_DIM
    kw_col = nq // LANES
    return pl.pallas_call(
        functools.partial(_indexer_kernel, n_sel=n_sel),
        grid=(nb,),
        in_specs=[
            pl.BlockSpec((bsz, BLOCK, nq), lambda i: (0, i, 0)),
            pl.BlockSpec((bsz, lp, LANES), lambda i: (0, 0, kw_col)),
        ],
        out_specs=pl.BlockSpec((bsz, None, nb, BLOCK, BLOCK), lambda i: (0, i, 0, 0, 0)),
        out_shape=jax.ShapeDtypeStruct((bsz, nb, nb, BLOCK, BLOCK), BF16),
        scratch_shapes=[pltpu.VMEM((bsz, lp + BLOCK, 2 * LANES), BF16),
                        pltpu.VMEM((bsz, IDX_HEADS * BLOCK, 2 * LANES), BF16),
                        pltpu.VMEM((bsz, IDX_HEADS, BLOCK, LANES), F32),
                        pltpu.VMEM((bsz, nb + 1, BLOCK, LANES), I32)],
        compiler_params=_cparams(("arbitrary",)),
        name="indexer",
    )(idx3, idx3)


def _s5_kernel(x_ref, mw_ref, vre_ref, vim_ref, are_ref, aim_ref, o_ref,
               perm_ref, y_ref, yb_ref, sre_ref, sim_ref, pre_ref, pim_ref, *, bsz):
    tc, c, p = SSM_CHUNK, SSM_GROUP, SSM_STATE
    ng = LANES // c
    nm = tc * c
    rows = x_ref.shape[0] // tc
    per_b = rows // bsz
    width = tc * LANES

    @pl.when(pl.program_id(0) == 0)
    def _():
        src = lax.broadcasted_iota(I32, (width, width), 0)
        dst = lax.broadcasted_iota(I32, (width, width), 1)
        group = (src & (LANES - 1)) >> (c.bit_length() - 1)
        step_in_chunk = src >> (LANES.bit_length() - 1)
        want = group * nm + step_in_chunk * c + (src & (c - 1))
        perm_ref[...] = jnp.where(dst == want, 1.0, 0.0).astype(BF16)

    x_all = jnp.concatenate(
        [x_ref[pl.ds(j, rows, stride=tc), :].astype(BF16) for j in range(tc)], axis=1)
    u_all = _dot(x_all, perm_ref[...]).astype(BF16)
    for g in range(ng):
        r = _dot(u_all[:, g * nm:(g + 1) * nm], mw_ref[g])
        y_ref[:, g * nm:(g + 1) * nm] = r[:, :nm]
        sre_ref[:, g * p:(g + 1) * p] = r[:, nm:nm + p]
        sim_ref[:, g * p:(g + 1) * p] = r[:, nm + p:nm + 2 * p]
    ar = are_ref[...]
    ai = aim_ref[...]

    def step(ti, carry):
        out = []
        for b in range(bsz):
            cr, ci = carry[b]
            rs = pl.ds(pl.multiple_of(b * per_b + ti * SUBLANES, SUBLANES), SUBLANES)
            lr = sre_ref[rs, :]
            li = sim_ref[rs, :]
            before_r, before_i = [], []
            for k in range(SUBLANES):
                before_r.append(cr)
                before_i.append(ci)
                cr, ci = (ar * cr - ai * ci + lr[k:k + 1], ar * ci + ai * cr + li[k:k + 1])
            pre_ref[rs, :] = jnp.concatenate(before_r, axis=0)
            pim_ref[rs, :] = jnp.concatenate(before_i, axis=0)
            out.append((cr, ci))
        return tuple(out)

    zero = jnp.zeros((1, ng * p), F32)
    lax.fori_loop(0, per_b // SUBLANES, step, tuple((zero, zero) for _ in range(bsz)))
    for g in range(ng):
        y = (y_ref[:, g * nm:(g + 1) * nm]
             + _dot(pre_ref[:, g * p:(g + 1) * p].astype(BF16), vre_ref[g])
             + _dot(pim_ref[:, g * p:(g + 1) * p].astype(BF16), vim_ref[g]))
        yb_ref[:, g * nm:(g + 1) * nm] = y.astype(BF16)
    y_all = _dot_nt(yb_ref[...], perm_ref[...])
    for j in range(tc):
        o_ref[pl.ds(j, rows, stride=tc), :] = y_all[:, j * LANES:(j + 1) * LANES]


def _s5_weights(a_re, a_im, log_dt, b_re, b_im, c_re, c_im, d_skip):
    hp = lax.Precision.HIGHEST
    g, p, c, tc = SSM_GROUPS, SSM_STATE, SSM_GROUP, SSM_CHUNK
    dt = jnp.exp(log_dt)[:, None]
    lam_re, lam_im = dt * a_re, dt * a_im
    mag = jnp.exp(lam_re)
    ab_re, ab_im = mag * jnp.cos(lam_im), mag * jnp.sin(lam_im)
    den = a_re * a_re + a_im * a_im
    f_re = ((ab_re - 1.0) * a_re + ab_im * a_im) / den
    f_im = (ab_im * a_re - (ab_re - 1.0) * a_im) / den
    bb_re = f_re[..., None] * b_re - f_im[..., None] * b_im
    bb_im = f_re[..., None] * b_im + f_im[..., None] * b_re
    d = jnp.arange(tc + 1, dtype=F32)[:, None, None]
    pmag = jnp.exp(d * lam_re)
    pw_re, pw_im = pmag * jnp.cos(d * lam_im), pmag * jnp.sin(d * lam_im)
    z_re = pw_re[:tc, :, :, None] * bb_re - pw_im[:tc, :, :, None] * bb_im
    z_im = pw_re[:tc, :, :, None] * bb_im + pw_im[:tc, :, :, None] * bb_re
    kmat = (jnp.einsum('gop,dgpi->gdio', c_re, z_re, precision=hp)
            - jnp.einsum('gop,dgpi->gdio', c_im, z_im, precision=hp))
    ti = jnp.arange(tc)
    lag = ti[None, :] - ti[:, None]
    place = (lag[None] == ti[:, None, None]).astype(F32)
    m = jnp.einsum('dij,gdxy->gixjy', place, kmat, precision=hp).reshape(g, tc * c, tc * c)
    m = m + jnp.eye(tc * c, dtype=F32) * jnp.tile(d_skip.reshape(g, 1, c), (1, tc, 1)).reshape(g, 1, tc * c)
    w_re = jnp.transpose(z_re[::-1], (1, 0, 3, 2)).reshape(g, tc * c, p)
    w_im = jnp.transpose(z_im[::-1], (1, 0, 3, 2)).reshape(g, tc * c, p)
    mw = jnp.concatenate([m, w_re, w_im], axis=2).astype(BF16)
    q_re, q_im = pw_re[1:], pw_im[1:]
    v_re = c_re[None] * q_re[:, :, None, :] - c_im[None] * q_im[:, :, None, :]
    v_im = c_re[None] * q_im[:, :, None, :] + c_im[None] * q_re[:, :, None, :]
    v_re = jnp.transpose(v_re, (1, 3, 0, 2)).reshape(g, p, tc * c).astype(BF16)
    v_im = jnp.transpose(-v_im, (1, 3, 0, 2)).reshape(g, p, tc * c).astype(BF16)
    return mw, v_re, v_im, pw_re[tc].reshape(1, g * p), pw_im[tc].reshape(1, g * p)


def _s5(su, weights, bsz, lp):
    del lp
    mw, v_re, v_im, a_re, a_im = weights
    t = su.shape[0]
    c, tc, p = SSM_GROUP, SSM_CHUNK, SSM_STATE
    ng = LANES // c
    rows = t // tc
    return pl.pallas_call(
        functools.partial(_s5_kernel, bsz=bsz),
        grid=(SSM_WIDTH // LANES,),
        in_specs=[
            pl.BlockSpec((t, LANES), lambda i: (0, i)),
            pl.BlockSpec((ng, tc * c, tc * c + 2 * p), lambda i: (i, 0, 0)),
            pl.BlockSpec((ng, p, tc * c), lambda i: (i, 0, 0)),
            pl.BlockSpec((ng, p, tc * c), lambda i: (i, 0, 0)),
            pl.BlockSpec((1, ng * p), lambda i: (0, i)),
            pl.BlockSpec((1, ng * p), lambda i: (0, i)),
        ],
        out_specs=pl.BlockSpec((t, LANES), lambda i: (0, i)),
        out_shape=jax.ShapeDtypeStruct((t, SSM_WIDTH), F32),
        scratch_shapes=[pltpu.VMEM((tc * LANES, tc * LANES), BF16),
                        pltpu.VMEM((rows, tc * LANES), F32), pltpu.VMEM((rows, tc * LANES), BF16),
                        pltpu.VMEM((rows, ng * p), F32), pltpu.VMEM((rows, ng * p), F32),
                        pltpu.VMEM((rows, ng * p), F32), pltpu.VMEM((rows, ng * p), F32)],
        compiler_params=_cparams(("arbitrary",)),
        name="s5",
    )(su, mw, v_re, v_im, a_re, a_im)


def _swiglu_accumulate(x, w1_ref, w3_ref, w2_ref, o_ref):
    a = _dot(x, w1_ref[0].astype(BF16))
    b = _dot(x, w3_ref[0].astype(BF16))
    act = (a * jax.nn.sigmoid(a) * b).astype(BF16)
    o_ref[...] += _dot(act, w2_ref[0].astype(BF16))


def _ffn_kernel(be_ref, nu_ref, x_ref, w1_ref, w3_ref, w2_ref, o_ref):
    i = pl.program_id(0)
    f = pl.program_id(1)

    @pl.when(f == 0)
    def _():
        o_ref[...] = jnp.zeros(o_ref.shape, F32)

    @pl.when(i < nu_ref[0])
    def _():
        _swiglu_accumulate(x_ref[...].astype(BF16), w1_ref, w3_ref, w2_ref, o_ref)


def _ffn_gather_kernel(be_ref, nu_ref, tok_ref, h_ref, w1_ref, w3_ref, w2_ref, o_ref, x_buf, sem, *, nf):
    i = pl.program_id(0)
    f = pl.program_id(1)
    tm = o_ref.shape[0]
    nu = nu_ref[0]
    slot = i % 2

    def row_copy(buf, tok, r):
        return pltpu.make_async_copy(h_ref.at[pl.ds(tok, 1), :], x_buf.at[buf, pl.ds(r, 1), :],
                                     sem.at[buf])

    def start_block(blk, buf):
        def body(r8, carry):
            for k in range(GATHER_UNROLL):
                r = r8 * GATHER_UNROLL + k
                row_copy(buf, tok_ref[blk * tm + r], r).start()
            return carry
        lax.fori_loop(0, tm // GATHER_UNROLL, body, 0)

    def wait_block(buf):
        def body(r8, carry):
            for k in range(GATHER_UNROLL):
                row_copy(buf, 0, r8 * GATHER_UNROLL + k).wait()
            return carry
        lax.fori_loop(0, tm // GATHER_UNROLL, body, 0)

    rows_per_step = tm // nf
    has_next = i + 1 < pl.num_programs(0)

    @pl.when(f == 0)
    def _():
        o_ref[...] = jnp.zeros(o_ref.shape, F32)

        @pl.when(i == 0)
        def _():
            start_block(0, 0)

        @pl.when(i <= nu)
        def _():
            wait_block(slot)

    def issue_share():
        for k in range(rows_per_step):
            r = f * rows_per_step + k
            row_copy(1 - slot, tok_ref[(i + 1) * tm + r], r).start()

    @pl.when((i < nu) & has_next)
    def _():
        issue_share()
        _swiglu_accumulate(x_buf[slot].astype(BF16), w1_ref, w3_ref, w2_ref, o_ref)

    @pl.when((i < nu) & jnp.logical_not(has_next))
    def _():
        _swiglu_accumulate(x_buf[slot].astype(BF16), w1_ref, w3_ref, w2_ref, o_ref)


def _ffn(x, w1, w3, w2, blk_expert, n_used, tm, row_token=None):
    d = x.shape[1]
    n = x.shape[0] if row_token is None else row_token.shape[0]
    dff = w1.shape[2]
    tf = _pick(dff, (256, 128))
    nf = dff // tf

    def live(i, nu):
        return jnp.minimum(i, nu[0] - 1)

    def fidx(i, f, nu):
        return jnp.where(i < nu[0], f, nf - 1)

    w_specs = [
        pl.BlockSpec((1, d, tf), lambda i, f, be, nu, *_: (be[live(i, nu)], 0, fidx(i, f, nu))),
        pl.BlockSpec((1, d, tf), lambda i, f, be, nu, *_: (be[live(i, nu)], 0, fidx(i, f, nu))),
        pl.BlockSpec((1, tf, d), lambda i, f, be, nu, *_: (be[live(i, nu)], fidx(i, f, nu), 0)),
    ]
    if row_token is None:
        kern, prefetch, scratch = _ffn_kernel, (blk_expert, n_used), []
        x_spec = pl.BlockSpec((tm, d), lambda i, f, be, nu: (live(i, nu), 0))
    else:
        assert tm % nf == 0
        kern, prefetch = functools.partial(_ffn_gather_kernel, nf=nf), (blk_expert, n_used, row_token)
        scratch = [pltpu.VMEM((2, tm, d), x.dtype), pltpu.SemaphoreType.DMA((2,))]
        x_spec = pl.BlockSpec(memory_space=pl.ANY)
    grid_spec = pltpu.PrefetchScalarGridSpec(
        num_scalar_prefetch=len(prefetch),
        grid=(n // tm, nf),
        in_specs=[x_spec] + w_specs,
        out_specs=pl.BlockSpec((tm, d), lambda i, f, *_: (i, 0)),
        scratch_shapes=scratch,
    )
    return pl.pallas_call(
        kern,
        grid_spec=grid_spec,
        out_shape=jax.ShapeDtypeStruct((n, d), F32),
        compiler_params=_cparams(("arbitrary", "arbitrary")),
        name="swiglu",
    )(*prefetch, x, w1, w3, w2)


def _router_kernel(h_ref, w_ref, idx_ref, gate_ref):
    hh, hl = _split_bf16(h_ref[...])
    wh, wl = _split_bf16(w_ref[...])
    logits = _dot_nt(wh, hh) + _dot_nt(wl, hh) + _dot_nt(wh, hl)
    e = lax.broadcasted_iota(I32, logits.shape, 0).astype(F32)
    m1 = jnp.max(logits, axis=0, keepdims=True)
    i1 = jnp.min(jnp.where(logits == m1, e, float(N_EXPERTS)), axis=0, keepdims=True)
    rest = jnp.where(e == i1, -jnp.inf, logits)
    m2 = jnp.max(rest, axis=0, keepdims=True)
    i2 = jnp.min(jnp.where(rest == m2, e, float(N_EXPERTS)), axis=0, keepdims=True)
    e2 = jnp.exp(m2 - m1)
    den = 1.0 + e2
    idx_ref[...] = jnp.concatenate([i1, i2], axis=0).astype(I32)
    gate_ref[...] = jnp.concatenate([1.0 / den, e2 / den], axis=0)


def _router(h, w_router_t):
    t, d = h.shape
    tm = _pick(t, (384, 256, 128))
    return pl.pallas_call(
        _router_kernel,
        grid=(t // tm,),
        in_specs=[pl.BlockSpec((tm, d), lambda i: (i, 0)), pl.BlockSpec((N_EXPERTS, d), lambda i: (0, 0))],
        out_specs=[pl.BlockSpec((TOP_K_EXPERTS, tm), lambda i: (0, i)),
                   pl.BlockSpec((TOP_K_EXPERTS, tm), lambda i: (0, i))],
        out_shape=[jax.ShapeDtypeStruct((TOP_K_EXPERTS, t), I32),
                   jax.ShapeDtypeStruct((TOP_K_EXPERTS, t), F32)],
        compiler_params=_cparams(("parallel",)),
        name="router",
    )(h, w_router_t)


def _moe(h, w_router, w1, w3, w2, e_base, ln_g, ln_b):
    t, d = h.shape
    tm = MOE_ROWS
    top_idx, gates = _router(h, w_router.T)
    n_assign = t * TOP_K_EXPERTS
    expert = top_idx.reshape(-1)
    onehot = (expert[:, None] == jnp.arange(N_EXPERTS)[None, :]).astype(I32)
    counts = jnp.sum(onehot, axis=0)
    padded = (counts + tm - 1) // tm * tm
    pad_end = jnp.cumsum(padded)
    dest = jnp.sum((jnp.cumsum(onehot, axis=0) - onehot + (pad_end - padded)[None, :]) * onehot, axis=1)
    n_blk = -(-(n_assign + N_EXPERTS * (tm - 1)) // tm)
    n_rows = n_blk * tm
    token = jnp.arange(n_assign, dtype=I32) % t
    row_token = jnp.zeros((n_rows,), I32).at[dest].set(token)
    blk_expert = jnp.minimum(
        jnp.sum((jnp.arange(n_blk)[:, None] * tm >= pad_end[None, :]).astype(I32), axis=1), N_EXPERTS - 1)
    n_used = (pad_end[-1:] // tm).astype(I32)
    ys = _ffn(h, w1, w3, w2, blk_expert + e_base, n_used, tm, row_token=row_token)
    return _moe_combine_ln(h, ys, dest.astype(I32), gates.T, ln_g, ln_b)


def _rope_tables(lp, n_heads, head_dim, off, rot_dim, scale=1.0):
    r = rot_dim // 2
    pos = jnp.arange(lp)
    inv = ROPE_THETA ** (-jnp.arange(0, rot_dim, 2, dtype=F32) / rot_dim)
    ang = pos.astype(F32)[:, None] * inv[None, :]
    cos, sin = jnp.cos(ang), jnp.sin(ang)
    c = jnp.ones((lp, head_dim), F32).at[:, off:off + r].set(cos).at[:, off + r:off + 2 * r].set(cos)
    sa = jnp.zeros((lp, head_dim), F32).at[:, off:off + r].set(-sin)
    sb = jnp.zeros((lp, head_dim), F32).at[:, off + r:off + 2 * r].set(sin)
    return tuple(jnp.tile(x * scale, (1, n_heads)) for x in (c, sa, sb)) + (r,)


def _pad_cols(w, n):
    return jnp.pad(w, ((0, 0), (0, n - w.shape[1])))


def kernel(x, meta, ln_in_g, ln_in_b, w_in, b_gate, mla_q_norm, mla_kv_norm, w_uq, w_ukv, ssm_a_re, ssm_a_im, ssm_log_dt, ssm_b_re, ssm_b_im, ssm_c_re, ssm_c_im, ssm_d, w_glu, w_branch_a, w_branch_b, w_branch_c, w_o, ln1_g, ln1_b, ffn_w1, ffn_w3, ffn_w2, w_router, moe_w1, moe_w3, moe_w2, ln2_g, ln2_b):
    bsz, seq, _ = x.shape
    n_tok = seq + N_META
    lp = -(-n_tok // BLOCK) * BLOCK
    t = bsz * lp
    n_sel = min(TOPK_MAX, seq // 4)
    meta_b = jnp.broadcast_to(meta[None].astype(x.dtype), (bsz, N_META, D_MODEL))
    pad = jnp.zeros((bsz, lp - n_tok, D_MODEL), x.dtype)
    h, hb = _layernorm(jnp.concatenate([meta_b, x, pad], axis=1).reshape(t, D_MODEL), ln_in_g, ln_in_b)

    a_scale = A_HEAD_DIM ** -0.5 * LOG2E
    rope_q = _rope_tables(lp, 1, A_HEAD_DIM, 0, A_ROT, a_scale)
    rope_k = _rope_tables(lp, 1, A_HEAD_DIM, 0, A_ROT)
    iq_tab = _rope_tables(lp, IDX_HEADS, IDX_DIM, 0, IDX_ROT)
    ik_tab = _rope_tables(lp, 1, LANES, 0, IDX_ROT)
    w_scale = (jnp.zeros((LANES,), F32).at[:IDX_DIM].set(1.0)
               .at[IDX_DIM:IDX_DIM + IDX_HEADS].set((IDX_HEADS * IDX_DIM) ** -0.5))
    rope_i = tuple(jnp.concatenate([a, b * w_scale[None, :], jnp.zeros((lp, LANES), F32)], axis=1)
                   for a, b in zip(iq_tab[:3], ik_tab[:3])) + (IDX_ROT // 2,)
    n_idx = IDX_HEADS * IDX_DIM + 2 * LANES
    rope_kr = _rope_tables(lp, 1, LANES, 0, MLA_ROPE)
    m_scale = (MLA_NOPE + MLA_ROPE) ** -0.5 * LOG2E
    rope_mq = _rope_tables(lp, 1, 2 * LANES, MLA_NOPE, MLA_ROPE, m_scale)

    offs = [0]
    for s in IN_SIZES:
        offs.append(offs[-1] + s)
    o_aq, o_ak, o_av, o_iq, o_ik, o_iw, o_dq, o_dkv, o_kr, o_su, o_gl, o_end = offs

    w_o_bf16 = w_o.astype(BF16)

    for layer in range(DEPTH):
        w_md = _pad_cols(w_in[layer, :, o_dq:o_su], MLA_Q_LORA + MLA_KV_LORA + LANES)
        w_su = w_in[layer, :, o_su:o_gl]

        q_a = _proj(hb, w_in, w_cols=(layer, o_aq, o_ak - o_aq), out_dtype=BF16, lp=lp, rope=rope_q,
                    name="proj_aq")
        k_a = _proj(hb, w_in, w_cols=(layer, o_ak, o_av - o_ak), out_dtype=BF16, lp=lp, rope=rope_k,
                    name="proj_ak")
        v_a = _proj(hb, w_in, w_cols=(layer, o_av, o_iq - o_av), out_dtype=BF16, lp=lp, name="proj_av")
        idx = _proj(h, w_in, w_cols=(layer, o_iq, n_idx), out_dtype=F32, lp=lp, rope=rope_i, hi_prec=True,
                    tm_prefs=(704, 384, 128), tn_prefs=(n_idx,), name="proj_idx")
        bias = _indexer(idx, bsz, lp, n_sel)
        out_a = _dsa_attention(q_a, k_a, v_a, bias, bsz, lp)

        dqkv, kr = _mla_down(hb, w_md, rope_kr, lp)
        wq = w_uq[layer].reshape(MLA_Q_LORA, MLA_HEADS, MLA_NOPE + MLA_ROPE)
        wq = jnp.pad(wq, ((0, 0), (0, 0), (0, 2 * LANES - MLA_NOPE - MLA_ROPE))).reshape(MLA_Q_LORA, -1)
        q_m = _proj(dqkv, wq, out_dtype=BF16, lp=lp, x_col=0, kdim=MLA_Q_LORA, rope=rope_mq,
                    rms_gain=mla_q_norm[layer], name="proj_mq")
        kv_m = _proj(dqkv, w_ukv, w_cols=(layer, 0, w_ukv.shape[2]), out_dtype=BF16, lp=lp, x_col=1,
                     kdim=MLA_KV_LORA, rms_gain=mla_kv_norm[layer], name="proj_mkv")
        out_b = _mla_attention(q_m, kv_m, kr, bsz, lp)

        su = _proj(hb, w_su, out_dtype=F32, lp=lp, name="proj_su")
        s5w = _s5_weights(ssm_a_re[layer], ssm_a_im[layer], ssm_log_dt[layer], ssm_b_re[layer],
                          ssm_b_im[layer], ssm_c_re[layer], ssm_c_im[layer], ssm_d[layer])
        y = _s5(su, s5w, bsz, lp)
        out_c = _glu(y, w_glu, layer, lp)

        w_gates = [w_in[layer, :, o_gl + k * D_MODEL:o_gl + (k + 1) * D_MODEL] for k in range(N_BRANCH)]
        merged = _merge(hb, out_a, out_b, out_c, w_gates, b_gate[layer], w_branch_a, w_branch_b,
                        w_branch_c, layer, lp)
        h, hb = _proj_ln(merged, w_o_bf16, layer, h, ln1_g[layer], ln1_b[layer])

        i = layer // 2
        if layer % 2 == 0:
            tm = _pick(t, (1056, 384, 128))
            nblk = t // tm
            f = _ffn(hb, ffn_w1, ffn_w3, ffn_w2, jnp.full((nblk,), i, I32), jnp.full((1,), nblk, I32), tm)
            h, hb = _layernorm(h, ln2_g[layer], ln2_b[layer], res=f)
        else:
            n_all = moe_w1.shape[0] * N_EXPERTS
            h, hb = _moe(h, w_router[i], moe_w1.reshape((n_all,) + moe_w1.shape[2:]),
                         moe_w3.reshape((n_all,) + moe_w3.shape[2:]),
                         moe_w2.reshape((n_all,) + moe_w2.shape[2:]), i * N_EXPERTS,
                         ln2_g[layer], ln2_b[layer])

    return h.reshape(bsz, lp, D_MODEL)[:, N_META:N_META + seq]
```

```python
import functools
import math

import jax
import jax.numpy as jnp
from jax import lax
from jax.experimental import pallas as pl
from jax.experimental.pallas import tpu as pltpu

F32 = jnp.float32
BF16 = jnp.bfloat16
I32 = jnp.int32

D_MODEL = 2048
DEPTH = 4
N_META = 16
BLOCK = 128
ROPE_THETA = 500000.0
LN_EPS = 1e-5
RMS_EPS = 1e-6
ALPHA = (2 * DEPTH) ** 0.25

A_HEADS = 8
A_KV_HEADS = 2
A_HEAD_DIM = 128
A_ROT = A_HEAD_DIM // 4
IDX_HEADS = 8
IDX_DIM = 64
IDX_ROT = IDX_DIM // 4
TOPK_MAX = 256

MLA_HEADS = 8
MLA_Q_LORA = 512
MLA_KV_LORA = 512
MLA_NOPE = 128
MLA_ROPE = 64
MLA_V = 128

SSM_WIDTH = 1024
SSM_GROUP = 16
SSM_GROUPS = SSM_WIDTH // SSM_GROUP
SSM_STATE = 64
SSM_CHUNK = 16

N_BRANCH = 3
A_WIDTH = A_HEADS * A_HEAD_DIM
MLA_WIDTH = MLA_HEADS * MLA_V
IN_SIZES = (A_HEADS * A_HEAD_DIM, A_KV_HEADS * A_HEAD_DIM, A_KV_HEADS * A_HEAD_DIM,
            IDX_HEADS * IDX_DIM, IDX_DIM, IDX_HEADS,
            MLA_Q_LORA, MLA_KV_LORA, MLA_ROPE,
            SSM_WIDTH, N_BRANCH * D_MODEL)

D_FF = 5632
N_EXPERTS = 8
TOP_K_EXPERTS = 2

LANES = 128
SUBLANES = 8
VMEM_LIMIT = 56 * 1024 * 1024
INT_MIN = -2 ** 31
LOG2E = math.log2(math.e)
GATHER_UNROLL = 8
MOE_ROWS = 704


def _pick(n, prefs):
    for p in prefs:
        if n % p == 0:
            return p
    raise ValueError(f"no tile in {prefs} divides {n}")


def _cparams(sem):
    return pltpu.CompilerParams(dimension_semantics=sem, vmem_limit_bytes=VMEM_LIMIT)


def _dot(a, b):
    return jnp.dot(a, b, preferred_element_type=F32)


def _dot_nt(a, b):
    return lax.dot_general(a, b, (((1,), (1,)), ((), ())), preferred_element_type=F32)


def _split_bf16(x):
    hi = x.astype(BF16)
    lo = (x - hi.astype(F32)).astype(BF16)
    return hi, lo


def _ln_math(x, g, b):
    mu = jnp.mean(x, axis=-1, keepdims=True)
    xc = x - mu
    var = jnp.mean(xc * xc, axis=-1, keepdims=True)
    return xc * lax.rsqrt(var + LN_EPS) * g + b


def _ln_kernel(x_ref, g_ref, b_ref, o_ref, ob_ref):
    y = _ln_math(x_ref[...], g_ref[...], b_ref[...])
    o_ref[...] = y
    ob_ref[...] = y.astype(BF16)


def _ln_res_kernel(h_ref, r_ref, g_ref, b_ref, o_ref, ob_ref):
    y = _ln_math(ALPHA * h_ref[...] + r_ref[...], g_ref[...], b_ref[...])
    o_ref[...] = y
    ob_ref[...] = y.astype(BF16)


def _moe_ln_kernel(pos_ref, h_ref, ys_ref, gt_ref, g_ref, b_ref, o_ref, ob_ref, y_buf, sem):
    i = pl.program_id(0)
    tm, t = h_ref.shape[0], pos_ref.shape[0] // TOP_K_EXPERTS
    slot = i % 2

    def row_copy(buf, k, src_row, r):
        return pltpu.make_async_copy(ys_ref.at[pl.ds(src_row, 1), :],
                                     y_buf.at[buf, k, pl.ds(r, 1), :], sem.at[buf])

    def start_block(blk, buf):
        def body(r8, carry):
            for k in range(TOP_K_EXPERTS):
                for u in range(GATHER_UNROLL):
                    r = r8 * GATHER_UNROLL + u
                    row_copy(buf, k, pos_ref[k * t + blk * tm + r], r).start()
            return carry
        lax.fori_loop(0, tm // GATHER_UNROLL, body, 0)

    def wait_block(buf):
        def body(r8, carry):
            for k in range(TOP_K_EXPERTS):
                for u in range(GATHER_UNROLL):
                    row_copy(buf, k, 0, r8 * GATHER_UNROLL + u).wait()
            return carry
        lax.fori_loop(0, tm // GATHER_UNROLL, body, 0)

    @pl.when(i == 0)
    def _():
        start_block(0, 0)

    wait_block(slot)

    @pl.when(i + 1 < pl.num_programs(0))
    def _():
        start_block(i + 1, 1 - slot)

    gt = gt_ref[...]
    f = y_buf[slot, 0] * gt[:, 0:1] + y_buf[slot, 1] * gt[:, 1:2]
    y = _ln_math(ALPHA * h_ref[...] + f, g_ref[...], b_ref[...])
    o_ref[...] = y
    ob_ref[...] = y.astype(BF16)


def _layernorm(x, g, b, res=None):
    t, d = x.shape
    tm = _pick(t, (384, 256, 128))
    row = pl.BlockSpec((tm, d), lambda i: (i, 0))
    vec = pl.BlockSpec((1, d), lambda i: (0, 0))
    ins = [x] if res is None else [x, res]
    return pl.pallas_call(
        _ln_kernel if res is None else _ln_res_kernel,
        grid=(t // tm,),
        in_specs=[row] * len(ins) + [vec, vec],
        out_specs=[row, row],
        out_shape=[jax.ShapeDtypeStruct((t, d), F32), jax.ShapeDtypeStruct((t, d), BF16)],
        compiler_params=_cparams(("parallel",)),
        name="layernorm",
    )(*ins, g.reshape(1, d), b.reshape(1, d))


def _proj_ln_kernel(x_ref, w_ref, h_ref, g_ref, b_ref, o_ref, ob_ref):
    y = _ln_math(ALPHA * h_ref[...] + _dot(x_ref[...], w_ref[...]), g_ref[...], b_ref[...])
    o_ref[...] = y
    ob_ref[...] = y.astype(BF16)


def _proj_ln(x, w, layer, h, g, b):
    t, k = x.shape
    d = w.shape[2]
    tm = _pick(t, (384, 256, 128))
    row = pl.BlockSpec((tm, d), lambda i: (i, 0))
    vec = pl.BlockSpec((1, d), lambda i: (0, 0))
    return pl.pallas_call(
        _proj_ln_kernel,
        grid=(t // tm,),
        in_specs=[pl.BlockSpec((tm, k), lambda i: (i, 0)),
                  pl.BlockSpec((None, k, d), lambda i: (layer, 0, 0)), row, vec, vec],
        out_specs=[row, row],
        out_shape=[jax.ShapeDtypeStruct((t, d), F32), jax.ShapeDtypeStruct((t, d), BF16)],
        compiler_params=_cparams(("parallel",)),
        name="proj_o_ln",
    )(x, w, h, g.reshape(1, d), b.reshape(1, d))


def _moe_combine_ln(h, ys, pos, gates_t, g, b):
    t, d = h.shape
    tm = _pick(t, (384, 256, 128))
    row = pl.BlockSpec((tm, d), lambda i, pos_ref: (i, 0))
    vec = pl.BlockSpec((1, d), lambda i, pos_ref: (0, 0))
    grid_spec = pltpu.PrefetchScalarGridSpec(
        num_scalar_prefetch=1,
        grid=(t // tm,),
        in_specs=[row, pl.BlockSpec(memory_space=pl.ANY),
                  pl.BlockSpec((tm, TOP_K_EXPERTS), lambda i, pos_ref: (i, 0)), vec, vec],
        out_specs=[row, row],
        scratch_shapes=[pltpu.VMEM((2, TOP_K_EXPERTS, tm, d), ys.dtype), pltpu.SemaphoreType.DMA((2,))],
    )
    return pl.pallas_call(
        _moe_ln_kernel,
        grid_spec=grid_spec,
        out_shape=[jax.ShapeDtypeStruct((t, d), F32), jax.ShapeDtypeStruct((t, d), BF16)],
        compiler_params=_cparams(("arbitrary",)),
        name="moe_combine_ln",
    )(pos, h, ys, gates_t, g.reshape(1, d), b.reshape(1, d))


def _rope_chunk(a, c, sa, sb, r):
    return a * c + pltpu.roll(a, LANES - r, 1) * sa + pltpu.roll(a, r, 1) * sb


def _proj_kernel(*refs, rope_r, has_bias, sigmoid, hi_prec, rms):
    it = iter(refs)
    x_ref = next(it)
    w_ref = next(it)
    n_ref = next(it) if rms else None
    tabs = (next(it), next(it), next(it)) if rope_r else None
    b_ref = next(it) if has_bias else None
    o_ref = next(it)
    x = x_ref[...]
    if rms:
        ms = jnp.mean(x * x, axis=-1, keepdims=True)
        x = x * lax.rsqrt(ms + RMS_EPS) * n_ref[...]
    if hi_prec:
        xh, xl = _split_bf16(x)
        wh, wl = _split_bf16(w_ref[...])
        acc = _dot(xh, wh) + _dot(xl, wh) + _dot(xh, wl)
    else:
        acc = _dot(x.astype(BF16), w_ref[...].astype(BF16))
    if has_bias:
        acc = acc + b_ref[...]
    if sigmoid:
        acc = jax.nn.sigmoid(acc)
    if rope_r:
        c_ref, sa_ref, sb_ref = tabs
        pw = c_ref.shape[1]
        for c in range(acc.shape[1] // LANES):
            sl = slice(c * LANES, (c + 1) * LANES)
            ts = slice((c * LANES) % pw, (c * LANES) % pw + LANES)
            o_ref[:, sl] = _rope_chunk(acc[:, sl], c_ref[:, ts], sa_ref[:, ts], sb_ref[:, ts],
                                       rope_r).astype(o_ref.dtype)
    else:
        o_ref[...] = acc.astype(o_ref.dtype)


def _proj(x, w, *, out_dtype, lp, x_col=0, kdim=None, w_cols=None, rope=None, bias=None, sigmoid=False,
          hi_prec=False, rms_gain=None, tm_prefs=(2112, 768, 384, 128), tn_prefs=(512, 384, 256, 128),
          name="proj"):
    t = x.shape[0]
    kdim = x.shape[1] if kdim is None else kdim
    tm = _pick(lp, tm_prefs)
    nrow = lp // tm
    if w_cols is None:
        n = w.shape[1]
        tn = _pick(n, tn_prefs)
        w_spec = pl.BlockSpec((kdim, tn), lambda i, j: (0, j))
    else:
        layer, col0, n = w_cols
        tn = _pick(math.gcd(n, col0) if col0 else n, tn_prefs)
        cb0 = col0 // tn
        w_spec = pl.BlockSpec((None, kdim, tn), lambda i, j: (layer, 0, cb0 + j))
    in_specs = [pl.BlockSpec((tm, kdim), lambda i, j: (i, x_col)), w_spec]
    ins = [x, w]
    if rms_gain is not None:
        in_specs.append(pl.BlockSpec((1, kdim), lambda i, j: (0, 0)))
        ins.append(rms_gain.reshape(1, kdim))
    if rope is not None:
        pw = rope[0].shape[1]
        if pw == n:
            tab = pl.BlockSpec((tm, tn), lambda i, j: (i % nrow, j))
        else:
            assert tn % pw == 0
            tab = pl.BlockSpec((tm, pw), lambda i, j: (i % nrow, 0))
        in_specs += [tab, tab, tab]
        ins += list(rope[:3])
    if bias is not None:
        in_specs.append(pl.BlockSpec((1, tn), lambda i, j: (0, j)))
        ins.append(bias.reshape(1, n))
    kern = functools.partial(_proj_kernel, rope_r=rope[3] if rope is not None else 0,
                             has_bias=bias is not None, sigmoid=sigmoid, hi_prec=hi_prec,
                             rms=rms_gain is not None)
    return pl.pallas_call(
        kern,
        grid=(t // tm, n // tn),
        in_specs=in_specs,
        out_specs=pl.BlockSpec((tm, tn), lambda i, j: (i, j)),
        out_shape=jax.ShapeDtypeStruct((t, n), out_dtype),
        compiler_params=_cparams(("parallel", "arbitrary")),
        name=name,
    )(*ins)


def _mla_down_kernel(x_ref, w_ref, c_ref, sa_ref, sb_ref, o_ref, kr_ref):
    acc = _dot(x_ref[...], w_ref[...].astype(BF16))
    nq = o_ref.shape[1]
    o_ref[...] = acc[:, :nq]
    kr_ref[...] = _rope_chunk(acc[:, nq:], c_ref[...], sa_ref[...], sb_ref[...],
                              MLA_ROPE // 2).astype(BF16)


def _mla_down(hb, w, rope, lp):
    t, d = hb.shape
    n = w.shape[1]
    nq = MLA_Q_LORA + MLA_KV_LORA
    tm = _pick(lp, (704, 384, 128))
    nrow = lp // tm
    tab = pl.BlockSpec((tm, LANES), lambda i: (i % nrow, 0))
    return pl.pallas_call(
        _mla_down_kernel,
        grid=(t // tm,),
        in_specs=[pl.BlockSpec((tm, d), lambda i: (i, 0)), pl.BlockSpec((d, n), lambda i: (0, 0)),
                  tab, tab, tab],
        out_specs=[pl.BlockSpec((tm, nq), lambda i: (i, 0)), pl.BlockSpec((tm, LANES), lambda i: (i, 0))],
        out_shape=[jax.ShapeDtypeStruct((t, nq), F32), jax.ShapeDtypeStruct((t, LANES), BF16)],
        compiler_params=_cparams(("parallel",)),
        name="mla_down",
    )(hb, w, *rope[:3])


def _glu_kernel(y_ref, w_ref, o_ref):
    n = o_ref.shape[1]
    y = y_ref[...].astype(BF16)
    ga = _dot(y, w_ref[:, :n].astype(BF16))
    gb = _dot(y, w_ref[:, n:].astype(BF16))
    o_ref[...] = (ga * jax.nn.sigmoid(gb)).astype(o_ref.dtype)


def _glu(y, w_glu, layer, lp):
    t, k = y.shape
    n = w_glu.shape[2] // 2
    tm = _pick(lp, (704, 384, 128))
    return pl.pallas_call(
        _glu_kernel,
        grid=(t // tm,),
        in_specs=[pl.BlockSpec((tm, k), lambda i: (i, 0)),
                  pl.BlockSpec((None, k, 2 * n), lambda i: (layer, 0, 0))],
        out_specs=pl.BlockSpec((tm, n), lambda i: (i, 0)),
        out_shape=jax.ShapeDtypeStruct((t, n), BF16),
        compiler_params=_cparams(("parallel",)),
        name="glu",
    )(y, w_glu)


def _merge_kernel(h_ref, a_ref, b_ref, c_ref, wg0_ref, wg1_ref, wg2_ref, bg_ref,
                  wa_ref, wb_ref, wc_ref, o_ref):
    h = h_ref[...]
    m = None
    for k, (wg_ref, x_ref, w_ref) in enumerate(((wg0_ref, a_ref, wa_ref), (wg1_ref, b_ref, wb_ref),
                                                (wg2_ref, c_ref, wc_ref))):
        gate = jax.nn.sigmoid(_dot(h, wg_ref[...].astype(BF16)) + bg_ref[k])
        term = gate * _dot(x_ref[...], w_ref[...].astype(BF16))
        m = term if m is None else m + term
    o_ref[...] = m.astype(o_ref.dtype)


def _merge(hb, out_a, out_b, out_c, w_gates, b_gate, wb_a, wb_b, wb_c, layer, lp):
    t = out_a.shape[0]
    n = wb_a.shape[2]
    tm = _pick(lp, (1056, 384, 128))
    tn = _pick(n, (256, 128))

    def act(arr):
        return pl.BlockSpec((tm, arr.shape[1]), lambda i, j: (i, 0))

    def wgt(arr):
        return pl.BlockSpec((None, arr.shape[1], tn), lambda i, j: (layer, 0, j))

    gate_w = pl.BlockSpec((hb.shape[1], tn), lambda i, j: (0, j))
    return pl.pallas_call(
        _merge_kernel,
        grid=(t // tm, n // tn),
        in_specs=[act(hb), act(out_a), act(out_b), act(out_c), gate_w, gate_w, gate_w,
                  pl.BlockSpec((N_BRANCH, 1, tn), lambda i, j: (0, 0, j)),
                  wgt(wb_a), wgt(wb_b), wgt(wb_c)],
        out_specs=pl.BlockSpec((tm, tn), lambda i, j: (i, j)),
        out_shape=jax.ShapeDtypeStruct((t, n), BF16),
        compiler_params=_cparams(("parallel", "arbitrary")),
        name="merge",
    )(hb, out_a, out_b, out_c, *w_gates, b_gate.reshape(N_BRANCH, 1, n), wb_a, wb_b, wb_c)


def _softmax_update(s, v, m_ref, l_ref, acc_ref):
    tk = s.shape[1]
    m_prev = m_ref[...]
    m_new = jnp.maximum(m_prev, jnp.max(s, axis=1, keepdims=True))
    alpha = jnp.exp2(m_prev - m_new)
    p = jnp.exp2(s - jnp.concatenate([m_new] * (tk // LANES), axis=1))
    l_ref[...] = alpha * l_ref[...] + jnp.sum(p, axis=1, keepdims=True)
    acc_ref[...] = alpha * acc_ref[...] + _dot(p.astype(BF16), v)
    m_ref[...] = m_new


def _mla_attn_kernel(q_ref, kv_ref, kr_ref, o_ref, s_ref, m_ref, l_ref, acc_ref):
    qi = pl.program_id(2)
    tq = q_ref.shape[0]
    m_ref[...] = jnp.full(m_ref.shape, -jnp.inf, F32)
    l_ref[...] = jnp.zeros(l_ref.shape, F32)
    acc_ref[...] = jnp.zeros(acc_ref.shape, F32)
    q = q_ref[...]

    def rows_of(kt):
        return pl.ds(pl.multiple_of(kt * tq, tq), tq)

    def produce(buf, kt):
        rows = rows_of(jnp.minimum(kt, qi))
        s_ref[buf] = _dot_nt(q, jnp.concatenate([kv_ref[rows, :MLA_NOPE], kr_ref[rows, :]], axis=1))

    def consume(s, kt):
        _softmax_update(s, kv_ref[rows_of(kt), MLA_NOPE:], m_ref, l_ref, acc_ref)

    produce(0, 0)
    produce(1, 1)

    def body(kp, carry):
        s0 = s_ref[0]
        s1 = s_ref[1]
        produce(0, 2 * kp + 2)
        produce(1, 2 * kp + 3)
        consume(s0, 2 * kp)
        consume(s1, 2 * kp + 1)
        return carry

    lax.fori_loop(0, qi // 2, body, 0)

    @pl.when(qi % 2 == 1)
    def _():
        consume(s_ref[0], qi - 1)

    row = lax.broadcasted_iota(I32, (tq, tq), 0)
    col = lax.broadcasted_iota(I32, (tq, tq), 1)
    consume(jnp.where(col <= row, s_ref[qi % 2], -jnp.inf), qi)
    o_ref[...] = (acc_ref[...] / l_ref[...]).astype(o_ref.dtype)


def _mla_attention(q, kv, kr, bsz, lp):
    t = q.shape[0]
    tq = _pick(lp, (384, 128))
    nq = lp // tq
    q3 = q.reshape(bsz, lp, q.shape[1])
    kv3 = kv.reshape(bsz, lp, kv.shape[1])
    kr3 = kr.reshape(bsz, lp, LANES)
    out = pl.pallas_call(
        _mla_attn_kernel,
        grid=(bsz, MLA_HEADS, nq),
        in_specs=[
            pl.BlockSpec((None, tq, 2 * LANES), lambda b, h, i: (b, i, h)),
            pl.BlockSpec((None, lp, MLA_NOPE + MLA_V), lambda b, h, i: (b, 0, h)),
            pl.BlockSpec((None, lp, LANES), lambda b, h, i: (b, 0, 0)),
        ],
        out_specs=pl.BlockSpec((None, tq, LANES), lambda b, h, i: (b, i, h)),
        out_shape=jax.ShapeDtypeStruct((bsz, lp, MLA_WIDTH), BF16),
        scratch_shapes=[pltpu.VMEM((2, tq, tq), F32),
                        pltpu.VMEM((tq, LANES), F32), pltpu.VMEM((tq, LANES), F32),
                        pltpu.VMEM((tq, MLA_V), F32)],
        compiler_params=_cparams(("parallel", "parallel", "arbitrary")),
        name="mla_attention",
    )(q3, kv3, kr3)
    return out.reshape(t, MLA_WIDTH)


def _dsa_attn_kernel(q_ref, k_ref, v_ref, bias_ref, o_ref, qg_ref, s_ref, m_ref, l_ref, acc_ref, *, tk):
    qi = pl.program_id(1)
    rep = A_HEADS // A_KV_HEADS
    nkt = tk // BLOCK
    m_ref[...] = jnp.full(m_ref.shape, -jnp.inf, F32)
    l_ref[...] = jnp.zeros(l_ref.shape, F32)
    acc_ref[...] = jnp.zeros(acc_ref.shape, F32)
    for g in range(A_KV_HEADS):
        for r in range(rep):
            h = g * rep + r
            qg_ref[g, r * BLOCK:(r + 1) * BLOCK, :] = q_ref[:, h * A_HEAD_DIM:(h + 1) * A_HEAD_DIM]

    def rows_of(kt):
        return pl.ds(pl.multiple_of(kt * tk, tk), tk)

    def scores(kt):
        rows = rows_of(kt)
        for g in range(A_KV_HEADS):
            s_ref[g] = _dot_nt(qg_ref[g], k_ref[rows, g * A_HEAD_DIM:(g + 1) * A_HEAD_DIM])

    def consume(kt, s):
        bias = jnp.concatenate([bias_ref[kt * nkt + j] for j in range(nkt)], axis=1)
        bias = jnp.concatenate([bias.astype(F32)] * rep, axis=0)
        for g in range(A_KV_HEADS):
            _softmax_update(s[g] + bias, v_ref[rows_of(kt), g * A_HEAD_DIM:(g + 1) * A_HEAD_DIM],
                            m_ref.at[g], l_ref.at[g], acc_ref.at[g])

    scores(0)

    def body(kt, carry):
        s = [s_ref[g] for g in range(A_KV_HEADS)]
        scores(kt + 1)
        consume(kt, s)
        return carry

    last = (qi * BLOCK) // tk
    lax.fori_loop(0, last, body, 0)
    consume(last, [s_ref[g] for g in range(A_KV_HEADS)])
    for g in range(A_KV_HEADS):
        o = acc_ref[g] / l_ref[g]
        for r in range(rep):
            h = g * rep + r
            o_ref[:, h * A_HEAD_DIM:(h + 1) * A_HEAD_DIM] = (
                o[r * BLOCK:(r + 1) * BLOCK]).astype(o_ref.dtype)


def _dsa_attention(q, k, v, bias, bsz, lp):
    t = q.shape[0]
    nb = lp // BLOCK
    tk = _pick(lp, (384, 128))
    kvw = A_KV_HEADS * A_HEAD_DIM
    q3 = q.reshape(bsz, lp, A_WIDTH)
    k3 = k.reshape(bsz, lp, kvw)
    v3 = v.reshape(bsz, lp, kvw)
    rows = (A_HEADS // A_KV_HEADS) * BLOCK
    out = pl.pallas_call(
        functools.partial(_dsa_attn_kernel, tk=tk),
        grid=(bsz, nb),
        in_specs=[
            pl.BlockSpec((None, BLOCK, A_WIDTH), lambda b, i: (b, i, 0)),
            pl.BlockSpec((None, lp, kvw), lambda b, i: (b, 0, 0)),
            pl.BlockSpec((None, lp, kvw), lambda b, i: (b, 0, 0)),
            pl.BlockSpec((None, None, nb, BLOCK, BLOCK), lambda b, i: (b, i, 0, 0, 0)),
        ],
        out_specs=pl.BlockSpec((None, BLOCK, A_WIDTH), lambda b, i: (b, i, 0)),
        out_shape=jax.ShapeDtypeStruct((bsz, lp, A_WIDTH), BF16),
        scratch_shapes=[pltpu.VMEM((A_KV_HEADS, rows, A_HEAD_DIM), BF16),
                        pltpu.VMEM((A_KV_HEADS, rows, tk), F32),
                        pltpu.VMEM((A_KV_HEADS, rows, LANES), F32),
                        pltpu.VMEM((A_KV_HEADS, rows, LANES), F32),
                        pltpu.VMEM((A_KV_HEADS, rows, A_HEAD_DIM), F32)],
        compiler_params=_cparams(("parallel", "arbitrary")),
        name="dsa_attention",
    )(q3, k3, v3, bias)
    return out.reshape(t, A_WIDTH)


def _indexer_kernel(iq_ref, kw_ref, o_ref, kcat_ref, qcat_ref, wb_ref, key_ref, *, n_sel):
    qb = pl.program_id(0)
    bsz, nb = o_ref.shape[0], o_ref.shape[1]
    batch = range(bsz)
    half = IDX_DIM
    lane = lax.broadcasted_iota(I32, (BLOCK, LANES), 1)
    row = lax.broadcasted_iota(I32, (BLOCK, LANES), 0)

    @pl.when(qb == 0)
    def _():
        for b in batch:
            kf = kw_ref[b]
            klane = lax.broadcasted_iota(I32, kf.shape, 1)
            kz = jnp.where(klane < half, kf, 0.0)
            hi = kz.astype(BF16).astype(F32)
            lo = kz - hi
            lp = kf.shape[0]
            kcat_ref[b, 0:lp, :] = jnp.concatenate(
                [(hi + pltpu.roll(hi, half, 1)).astype(BF16), lo.astype(BF16)], axis=1)
            kcat_ref[b, lp:lp + BLOCK, :] = jnp.zeros((BLOCK, 2 * LANES), BF16)

    for b in batch:
        wq = kw_ref[b, pl.ds(pl.multiple_of(qb * BLOCK, BLOCK), BLOCK), :]
        for h in range(IDX_HEADS):
            chunk = iq_ref[b, :, (h // 2) * LANES:(h // 2 + 1) * LANES]
            if h % 2 == 0:
                a = jnp.where(lane < half, chunk, 0.0)
            else:
                a = pltpu.roll(jnp.where(lane >= half, chunk, 0.0), half, 1)
            hi = a.astype(BF16).astype(F32)
            lo = a - hi
            qcat_ref[b, h * BLOCK:(h + 1) * BLOCK, :] = jnp.concatenate(
                [(hi + pltpu.roll(lo, half, 1)).astype(BF16), hi.astype(BF16)], axis=1)
            wb_ref[b, h] = jnp.broadcast_to(wq[:, half + h:half + h + 1], (BLOCK, LANES))

    qpos = qb * BLOCK + row

    def sortable(x):
        b = pltpu.bitcast(x, I32)
        return b ^ ((b >> 31) & 0x7FFFFFFF)

    n_pair = (qb + 2) // 2

    def score_pair(kp, carry):
        for b in batch:
            kblk = kcat_ref[b, pl.ds(pl.multiple_of(kp * 2 * BLOCK, 2 * BLOCK), 2 * BLOCK), :]
            s = _dot_nt(qcat_ref[b], kblk)
            for j in range(2):
                sc = jnp.zeros((BLOCK, LANES), F32)
                for h in range(IDX_HEADS):
                    sc = sc + (jnp.maximum(s[h * BLOCK:(h + 1) * BLOCK, j * LANES:(j + 1) * LANES], 0.0)
                               * wb_ref[b, h])
                kpos = (kp * 2 + j) * BLOCK + lane
                sc = jnp.where(kpos < N_META, jnp.inf, sc)
                sc = jnp.where(kpos <= qpos, sc, -jnp.inf)
                key_ref[b, kp * 2 + j] = sortable(sc)
        return carry

    lax.fori_loop(0, n_pair, score_pair, 0)

    def count(pred):
        def body(kp, cs):
            return tuple(c + jnp.where(pred(key_ref[b, kp * 2], b), 1.0, 0.0)
                         + jnp.where(pred(key_ref[b, kp * 2 + 1], b), 1.0, 0.0)
                         for b, c in zip(batch, cs))
        cs = lax.fori_loop(0, n_pair, body, tuple(jnp.zeros((BLOCK, LANES), F32) for _ in batch))
        return tuple(jnp.sum(c, axis=1, keepdims=True) for c in cs)

    def bit_step(i, thrs):
        bit = 31 - i
        cands = tuple(jnp.where(bit == 31, thr ^ INT_MIN, thr | (1 << jnp.minimum(bit, 30)))
                      for thr in thrs)
        counts = count(lambda k, b: k >= cands[b])
        return tuple(jnp.where(n >= n_sel, cand, thr) for n, cand, thr in zip(counts, cands, thrs))

    thrs = lax.fori_loop(0, 32, bit_step,
                         tuple(jnp.full((BLOCK, LANES), INT_MIN, I32) for _ in batch))
    n_ge = count(lambda k, b: k >= thrs[b])
    all_ties_fit = functools.reduce(jnp.maximum, [jnp.max(n) for n in n_ge]) <= n_sel

    @pl.when(all_ties_fit)
    def _():
        def emit(kt, carry):
            kpos = kt * BLOCK + lane
            for b in batch:
                o_ref[b, kt] = jnp.where((key_ref[b, kt] >= thrs[b]) & (kpos <= qpos),
                                         0.0, -jnp.inf).astype(o_ref.dtype)
            return carry

        lax.fori_loop(0, qb + 1, emit, 0)

    @pl.when(jnp.logical_not(all_ties_fit))
    def _():
        n_gt = count(lambda k, b: k > thrs[b])
        tri = (lax.broadcasted_iota(I32, (LANES, LANES), 0)
               <= lax.broadcasted_iota(I32, (LANES, LANES), 1)).astype(BF16)

        def emit(kt, takens):
            out = []
            kpos = kt * BLOCK + lane
            for b in batch:
                key = key_ref[b, kt]
                eq = key == thrs[b]
                rank = _dot(jnp.where(eq, 1.0, 0.0).astype(BF16), tri)
                sel = (key > thrs[b]) | (eq & (takens[b] + rank <= n_sel - n_gt[b]))
                o_ref[b, kt] = jnp.where(sel & (kpos <= qpos), 0.0, -jnp.inf).astype(o_ref.dtype)
                out.append(takens[b] + rank[:, LANES - 1:LANES])
            return tuple(out)

        lax.fori_loop(0, qb + 1, emit, tuple(jnp.zeros((BLOCK, 1), F32) for _ in batch))

    def fill(kt, carry):
        for b in batch:
            o_ref[b, kt] = jnp.full((BLOCK, LANES), -jnp.inf, o_ref.dtype)
        return carry

    lax.fori_loop(qb + 1, nb, fill, 0)


def _indexer(idx, bsz, lp, n_sel):
    nb = lp // BLOCK
    idx3 = idx.reshape(bsz, lp, idx.shape[1])
    nq = IDX_HEADS * IDX_DIM
    kw_col = nq // LANES
    return pl.pallas_call(
        functools.partial(_indexer_kernel, n_sel=n_sel),
        grid=(nb,),
        in_specs=[
            pl.BlockSpec((bsz, BLOCK, nq), lambda i: (0, i, 0)),
            pl.BlockSpec((bsz, lp, LANES), lambda i: (0, 0, kw_col)),
        ],
        out_specs=pl.BlockSpec((bsz, None, nb, BLOCK, BLOCK), lambda i: (0, i, 0, 0, 0)),
        out_shape=jax.ShapeDtypeStruct((bsz, nb, nb, BLOCK, BLOCK), BF16),
        scratch_shapes=[pltpu.VMEM((bsz, lp + BLOCK, 2 * LANES), BF16),
                        pltpu.VMEM((bsz, IDX_HEADS * BLOCK, 2 * LANES), BF16),
                        pltpu.VMEM((bsz, IDX_HEADS, BLOCK, LANES), F32),
                        pltpu.VMEM((bsz, nb + 1, BLOCK, LANES), I32)],
        compiler_params=_cparams(("arbitrary",)),
        name="indexer",
    )(idx3, idx3)


def _s5_kernel(x_ref, mw_ref, vre_ref, vim_ref, are_ref, aim_ref, o_ref,
               perm_ref, y_ref, yb_ref, sre_ref, sim_ref, pre_ref, pim_ref, *, bsz):
    tc, c, p = SSM_CHUNK, SSM_GROUP, SSM_STATE
    ng = LANES // c
    nm = tc * c
    rows = x_ref.shape[0] // tc
    per_b = rows // bsz
    width = tc * LANES

    @pl.when(pl.program_id(0) == 0)
    def _():
        src = lax.broadcasted_iota(I32, (width, width), 0)
        dst = lax.broadcasted_iota(I32, (width, width), 1)
        group = (src & (LANES - 1)) >> (c.bit_length() - 1)
        step_in_chunk = src >> (LANES.bit_length() - 1)
        want = group * nm + step_in_chunk * c + (src & (c - 1))
        perm_ref[...] = jnp.where(dst == want, 1.0, 0.0).astype(BF16)

    x_all = jnp.concatenate(
        [x_ref[pl.ds(j, rows, stride=tc), :].astype(BF16) for j in range(tc)], axis=1)
    u_all = _dot(x_all, perm_ref[...]).astype(BF16)
    for g in range(ng):
        r = _dot(u_all[:, g * nm:(g + 1) * nm], mw_ref[g])
        y_ref[:, g * nm:(g + 1) * nm] = r[:, :nm]
        sre_ref[:, g * p:(g + 1) * p] = r[:, nm:nm + p]
        sim_ref[:, g * p:(g + 1) * p] = r[:, nm + p:nm + 2 * p]
    ar = are_ref[...]
    ai = aim_ref[...]

    def step(ti, carry):
        out = []
        for b in range(bsz):
            cr, ci = carry[b]
            rs = pl.ds(pl.multiple_of(b * per_b + ti * SUBLANES, SUBLANES), SUBLANES)
            lr = sre_ref[rs, :]
            li = sim_ref[rs, :]
            before_r, before_i = [], []
            for k in range(SUBLANES):
                before_r.append(cr)
                before_i.append(ci)
                cr, ci = (ar * cr - ai * ci + lr[k:k + 1], ar * ci + ai * cr + li[k:k + 1])
            pre_ref[rs, :] = jnp.concatenate(before_r, axis=0)
            pim_ref[rs, :] = jnp.concatenate(before_i, axis=0)
            out.append((cr, ci))
        return tuple(out)

    zero = jnp.zeros((1, ng * p), F32)
    lax.fori_loop(0, per_b // SUBLANES, step, tuple((zero, zero) for _ in range(bsz)))
    for g in range(ng):
        y = (y_ref[:, g * nm:(g + 1) * nm]
             + _dot(pre_ref[:, g * p:(g + 1) * p].astype(BF16), vre_ref[g])
             + _dot(pim_ref[:, g * p:(g + 1) * p].astype(BF16), vim_ref[g]))
        yb_ref[:, g * nm:(g + 1) * nm] = y.astype(BF16)
    y_all = _dot_nt(yb_ref[...], perm_ref[...])
    for j in range(tc):
        o_ref[pl.ds(j, rows, stride=tc), :] = y_all[:, j * LANES:(j + 1) * LANES]


def _s5_weights(a_re, a_im, log_dt, b_re, b_im, c_re, c_im, d_skip):
    hp = lax.Precision.HIGHEST
    g, p, c, tc = SSM_GROUPS, SSM_STATE, SSM_GROUP, SSM_CHUNK
    dt = jnp.exp(log_dt)[:, None]
    lam_re, lam_im = dt * a_re, dt * a_im
    mag = jnp.exp(lam_re)
    ab_re, ab_im = mag * jnp.cos(lam_im), mag * jnp.sin(lam_im)
    den = a_re * a_re + a_im * a_im
    f_re = ((ab_re - 1.0) * a_re + ab_im * a_im) / den
    f_im = (ab_im * a_re - (ab_re - 1.0) * a_im) / den
    bb_re = f_re[..., None] * b_re - f_im[..., None] * b_im
    bb_im = f_re[..., None] * b_im + f_im[..., None] * b_re
    d = jnp.arange(tc + 1, dtype=F32)[:, None, None]
    pmag = jnp.exp(d * lam_re)
    pw_re, pw_im = pmag * jnp.cos(d * lam_im), pmag * jnp.sin(d * lam_im)
    z_re = pw_re[:tc, :, :, None] * bb_re - pw_im[:tc, :, :, None] * bb_im
    z_im = pw_re[:tc, :, :, None] * bb_im + pw_im[:tc, :, :, None] * bb_re
    kmat = (jnp.einsum('gop,dgpi->gdio', c_re, z_re, precision=hp)
            - jnp.einsum('gop,dgpi->gdio', c_im, z_im, precision=hp))
    ti = jnp.arange(tc)
    lag = ti[None, :] - ti[:, None]
    place = (lag[None] == ti[:, None, None]).astype(F32)
    m = jnp.einsum('dij,gdxy->gixjy', place, kmat, precision=hp).reshape(g, tc * c, tc * c)
    m = m + jnp.eye(tc * c, dtype=F32) * jnp.tile(d_skip.reshape(g, 1, c), (1, tc, 1)).reshape(g, 1, tc * c)
    w_re = jnp.transpose(z_re[::-1], (1, 0, 3, 2)).reshape(g, tc * c, p)
    w_im = jnp.transpose(z_im[::-1], (1, 0, 3, 2)).reshape(g, tc * c, p)
    mw = jnp.concatenate([m, w_re, w_im], axis=2).astype(BF16)
    q_re, q_im = pw_re[1:], pw_im[1:]
    v_re = c_re[None] * q_re[:, :, None, :] - c_im[None] * q_im[:, :, None, :]
    v_im = c_re[None] * q_im[:, :, None, :] + c_im[None] * q_re[:, :, None, :]
    v_re = jnp.transpose(v_re, (1, 3, 0, 2)).reshape(g, p, tc * c).astype(BF16)
    v_im = jnp.transpose(-v_im, (1, 3, 0, 2)).reshape(g, p, tc * c).astype(BF16)
    return mw, v_re, v_im, pw_re[tc].reshape(1, g * p), pw_im[tc].reshape(1, g * p)


def _s5(su, weights, bsz, lp):
    del lp
    mw, v_re, v_im, a_re, a_im = weights
    t = su.shape[0]
    c, tc, p = SSM_GROUP, SSM_CHUNK, SSM_STATE
    ng = LANES // c
    rows = t // tc
    return pl.pallas_call(
        functools.partial(_s5_kernel, bsz=bsz),
        grid=(SSM_WIDTH // LANES,),
        in_specs=[
            pl.BlockSpec((t, LANES), lambda i: (0, i)),
            pl.BlockSpec((ng, tc * c, tc * c + 2 * p), lambda i: (i, 0, 0)),
            pl.BlockSpec((ng, p, tc * c), lambda i: (i, 0, 0)),
            pl.BlockSpec((ng, p, tc * c), lambda i: (i, 0, 0)),
            pl.BlockSpec((1, ng * p), lambda i: (0, i)),
            pl.BlockSpec((1, ng * p), lambda i: (0, i)),
        ],
        out_specs=pl.BlockSpec((t, LANES), lambda i: (0, i)),
        out_shape=jax.ShapeDtypeStruct((t, SSM_WIDTH), F32),
        scratch_shapes=[pltpu.VMEM((tc * LANES, tc * LANES), BF16),
                        pltpu.VMEM((rows, tc * LANES), F32), pltpu.VMEM((rows, tc * LANES), BF16),
                        pltpu.VMEM((rows, ng * p), F32), pltpu.VMEM((rows, ng * p), F32),
                        pltpu.VMEM((rows, ng * p), F32), pltpu.VMEM((rows, ng * p), F32)],
        compiler_params=_cparams(("arbitrary",)),
        name="s5",
    )(su, mw, v_re, v_im, a_re, a_im)


def _swiglu_accumulate(x, w1_ref, w3_ref, w2_ref, o_ref):
    a = _dot(x, w1_ref[0].astype(BF16))
    b = _dot(x, w3_ref[0].astype(BF16))
    act = (a * jax.nn.sigmoid(a) * b).astype(BF16)
    o_ref[...] += _dot(act, w2_ref[0].astype(BF16))


def _ffn_kernel(be_ref, nu_ref, x_ref, w1_ref, w3_ref, w2_ref, o_ref):
    i = pl.program_id(0)
    f = pl.program_id(1)

    @pl.when(f == 0)
    def _():
        o_ref[...] = jnp.zeros(o_ref.shape, F32)

    @pl.when(i < nu_ref[0])
    def _():
        _swiglu_accumulate(x_ref[...].astype(BF16), w1_ref, w3_ref, w2_ref, o_ref)


def _ffn_gather_kernel(be_ref, nu_ref, tok_ref, h_ref, w1_ref, w3_ref, w2_ref, o_ref, x_buf, sem, *, nf):
    i = pl.program_id(0)
    f = pl.program_id(1)
    tm = o_ref.shape[0]
    nu = nu_ref[0]
    slot = i % 2

    def row_copy(buf, tok, r):
        return pltpu.make_async_copy(h_ref.at[pl.ds(tok, 1), :], x_buf.at[buf, pl.ds(r, 1), :],
                                     sem.at[buf])

    def start_block(blk, buf):
        def body(r8, carry):
            for k in range(GATHER_UNROLL):
                r = r8 * GATHER_UNROLL + k
                row_copy(buf, tok_ref[blk * tm + r], r).start()
            return carry
        lax.fori_loop(0, tm // GATHER_UNROLL, body, 0)

    def wait_block(buf):
        def body(r8, carry):
            for k in range(GATHER_UNROLL):
                row_copy(buf, 0, r8 * GATHER_UNROLL + k).wait()
            return carry
        lax.fori_loop(0, tm // GATHER_UNROLL, body, 0)

    rows_per_step = tm // nf
    has_next = i + 1 < pl.num_programs(0)

    @pl.when(f == 0)
    def _():
        o_ref[...] = jnp.zeros(o_ref.shape, F32)

        @pl.when(i == 0)
        def _():
            start_block(0, 0)

        @pl.when(i <= nu)
        def _():
            wait_block(slot)

    def issue_share():
        for k in range(rows_per_step):
            r = f * rows_per_step + k
            row_copy(1 - slot, tok_ref[(i + 1) * tm + r], r).start()

    @pl.when((i < nu) & has_next)
    def _():
        issue_share()
        _swiglu_accumulate(x_buf[slot].astype(BF16), w1_ref, w3_ref, w2_ref, o_ref)

    @pl.when((i < nu) & jnp.logical_not(has_next))
    def _():
        _swiglu_accumulate(x_buf[slot].astype(BF16), w1_ref, w3_ref, w2_ref, o_ref)


def _ffn(x, w1, w3, w2, blk_expert, n_used, tm, row_token=None):
    d = x.shape[1]
    n = x.shape[0] if row_token is None else row_token.shape[0]
    dff = w1.shape[2]
    tf = _pick(dff, (256, 128))
    nf = dff // tf

    def live(i, nu):
        return jnp.minimum(i, nu[0] - 1)

    def fidx(i, f, nu):
        return jnp.where(i < nu[0], f, nf - 1)

    w_specs = [
        pl.BlockSpec((1, d, tf), lambda i, f, be, nu, *_: (be[live(i, nu)], 0, fidx(i, f, nu))),
        pl.BlockSpec((1, d, tf), lambda i, f, be, nu, *_: (be[live(i, nu)], 0, fidx(i, f, nu))),
        pl.BlockSpec((1, tf, d), lambda i, f, be, nu, *_: (be[live(i, nu)], fidx(i, f, nu), 0)),
    ]
    if row_token is None:
        kern, prefetch, scratch = _ffn_kernel, (blk_expert, n_used), []
        x_spec = pl.BlockSpec((tm, d), lambda i, f, be, nu: (live(i, nu), 0))
    else:
        assert tm % nf == 0
        kern, prefetch = functools.partial(_ffn_gather_kernel, nf=nf), (blk_expert, n_used, row_token)
        scratch = [pltpu.VMEM((2, tm, d), x.dtype), pltpu.SemaphoreType.DMA((2,))]
        x_spec = pl.BlockSpec(memory_space=pl.ANY)
    grid_spec = pltpu.PrefetchScalarGridSpec(
        num_scalar_prefetch=len(prefetch),
        grid=(n // tm, nf),
        in_specs=[x_spec] + w_specs,
        out_specs=pl.BlockSpec((tm, d), lambda i, f, *_: (i, 0)),
        scratch_shapes=scratch,
    )
    return pl.pallas_call(
        kern,
        grid_spec=grid_spec,
        out_shape=jax.ShapeDtypeStruct((n, d), F32),
        compiler_params=_cparams(("arbitrary", "arbitrary")),
        name="swiglu",
    )(*prefetch, x, w1, w3, w2)


def _router_kernel(h_ref, w_ref, idx_ref, gate_ref):
    hh, hl = _split_bf16(h_ref[...])
    wh, wl = _split_bf16(w_ref[...])
    logits = _dot_nt(wh, hh) + _dot_nt(wl, hh) + _dot_nt(wh, hl)
    e = lax.broadcasted_iota(I32, logits.shape, 0).astype(F32)
    m1 = jnp.max(logits, axis=0, keepdims=True)
    i1 = jnp.min(jnp.where(logits == m1, e, float(N_EXPERTS)), axis=0, keepdims=True)
    rest = jnp.where(e == i1, -jnp.inf, logits)
    m2 = jnp.max(rest, axis=0, keepdims=True)
    i2 = jnp.min(jnp.where(rest == m2, e, float(N_EXPERTS)), axis=0, keepdims=True)
    e2 = jnp.exp(m2 - m1)
    den = 1.0 + e2
    idx_ref[...] = jnp.concatenate([i1, i2], axis=0).astype(I32)
    gate_ref[...] = jnp.concatenate([1.0 / den, e2 / den], axis=0)


def _router(h, w_router_t):
    t, d = h.shape
    tm = _pick(t, (384, 256, 128))
    return pl.pallas_call(
        _router_kernel,
        grid=(t // tm,),
        in_specs=[pl.BlockSpec((tm, d), lambda i: (i, 0)), pl.BlockSpec((N_EXPERTS, d), lambda i: (0, 0))],
        out_specs=[pl.BlockSpec((TOP_K_EXPERTS, tm), lambda i: (0, i)),
                   pl.BlockSpec((TOP_K_EXPERTS, tm), lambda i: (0, i))],
        out_shape=[jax.ShapeDtypeStruct((TOP_K_EXPERTS, t), I32),
                   jax.ShapeDtypeStruct((TOP_K_EXPERTS, t), F32)],
        compiler_params=_cparams(("parallel",)),
        name="router",
    )(h, w_router_t)


def _moe(h, w_router, w1, w3, w2, e_base, ln_g, ln_b):
    t, d = h.shape
    tm = MOE_ROWS
    top_idx, gates = _router(h, w_router.T)
    n_assign = t * TOP_K_EXPERTS
    expert = top_idx.reshape(-1)
    onehot = (expert[:, None] == jnp.arange(N_EXPERTS)[None, :]).astype(I32)
    counts = jnp.sum(onehot, axis=0)
    padded = (counts + tm - 1) // tm * tm
    pad_end = jnp.cumsum(padded)
    dest = jnp.sum((jnp.cumsum(onehot, axis=0) - onehot + (pad_end - padded)[None, :]) * onehot, axis=1)
    n_blk = -(-(n_assign + N_EXPERTS * (tm - 1)) // tm)
    n_rows = n_blk * tm
    token = jnp.arange(n_assign, dtype=I32) % t
    row_token = jnp.zeros((n_rows,), I32).at[dest].set(token)
    blk_expert = jnp.minimum(
        jnp.sum((jnp.arange(n_blk)[:, None] * tm >= pad_end[None, :]).astype(I32), axis=1), N_EXPERTS - 1)
    n_used = (pad_end[-1:] // tm).astype(I32)
    ys = _ffn(h, w1, w3, w2, blk_expert + e_base, n_used, tm, row_token=row_token)
    return _moe_combine_ln(h, ys, dest.astype(I32), gates.T, ln_g, ln_b)


def _rope_tables(lp, n_heads, head_dim, off, rot_dim, scale=1.0):
    r = rot_dim // 2
    pos = jnp.arange(lp)
    inv = ROPE_THETA ** (-jnp.arange(0, rot_dim, 2, dtype=F32) / rot_dim)
    ang = pos.astype(F32)[:, None] * inv[None, :]
    cos, sin = jnp.cos(ang), jnp.sin(ang)
    c = jnp.ones((lp, head_dim), F32).at[:, off:off + r].set(cos).at[:, off + r:off + 2 * r].set(cos)
    sa = jnp.zeros((lp, head_dim), F32).at[:, off:off + r].set(-sin)
    sb = jnp.zeros((lp, head_dim), F32).at[:, off + r:off + 2 * r].set(sin)
    return tuple(jnp.tile(x * scale, (1, n_heads)) for x in (c, sa, sb)) + (r,)


def _pad_cols(w, n):
    return jnp.pad(w, ((0, 0), (0, n - w.shape[1])))


def kernel(x, meta, ln_in_g, ln_in_b, w_in, b_gate, mla_q_norm, mla_kv_norm, w_uq, w_ukv, ssm_a_re, ssm_a_im, ssm_log_dt, ssm_b_re, ssm_b_im, ssm_c_re, ssm_c_im, ssm_d, w_glu, w_branch_a, w_branch_b, w_branch_c, w_o, ln1_g, ln1_b, ffn_w1, ffn_w3, ffn_w2, w_router, moe_w1, moe_w3, moe_w2, ln2_g, ln2_b):
    bsz, seq, _ = x.shape
    n_tok = seq + N_META
    lp = -(-n_tok // BLOCK) * BLOCK
    t = bsz * lp
    n_sel = min(TOPK_MAX, seq // 4)
    meta_b = jnp.broadcast_to(meta[None].astype(x.dtype), (bsz, N_META, D_MODEL))
    pad = jnp.zeros((bsz, lp - n_tok, D_MODEL), x.dtype)
    h, hb = _layernorm(jnp.concatenate([meta_b, x, pad], axis=1).reshape(t, D_MODEL), ln_in_g, ln_in_b)

    a_scale = A_HEAD_DIM ** -0.5 * LOG2E
    rope_q = _rope_tables(lp, 1, A_HEAD_DIM, 0, A_ROT, a_scale)
    rope_k = _rope_tables(lp, 1, A_HEAD_DIM, 0, A_ROT)
    iq_tab = _rope_tables(lp, IDX_HEADS, IDX_DIM, 0, IDX_ROT)
    ik_tab = _rope_tables(lp, 1, LANES, 0, IDX_ROT)
    w_scale = (jnp.zeros((LANES,), F32).at[:IDX_DIM].set(1.0)
               .at[IDX_DIM:IDX_DIM + IDX_HEADS].set((IDX_HEADS * IDX_DIM) ** -0.5))
    rope_i = tuple(jnp.concatenate([a, b * w_scale[None, :], jnp.zeros((lp, LANES), F32)], axis=1)
                   for a, b in zip(iq_tab[:3], ik_tab[:3])) + (IDX_ROT // 2,)
    n_idx = IDX_HEADS * IDX_DIM + 2 * LANES
    rope_kr = _rope_tables(lp, 1, LANES, 0, MLA_ROPE)
    m_scale = (MLA_NOPE + MLA_ROPE) ** -0.5 * LOG2E
    rope_mq = _rope_tables(lp, 1, 2 * LANES, MLA_NOPE, MLA_ROPE, m_scale)

    offs = [0]
    for s in IN_SIZES:
        offs.append(offs[-1] + s)
    o_aq, o_ak, o_av, o_iq, o_ik, o_iw, o_dq, o_dkv, o_kr, o_su, o_gl, o_end = offs

    w_o_bf16 = w_o.astype(BF16)

    for layer in range(DEPTH):
        w_md = _pad_cols(w_in[layer, :, o_dq:o_su], MLA_Q_LORA + MLA_KV_LORA + LANES)
        w_su = w_in[layer, :, o_su:o_gl]

        q_a = _proj(hb, w_in, w_cols=(layer, o_aq, o_ak - o_aq), out_dtype=BF16, lp=lp, rope=rope_q,
                    name="proj_aq")
        k_a = _proj(hb, w_in, w_cols=(layer, o_ak, o_av - o_ak), out_dtype=BF16, lp=lp, rope=rope_k,
                    name="proj_ak")
        v_a = _proj(hb, w_in, w_cols=(layer, o_av, o_iq - o_av), out_dtype=BF16, lp=lp, name="proj_av")
        idx = _proj(h, w_in, w_cols=(layer, o_iq, n_idx), out_dtype=F32, lp=lp, rope=rope_i, hi_prec=True,
                    tm_prefs=(704, 384, 128), tn_prefs=(n_idx,), name="proj_idx")
        bias = _indexer(idx, bsz, lp, n_sel)
        out_a = _dsa_attention(q_a, k_a, v_a, bias, bsz, lp)

        dqkv, kr = _mla_down(hb, w_md, rope_kr, lp)
        wq = w_uq[layer].reshape(MLA_Q_LORA, MLA_HEADS, MLA_NOPE + MLA_ROPE)
        wq = jnp.pad(wq, ((0, 0), (0, 0), (0, 2 * LANES - MLA_NOPE - MLA_ROPE))).reshape(MLA_Q_LORA, -1)
        q_m = _proj(dqkv, wq, out_dtype=BF16, lp=lp, x_col=0, kdim=MLA_Q_LORA, rope=rope_mq,
                    rms_gain=mla_q_norm[layer], name="proj_mq")
        kv_m = _proj(dqkv, w_ukv, w_cols=(layer, 0, w_ukv.shape[2]), out_dtype=BF16, lp=lp, x_col=1,
                     kdim=MLA_KV_LORA, rms_gain=mla_kv_norm[layer], name="proj_mkv")
        out_b = _mla_attention(q_m, kv_m, kr, bsz, lp)

        su = _proj(hb, w_su, out_dtype=F32, lp=lp, name="proj_su")
        s5w = _s5_weights(ssm_a_re[layer], ssm_a_im[layer], ssm_log_dt[layer], ssm_b_re[layer],
                          ssm_b_im[layer], ssm_c_re[layer], ssm_c_im[layer], ssm_d[layer])
        y = _s5(su, s5w, bsz, lp)
        out_c = _glu(y, w_glu, layer, lp)

        w_gates = [w_in[layer, :, o_gl + k * D_MODEL:o_gl + (k + 1) * D_MODEL] for k in range(N_BRANCH)]
        merged = _merge(hb, out_a, out_b, out_c, w_gates, b_gate[layer], w_branch_a, w_branch_b,
                        w_branch_c, layer, lp)
        h, hb = _proj_ln(merged, w_o_bf16, layer, h, ln1_g[layer], ln1_b[layer])

        i = layer // 2
        if layer % 2 == 0:
            tm = _pick(t, (1056, 384, 128))
            nblk = t // tm
            f = _ffn(hb, ffn_w1, ffn_w3, ffn_w2, jnp.full((nblk,), i, I32), jnp.full((1,), nblk, I32), tm)
            h, hb = _layernorm(h, ln2_g[layer], ln2_b[layer], res=f)
        else:
            n_all = moe_w1.shape[0] * N_EXPERTS
            h, hb = _moe(h, w_router[i], moe_w1.reshape((n_all,) + moe_w1.shape[2:]),
                         moe_w3.reshape((n_all,) + moe_w3.shape[2:]),
                         moe_w2.reshape((n_all,) + moe_w2.shape[2:]), i * N_EXPERTS,
                         ln2_g[layer], ln2_b[layer])

    return h.reshape(bsz, lp, D_MODEL)[:, N_META:N_META + seq]
```

```python
import functools
import math

import jax
import jax.numpy as jnp
from jax import lax
from jax.experimental import pallas as pl
from jax.experimental.pallas import tpu as pltpu

F32 = jnp.float32
BF16 = jnp.bfloat16
I32 = jnp.int32

D_MODEL = 2048
DEPTH = 4
N_META = 16
BLOCK = 128
ROPE_THETA = 500000.0
LN_EPS = 1e-5
RMS_EPS = 1e-6
ALPHA = (2 * DEPTH) ** 0.25

A_HEADS = 8
A_KV_HEADS = 2
A_HEAD_DIM = 128
A_ROT = A_HEAD_DIM // 4
IDX_HEADS = 8
IDX_DIM = 64
IDX_ROT = IDX_DIM // 4
TOPK_MAX = 256

MLA_HEADS = 8
MLA_Q_LORA = 512
MLA_KV_LORA = 512
MLA_NOPE = 128
MLA_ROPE = 64
MLA_V = 128

SSM_WIDTH = 1024
SSM_GROUP = 16
SSM_GROUPS = SSM_WIDTH // SSM_GROUP
SSM_STATE = 64
SSM_CHUNK = 16

N_BRANCH = 3
A_WIDTH = A_HEADS * A_HEAD_DIM
MLA_WIDTH = MLA_HEADS * MLA_V
IN_SIZES = (A_HEADS * A_HEAD_DIM, A_KV_HEADS * A_HEAD_DIM, A_KV_HEADS * A_HEAD_DIM,
            IDX_HEADS * IDX_DIM, IDX_DIM, IDX_HEADS,
            MLA_Q_LORA, MLA_KV_LORA, MLA_ROPE,
            SSM_WIDTH, N_BRANCH * D_MODEL)

D_FF = 5632
N_EXPERTS = 8
TOP_K_EXPERTS = 2

LANES = 128
SUBLANES = 8
VMEM_LIMIT = 56 * 1024 * 1024
INT_MIN = -2 ** 31
LOG2E = math.log2(math.e)
GATHER_UNROLL = 8
MOE_ROWS = 704


def _pick(n, prefs):
    for p in prefs:
        if n % p == 0:
            return p
    raise ValueError(f"no tile in {prefs} divides {n}")


def _cparams(sem):
    return pltpu.CompilerParams(dimension_semantics=sem, vmem_limit_bytes=VMEM_LIMIT)


def _dot(a, b):
    return jnp.dot(a, b, preferred_element_type=F32)


def _dot_nt(a, b):
    return lax.dot_general(a, b, (((1,), (1,)), ((), ())), preferred_element_type=F32)


def _split_bf16(x):
    hi = x.astype(BF16)
    lo = (x - hi.astype(F32)).astype(BF16)
    return hi, lo


def _ln_math(x, g, b):
    mu = jnp.mean(x, axis=-1, keepdims=True)
    xc = x - mu
    var = jnp.mean(xc * xc, axis=-1, keepdims=True)
    return xc * lax.rsqrt(var + LN_EPS) * g + b


def _ln_kernel(x_ref, g_ref, b_ref, o_ref, ob_ref):
    y = _ln_math(x_ref[...], g_ref[...], b_ref[...])
    o_ref[...] = y
    ob_ref[...] = y.astype(BF16)


def _ln_res_kernel(h_ref, r_ref, g_ref, b_ref, o_ref, ob_ref):
    y = _ln_math(ALPHA * h_ref[...] + r_ref[...], g_ref[...], b_ref[...])
    o_ref[...] = y
    ob_ref[...] = y.astype(BF16)


def _moe_ln_kernel(pos_ref, h_ref, ys_ref, gt_ref, g_ref, b_ref, o_ref, ob_ref, y_buf, sem):
    i = pl.program_id(0)
    tm, t = h_ref.shape[0], pos_ref.shape[0] // TOP_K_EXPERTS
    slot = i % 2

    def row_copy(buf, k, src_row, r):
        return pltpu.make_async_copy(ys_ref.at[pl.ds(src_row, 1), :],
                                     y_buf.at[buf, k, pl.ds(r, 1), :], sem.at[buf])

    def start_block(blk, buf):
        def body(r8, carry):
            for k in range(TOP_K_EXPERTS):
                for u in range(GATHER_UNROLL):
                    r = r8 * GATHER_UNROLL + u
                    row_copy(buf, k, pos_ref[k * t + blk * tm + r], r).start()
            return carry
        lax.fori_loop(0, tm // GATHER_UNROLL, body, 0)

    def wait_block(buf):
        def body(r8, carry):
            for k in range(TOP_K_EXPERTS):
                for u in range(GATHER_UNROLL):
                    row_copy(buf, k, 0, r8 * GATHER_UNROLL + u).wait()
            return carry
        lax.fori_loop(0, tm // GATHER_UNROLL, body, 0)

    @pl.when(i == 0)
    def _():
        start_block(0, 0)

    wait_block(slot)

    @pl.when(i + 1 < pl.num_programs(0))
    def _():
        start_block(i + 1, 1 - slot)

    gt = gt_ref[...]
    f = y_buf[slot, 0] * gt[:, 0:1] + y_buf[slot, 1] * gt[:, 1:2]
    y = _ln_math(ALPHA * h_ref[...] + f, g_ref[...], b_ref[...])
    o_ref[...] = y
    ob_ref[...] = y.astype(BF16)


def _layernorm(x, g, b, res=None):
    t, d = x.shape
    tm = _pick(t, (384, 256, 128))
    row = pl.BlockSpec((tm, d), lambda i: (i, 0))
    vec = pl.BlockSpec((1, d), lambda i: (0, 0))
    ins = [x] if res is None else [x, res]
    return pl.pallas_call(
        _ln_kernel if res is None else _ln_res_kernel,
        grid=(t // tm,),
        in_specs=[row] * len(ins) + [vec, vec],
        out_specs=[row, row],
        out_shape=[jax.ShapeDtypeStruct((t, d), F32), jax.ShapeDtypeStruct((t, d), BF16)],
        compiler_params=_cparams(("parallel",)),
        name="layernorm",
    )(*ins, g.reshape(1, d), b.reshape(1, d))


def _proj_ln_kernel(x_ref, w_ref, h_ref, g_ref, b_ref, o_ref, ob_ref):
    y = _ln_math(ALPHA * h_ref[...] + _dot(x_ref[...], w_ref[...]), g_ref[...], b_ref[...])
    o_ref[...] = y
    ob_ref[...] = y.astype(BF16)


def _proj_ln(x, w, layer, h, g, b):
    t, k = x.shape
    d = w.shape[2]
    tm = _pick(t, (384, 256, 128))
    row = pl.BlockSpec((tm, d), lambda i: (i, 0))
    vec = pl.BlockSpec((1, d), lambda i: (0, 0))
    return pl.pallas_call(
        _proj_ln_kernel,
        grid=(t // tm,),
        in_specs=[pl.BlockSpec((tm, k), lambda i: (i, 0)),
                  pl.BlockSpec((None, k, d), lambda i: (layer, 0, 0)), row, vec, vec],
        out_specs=[row, row],
        out_shape=[jax.ShapeDtypeStruct((t, d), F32), jax.ShapeDtypeStruct((t, d), BF16)],
        compiler_params=_cparams(("parallel",)),
        name="proj_o_ln",
    )(x, w, h, g.reshape(1, d), b.reshape(1, d))


def _moe_combine_ln(h, ys, pos, gates_t, g, b):
    t, d = h.shape
    tm = _pick(t, (384, 256, 128))
    row = pl.BlockSpec((tm, d), lambda i, pos_ref: (i, 0))
    vec = pl.BlockSpec((1, d), lambda i, pos_ref: (0, 0))
    grid_spec = pltpu.PrefetchScalarGridSpec(
        num_scalar_prefetch=1,
        grid=(t // tm,),
        in_specs=[row, pl.BlockSpec(memory_space=pl.ANY),
                  pl.BlockSpec((tm, TOP_K_EXPERTS), lambda i, pos_ref: (i, 0)), vec, vec],
        out_specs=[row, row],
        scratch_shapes=[pltpu.VMEM((2, TOP_K_EXPERTS, tm, d), ys.dtype), pltpu.SemaphoreType.DMA((2,))],
    )
    return pl.pallas_call(
        _moe_ln_kernel,
        grid_spec=grid_spec,
        out_shape=[jax.ShapeDtypeStruct((t, d), F32), jax.ShapeDtypeStruct((t, d), BF16)],
        compiler_params=_cparams(("arbitrary",)),
        name="moe_combine_ln",
    )(pos, h, ys, gates_t, g.reshape(1, d), b.reshape(1, d))


def _rope_chunk(a, c, sa, sb, r):
    return a * c + pltpu.roll(a, LANES - r, 1) * sa + pltpu.roll(a, r, 1) * sb


def _proj_kernel(*refs, rope_r, has_bias, sigmoid, hi_prec, rms):
    it = iter(refs)
    x_ref = next(it)
    w_ref = next(it)
    n_ref = next(it) if rms else None
    tabs = (next(it), next(it), next(it)) if rope_r else None
    b_ref = next(it) if has_bias else None
    o_ref = next(it)
    x = x_ref[...]
    if rms:
        ms = jnp.mean(x * x, axis=-1, keepdims=True)
        x = x * lax.rsqrt(ms + RMS_EPS) * n_ref[...]
    if hi_prec:
        xh, xl = _split_bf16(x)
        wh, wl = _split_bf16(w_ref[...])
        acc = _dot(xh, wh) + _dot(xl, wh) + _dot(xh, wl)
    else:
        acc = _dot(x.astype(BF16), w_ref[...].astype(BF16))
    if has_bias:
        acc = acc + b_ref[...]
    if sigmoid:
        acc = jax.nn.sigmoid(acc)
    if rope_r:
        c_ref, sa_ref, sb_ref = tabs
        pw = c_ref.shape[1]
        for c in range(acc.shape[1] // LANES):
            sl = slice(c * LANES, (c + 1) * LANES)
            ts = slice((c * LANES) % pw, (c * LANES) % pw + LANES)
            o_ref[:, sl] = _rope_chunk(acc[:, sl], c_ref[:, ts], sa_ref[:, ts], sb_ref[:, ts],
                                       rope_r).astype(o_ref.dtype)
    else:
        o_ref[...] = acc.astype(o_ref.dtype)


def _proj(x, w, *, out_dtype, lp, x_col=0, kdim=None, w_cols=None, rope=None, bias=None, sigmoid=False,
          hi_prec=False, rms_gain=None, tm_prefs=(2112, 768, 384, 128), tn_prefs=(512, 384, 256, 128),
          name="proj"):
    t = x.shape[0]
    kdim = x.shape[1] if kdim is None else kdim
    tm = _pick(lp, tm_prefs)
    nrow = lp // tm
    if w_cols is None:
        n = w.shape[1]
        tn = _pick(n, tn_prefs)
        w_spec = pl.BlockSpec((kdim, tn), lambda i, j: (0, j))
    else:
        layer, col0, n = w_cols
        tn = _pick(math.gcd(n, col0) if col0 else n, tn_prefs)
        cb0 = col0 // tn
        w_spec = pl.BlockSpec((None, kdim, tn), lambda i, j: (layer, 0, cb0 + j))
    in_specs = [pl.BlockSpec((tm, kdim), lambda i, j: (i, x_col)), w_spec]
    ins = [x, w]
    if rms_gain is not None:
        in_specs.append(pl.BlockSpec((1, kdim), lambda i, j: (0, 0)))
        ins.append(rms_gain.reshape(1, kdim))
    if rope is not None:
        pw = rope[0].shape[1]
        if pw == n:
            tab = pl.BlockSpec((tm, tn), lambda i, j: (i % nrow, j))
        else:
            assert tn % pw == 0
            tab = pl.BlockSpec((tm, pw), lambda i, j: (i % nrow, 0))
        in_specs += [tab, tab, tab]
        ins += list(rope[:3])
    if bias is not None:
        in_specs.append(pl.BlockSpec((1, tn), lambda i, j: (0, j)))
        ins.append(bias.reshape(1, n))
    kern = functools.partial(_proj_kernel, rope_r=rope[3] if rope is not None else 0,
                             has_bias=bias is not None, sigmoid=sigmoid, hi_prec=hi_prec,
                             rms=rms_gain is not None)
    return pl.pallas_call(
        kern,
        grid=(t // tm, n // tn),
        in_specs=in_specs,
        out_specs=pl.BlockSpec((tm, tn), lambda i, j: (i, j)),
        out_shape=jax.ShapeDtypeStruct((t, n), out_dtype),
        compiler_params=_cparams(("parallel", "arbitrary")),
        name=name,
    )(*ins)


def _mla_down_kernel(x_ref, w_ref, c_ref, sa_ref, sb_ref, o_ref, kr_ref):
    acc = _dot(x_ref[...], w_ref[...].astype(BF16))
    nq = o_ref.shape[1]
    o_ref[...] = acc[:, :nq]
    kr_ref[...] = _rope_chunk(acc[:, nq:], c_ref[...], sa_ref[...], sb_ref[...],
                              MLA_ROPE // 2).astype(BF16)


def _mla_down(hb, w, rope, lp):
    t, d = hb.shape
    n = w.shape[1]
    nq = MLA_Q_LORA + MLA_KV_LORA
    tm = _pick(lp, (704, 384, 128))
    nrow = lp // tm
    tab = pl.BlockSpec((tm, LANES), lambda i: (i % nrow, 0))
    return pl.pallas_call(
        _mla_down_kernel,
        grid=(t // tm,),
        in_specs=[pl.BlockSpec((tm, d), lambda i: (i, 0)), pl.BlockSpec((d, n), lambda i: (0, 0)),
                  tab, tab, tab],
        out_specs=[pl.BlockSpec((tm, nq), lambda i: (i, 0)), pl.BlockSpec((tm, LANES), lambda i: (i, 0))],
        out_shape=[jax.ShapeDtypeStruct((t, nq), F32), jax.ShapeDtypeStruct((t, LANES), BF16)],
        compiler_params=_cparams(("parallel",)),
        name="mla_down",
    )(hb, w, *rope[:3])


def _glu_kernel(y_ref, w_ref, o_ref):
    n = o_ref.shape[1]
    y = y_ref[...].astype(BF16)
    ga = _dot(y, w_ref[:, :n].astype(BF16))
    gb = _dot(y, w_ref[:, n:].astype(BF16))
    o_ref[...] = (ga * jax.nn.sigmoid(gb)).astype(o_ref.dtype)


def _glu(y, w_glu, layer, lp):
    t, k = y.shape
    n = w_glu.shape[2] // 2
    tm = _pick(lp, (704, 384, 128))
    return pl.pallas_call(
        _glu_kernel,
        grid=(t // tm,),
        in_specs=[pl.BlockSpec((tm, k), lambda i: (i, 0)),
                  pl.BlockSpec((None, k, 2 * n), lambda i: (layer, 0, 0))],
        out_specs=pl.BlockSpec((tm, n), lambda i: (i, 0)),
        out_shape=jax.ShapeDtypeStruct((t, n), BF16),
        compiler_params=_cparams(("parallel",)),
        name="glu",
    )(y, w_glu)


def _merge_kernel(h_ref, a_ref, b_ref, c_ref, wg0_ref, wg1_ref, wg2_ref, bg_ref,
                  wa_ref, wb_ref, wc_ref, o_ref):
    h = h_ref[...]
    m = None
    for k, (wg_ref, x_ref, w_ref) in enumerate(((wg0_ref, a_ref, wa_ref), (wg1_ref, b_ref, wb_ref),
                                                (wg2_ref, c_ref, wc_ref))):
        gate = jax.nn.sigmoid(_dot(h, wg_ref[...].astype(BF16)) + bg_ref[k])
        term = gate * _dot(x_ref[...], w_ref[...].astype(BF16))
        m = term if m is None else m + term
    o_ref[...] = m.astype(o_ref.dtype)


def _merge(hb, out_a, out_b, out_c, w_gates, b_gate, wb_a, wb_b, wb_c, layer, lp):
    t = out_a.shape[0]
    n = wb_a.shape[2]
    tm = _pick(lp, (1056, 384, 128))
    tn = _pick(n, (256, 128))

    def act(arr):
        return pl.BlockSpec((tm, arr.shape[1]), lambda i, j: (i, 0))

    def wgt(arr):
        return pl.BlockSpec((None, arr.shape[1], tn), lambda i, j: (layer, 0, j))

    gate_w = pl.BlockSpec((hb.shape[1], tn), lambda i, j: (0, j))
    return pl.pallas_call(
        _merge_kernel,
        grid=(t // tm, n // tn),
        in_specs=[act(hb), act(out_a), act(out_b), act(out_c), gate_w, gate_w, gate_w,
                  pl.BlockSpec((N_BRANCH, 1, tn), lambda i, j: (0, 0, j)),
                  wgt(wb_a), wgt(wb_b), wgt(wb_c)],
        out_specs=pl.BlockSpec((tm, tn), lambda i, j: (i, j)),
        out_shape=jax.ShapeDtypeStruct((t, n), BF16),
        compiler_params=_cparams(("parallel", "arbitrary")),
        name="merge",
    )(hb, out_a, out_b, out_c, *w_gates, b_gate.reshape(N_BRANCH, 1, n), wb_a, wb_b, wb_c)


def _softmax_update(s, v, m_ref, l_ref, acc_ref):
    tk = s.shape[1]
    m_prev = m_ref[...]
    m_new = jnp.maximum(m_prev, jnp.max(s, axis=1, keepdims=True))
    alpha = jnp.exp2(m_prev - m_new)
    p = jnp.exp2(s - jnp.concatenate([m_new] * (tk // LANES), axis=1))
    l_ref[...] = alpha * l_ref[...] + jnp.sum(p, axis=1, keepdims=True)
    acc_ref[...] = alpha * acc_ref[...] + _dot(p.astype(BF16), v)
    m_ref[...] = m_new


def _mla_attn_kernel(q_ref, kv_ref, kr_ref, o_ref, s_ref, m_ref, l_ref, acc_ref):
    qi = pl.program_id(2)
    tq = q_ref.shape[0]
    m_ref[...] = jnp.full(m_ref.shape, -jnp.inf, F32)
    l_ref[...] = jnp.zeros(l_ref.shape, F32)
    acc_ref[...] = jnp.zeros(acc_ref.shape, F32)
    q = q_ref[...]

    def rows_of(kt):
        return pl.ds(pl.multiple_of(kt * tq, tq), tq)

    def produce(buf, kt):
        rows = rows_of(jnp.minimum(kt, qi))
        s_ref[buf] = _dot_nt(q, jnp.concatenate([kv_ref[rows, :MLA_NOPE], kr_ref[rows, :]], axis=1))

    def consume(s, kt):
        _softmax_update(s, kv_ref[rows_of(kt), MLA_NOPE:], m_ref, l_ref, acc_ref)

    produce(0, 0)
    produce(1, 1)

    def body(kp, carry):
        s0 = s_ref[0]
        s1 = s_ref[1]
        produce(0, 2 * kp + 2)
        produce(1, 2 * kp + 3)
        consume(s0, 2 * kp)
        consume(s1, 2 * kp + 1)
        return carry

    lax.fori_loop(0, qi // 2, body, 0)

    @pl.when(qi % 2 == 1)
    def _():
        consume(s_ref[0], qi - 1)

    row = lax.broadcasted_iota(I32, (tq, tq), 0)
    col = lax.broadcasted_iota(I32, (tq, tq), 1)
    consume(jnp.where(col <= row, s_ref[qi % 2], -jnp.inf), qi)
    o_ref[...] = (acc_ref[...] / l_ref[...]).astype(o_ref.dtype)


def _mla_attention(q, kv, kr, bsz, lp):
    t = q.shape[0]
    tq = _pick(lp, (384, 128))
    nq = lp // tq
    q3 = q.reshape(bsz, lp, q.shape[1])
    kv3 = kv.reshape(bsz, lp, kv.shape[1])
    kr3 = kr.reshape(bsz, lp, LANES)
    out = pl.pallas_call(
        _mla_attn_kernel,
        grid=(bsz, MLA_HEADS, nq),
        in_specs=[
            pl.BlockSpec((None, tq, 2 * LANES), lambda b, h, i: (b, i, h)),
            pl.BlockSpec((None, lp, MLA_NOPE + MLA_V), lambda b, h, i: (b, 0, h)),
            pl.BlockSpec((None, lp, LANES), lambda b, h, i: (b, 0, 0)),
        ],
        out_specs=pl.BlockSpec((None, tq, LANES), lambda b, h, i: (b, i, h)),
        out_shape=jax.ShapeDtypeStruct((bsz, lp, MLA_WIDTH), BF16),
        scratch_shapes=[pltpu.VMEM((2, tq, tq), F32),
                        pltpu.VMEM((tq, LANES), F32), pltpu.VMEM((tq, LANES), F32),
                        pltpu.VMEM((tq, MLA_V), F32)],
        compiler_params=_cparams(("parallel", "parallel", "arbitrary")),
        name="mla_attention",
    )(q3, kv3, kr3)
    return out.reshape(t, MLA_WIDTH)


def _dsa_attn_kernel(q_ref, k_ref, v_ref, bias_ref, o_ref, qg_ref, s_ref, m_ref, l_ref, acc_ref, *, tk):
    qi = pl.program_id(1)
    rep = A_HEADS // A_KV_HEADS
    nkt = tk // BLOCK
    m_ref[...] = jnp.full(m_ref.shape, -jnp.inf, F32)
    l_ref[...] = jnp.zeros(l_ref.shape, F32)
    acc_ref[...] = jnp.zeros(acc_ref.shape, F32)
    for g in range(A_KV_HEADS):
        for r in range(rep):
            h = g * rep + r
            qg_ref[g, r * BLOCK:(r + 1) * BLOCK, :] = q_ref[:, h * A_HEAD_DIM:(h + 1) * A_HEAD_DIM]

    def rows_of(kt):
        return pl.ds(pl.multiple_of(kt * tk, tk), tk)

    def scores(kt):
        rows = rows_of(kt)
        for g in range(A_KV_HEADS):
            s_ref[g] = _dot_nt(qg_ref[g], k_ref[rows, g * A_HEAD_DIM:(g + 1) * A_HEAD_DIM])

    def consume(kt, s):
        bias = jnp.concatenate([bias_ref[kt * nkt + j] for j in range(nkt)], axis=1)
        bias = jnp.concatenate([bias.astype(F32)] * rep, axis=0)
        for g in range(A_KV_HEADS):
            _softmax_update(s[g] + bias, v_ref[rows_of(kt), g * A_HEAD_DIM:(g + 1) * A_HEAD_DIM],
                            m_ref.at[g], l_ref.at[g], acc_ref.at[g])

    scores(0)

    def body(kt, carry):
        s = [s_ref[g] for g in range(A_KV_HEADS)]
        scores(kt + 1)
        consume(kt, s)
        return carry

    last = (qi * BLOCK) // tk
    lax.fori_loop(0, last, body, 0)
    consume(last, [s_ref[g] for g in range(A_KV_HEADS)])
    for g in range(A_KV_HEADS):
        o = acc_ref[g] / l_ref[g]
        for r in range(rep):
            h = g * rep + r
            o_ref[:, h * A_HEAD_DIM:(h + 1) * A_HEAD_DIM] = (
                o[r * BLOCK:(r + 1) * BLOCK]).astype(o_ref.dtype)


def _dsa_attention(q, k, v, bias, bsz, lp):
    t = q.shape[0]
    nb = lp // BLOCK
    tk = _pick(lp, (384, 128))
    kvw = A_KV_HEADS * A_HEAD_DIM
    q3 = q.reshape(bsz, lp, A_WIDTH)
    k3 = k.reshape(bsz, lp, kvw)
    v3 = v.reshape(bsz, lp, kvw)
    rows = (A_HEADS // A_KV_HEADS) * BLOCK
    out = pl.pallas_call(
        functools.partial(_dsa_attn_kernel, tk=tk),
        grid=(bsz, nb),
        in_specs=[
            pl.BlockSpec((None, BLOCK, A_WIDTH), lambda b, i: (b, i, 0)),
            pl.BlockSpec((None, lp, kvw), lambda b, i: (b, 0, 0)),
            pl.BlockSpec((None, lp, kvw), lambda b, i: (b, 0, 0)),
            pl.BlockSpec((None, None, nb, BLOCK, BLOCK), lambda b, i: (b, i, 0, 0, 0)),
        ],
        out_specs=pl.BlockSpec((None, BLOCK, A_WIDTH), lambda b, i: (b, i, 0)),
        out_shape=jax.ShapeDtypeStruct((bsz, lp, A_WIDTH), BF16),
        scratch_shapes=[pltpu.VMEM((A_KV_HEADS, rows, A_HEAD_DIM), BF16),
                        pltpu.VMEM((A_KV_HEADS, rows, tk), F32),
                        pltpu.VMEM((A_KV_HEADS, rows, LANES), F32),
                        pltpu.VMEM((A_KV_HEADS, rows, LANES), F32),
                        pltpu.VMEM((A_KV_HEADS, rows, A_HEAD_DIM), F32)],
        compiler_params=_cparams(("parallel", "arbitrary")),
        name="dsa_attention",
    )(q3, k3, v3, bias)
    return out.reshape(t, A_WIDTH)


def _indexer_kernel(iq_ref, kw_ref, o_ref, kcat_ref, qcat_ref, wb_ref, key_ref, *, n_sel):
    qb = pl.program_id(0)
    bsz, nb = o_ref.shape[0], o_ref.shape[1]
    batch = range(bsz)
    half = IDX_DIM
    lane = lax.broadcasted_iota(I32, (BLOCK, LANES), 1)
    row = lax.broadcasted_iota(I32, (BLOCK, LANES), 0)

    @pl.when(qb == 0)
    def _():
        for b in batch:
            kf = kw_ref[b]
            klane = lax.broadcasted_iota(I32, kf.shape, 1)
            kz = jnp.where(klane < half, kf, 0.0)
            hi = kz.astype(BF16).astype(F32)
            lo = kz - hi
            lp = kf.shape[0]
            kcat_ref[b, 0:lp, :] = jnp.concatenate(
                [(hi + pltpu.roll(hi, half, 1)).astype(BF16), lo.astype(BF16)], axis=1)
            kcat_ref[b, lp:lp + BLOCK, :] = jnp.zeros((BLOCK, 2 * LANES), BF16)

    for b in batch:
        wq = kw_ref[b, pl.ds(pl.multiple_of(qb * BLOCK, BLOCK), BLOCK), :]
        for h in range(IDX_HEADS):
            chunk = iq_ref[b, :, (h // 2) * LANES:(h // 2 + 1) * LANES]
            if h % 2 == 0:
                a = jnp.where(lane < half, chunk, 0.0)
            else:
                a = pltpu.roll(jnp.where(lane >= half, chunk, 0.0), half, 1)
            hi = a.astype(BF16).astype(F32)
            lo = a - hi
            qcat_ref[b, h * BLOCK:(h + 1) * BLOCK, :] = jnp.concatenate(
                [(hi + pltpu.roll(lo, half, 1)).astype(BF16), hi.astype(BF16)], axis=1)
            wb_ref[b, h] = jnp.broadcast_to(wq[:, half + h:half + h + 1], (BLOCK, LANES))

    qpos = qb * BLOCK + row

    def sortable(x):
        b = pltpu.bitcast(x, I32)
        return b ^ ((b >> 31) & 0x7FFFFFFF)

    n_pair = (qb + 2) // 2

    def score_pair(kp, carry):
        for b in batch:
            kblk = kcat_ref[b, pl.ds(pl.multiple_of(kp * 2 * BLOCK, 2 * BLOCK), 2 * BLOCK), :]
            s = _dot_nt(qcat_ref[b], kblk)
            for j in range(2):
                sc = jnp.zeros((BLOCK, LANES), F32)
                for h in range(IDX_HEADS):
                    sc = sc + (jnp.maximum(s[h * BLOCK:(h + 1) * BLOCK, j * LANES:(j + 1) * LANES], 0.0)
                               * wb_ref[b, h])
                kpos = (kp * 2 + j) * BLOCK + lane
                sc = jnp.where(kpos < N_META, jnp.inf, sc)
                sc = jnp.where(kpos <= qpos, sc, -jnp.inf)
                key_ref[b, kp * 2 + j] = sortable(sc)
        return carry

    lax.fori_loop(0, n_pair, score_pair, 0)

    def count(pred):
        def body(kp, cs):
            return tuple(c + jnp.where(pred(key_ref[b, kp * 2], b), 1.0, 0.0)
                         + jnp.where(pred(key_ref[b, kp * 2 + 1], b), 1.0, 0.0)
                         for b, c in zip(batch, cs))
        cs = lax.fori_loop(0, n_pair, body, tuple(jnp.zeros((BLOCK, LANES), F32) for _ in batch))
        return tuple(jnp.sum(c, axis=1, keepdims=True) for c in cs)

    def bit_step(i, thrs):
        bit = 31 - i
        cands = tuple(jnp.where(bit == 31, thr ^ INT_MIN, thr | (1 << jnp.minimum(bit, 30)))
                      for thr in thrs)
        counts = count(lambda k, b: k >= cands[b])
        return tuple(jnp.where(n >= n_sel, cand, thr) for n, cand, thr in zip(counts, cands, thrs))

    thrs = lax.fori_loop(0, 32, bit_step,
                         tuple(jnp.full((BLOCK, LANES), INT_MIN, I32) for _ in batch))
    n_ge = count(lambda k, b: k >= thrs[b])
    all_ties_fit = functools.reduce(jnp.maximum, [jnp.max(n) for n in n_ge]) <= n_sel

    @pl.when(all_ties_fit)
    def _():
        def emit(kt, carry):
            kpos = kt * BLOCK + lane
            for b in batch:
                o_ref[b, kt] = jnp.where((key_ref[b, kt] >= thrs[b]) & (kpos <= qpos),
                                         0.0, -jnp.inf).astype(o_ref.dtype)
            return carry

        lax.fori_loop(0, qb + 1, emit, 0)

    @pl.when(jnp.logical_not(all_ties_fit))
    def _():
        n_gt = count(lambda k, b: k > thrs[b])
        tri = (lax.broadcasted_iota(I32, (LANES, LANES), 0)
               <= lax.broadcasted_iota(I32, (LANES, LANES), 1)).astype(BF16)

        def emit(kt, takens):
            out = []
            kpos = kt * BLOCK + lane
            for b in batch:
                key = key_ref[b, kt]
                eq = key == thrs[b]
                rank = _dot(jnp.where(eq, 1.0, 0.0).astype(BF16), tri)
                sel = (key > thrs[b]) | (eq & (takens[b] + rank <= n_sel - n_gt[b]))
                o_ref[b, kt] = jnp.where(sel & (kpos <= qpos), 0.0, -jnp.inf).astype(o_ref.dtype)
                out.append(takens[b] + rank[:, LANES - 1:LANES])
            return tuple(out)

        lax.fori_loop(0, qb + 1, emit, tuple(jnp.zeros((BLOCK, 1), F32) for _ in batch))

    def fill(kt, carry):
        for b in batch:
            o_ref[b, kt] = jnp.full((BLOCK, LANES), -jnp.inf, o_ref.dtype)
        return carry

    lax.fori_loop(qb + 1, nb, fill, 0)


def _indexer(idx, bsz, lp, n_sel):
    nb = lp // BLOCK
    idx3 = idx.reshape(bsz, lp, idx.shape[1])
    nq = IDX_HEADS * IDX_DIM
    kw_col = nq // LANES
    return pl.pallas_call(
        functools.partial(_indexer_kernel, n_sel=n_sel),
        grid=(nb,),
        in_specs=[
            pl.BlockSpec((bsz, BLOCK, nq), lambda i: (0, i, 0)),
            pl.BlockSpec((bsz, lp, LANES), lambda i: (0, 0, kw_col)),
        ],
        out_specs=pl.BlockSpec((bsz, None, nb, BLOCK, BLOCK), lambda i: (0, i, 0, 0, 0)),
        out_shape=jax.ShapeDtypeStruct((bsz, nb, nb, BLOCK, BLOCK), BF16),
        scratch_shapes=[pltpu.VMEM((bsz, lp + BLOCK, 2 * LANES), BF16),
                        pltpu.VMEM((bsz, IDX_HEADS * BLOCK, 2 * LANES), BF16),
                        pltpu.VMEM((bsz, IDX_HEADS, BLOCK, LANES), F32),
                        pltpu.VMEM((bsz, nb + 1, BLOCK, LANES), I32)],
        compiler_params=_cparams(("arbitrary",)),
        name="indexer",
    )(idx3, idx3)


def _s5_kernel(x_ref, mw_ref, vre_ref, vim_ref, are_ref, aim_ref, o_ref,
               perm_ref, y_ref, yb_ref, sre_ref, sim_ref, pre_ref, pim_ref, *, bsz):
    tc, c, p = SSM_CHUNK, SSM_GROUP, SSM_STATE
    ng = LANES // c
    nm = tc * c
    rows = x_ref.shape[0] // tc
    per_b = rows // bsz
    width = tc * LANES

    @pl.when(pl.program_id(0) == 0)
    def _():
        src = lax.broadcasted_iota(I32, (width, width), 0)
        dst = lax.broadcasted_iota(I32, (width, width), 1)
        group = (src & (LANES - 1)) >> (c.bit_length() - 1)
        step_in_chunk = src >> (LANES.bit_length() - 1)
        want = group * nm + step_in_chunk * c + (src & (c - 1))
        perm_ref[...] = jnp.where(dst == want, 1.0, 0.0).astype(BF16)

    x_all = jnp.concatenate(
        [x_ref[pl.ds(j, rows, stride=tc), :].astype(BF16) for j in range(tc)], axis=1)
    u_all = _dot(x_all, perm_ref[...]).astype(BF16)
    for g in range(ng):
        r = _dot(u_all[:, g * nm:(g + 1) * nm], mw_ref[g])
        y_ref[:, g * nm:(g + 1) * nm] = r[:, :nm]
        sre_ref[:, g * p:(g + 1) * p] = r[:, nm:nm + p]
        sim_ref[:, g * p:(g + 1) * p] = r[:, nm + p:nm + 2 * p]
    ar = are_ref[...]
    ai = aim_ref[...]

    def step(ti, carry):
        out = []
        for b in range(bsz):
            cr, ci = carry[b]
            rs = pl.ds(pl.multiple_of(b * per_b + ti * SUBLANES, SUBLANES), SUBLANES)
            lr = sre_ref[rs, :]
            li = sim_ref[rs, :]
            before_r, before_i = [], []
            for k in range(SUBLANES):
                before_r.append(cr)
                before_i.append(ci)
                cr, ci = (ar * cr - ai * ci + lr[k:k + 1], ar * ci + ai * cr + li[k:k + 1])
            pre_ref[rs, :] = jnp.concatenate(before_r, axis=0)
            pim_ref[rs, :] = jnp.concatenate(before_i, axis=0)
            out.append((cr, ci))
        return tuple(out)

    zero = jnp.zeros((1, ng * p), F32)
    lax.fori_loop(0, per_b // SUBLANES, step, tuple((zero, zero) for _ in range(bsz)))
    for g in range(ng):
        y = (y_ref[:, g * nm:(g + 1) * nm]
             + _dot(pre_ref[:, g * p:(g + 1) * p].astype(BF16), vre_ref[g])
             + _dot(pim_ref[:, g * p:(g + 1) * p].astype(BF16), vim_ref[g]))
        yb_ref[:, g * nm:(g + 1) * nm] = y.astype(BF16)
    y_all = _dot_nt(yb_ref[...], perm_ref[...])
    for j in range(tc):
        o_ref[pl.ds(j, rows, stride=tc), :] = y_all[:, j * LANES:(j + 1) * LANES]


def _s5_weights(a_re, a_im, log_dt, b_re, b_im, c_re, c_im, d_skip):
    hp = lax.Precision.HIGHEST
    g, p, c, tc = SSM_GROUPS, SSM_STATE, SSM_GROUP, SSM_CHUNK
    dt = jnp.exp(log_dt)[:, None]
    lam_re, lam_im = dt * a_re, dt * a_im
    mag = jnp.exp(lam_re)
    ab_re, ab_im = mag * jnp.cos(lam_im), mag * jnp.sin(lam_im)
    den = a_re * a_re + a_im * a_im
    f_re = ((ab_re - 1.0) * a_re + ab_im * a_im) / den
    f_im = (ab_im * a_re - (ab_re - 1.0) * a_im) / den
    bb_re = f_re[..., None] * b_re - f_im[..., None] * b_im
    bb_im = f_re[..., None] * b_im + f_im[..., None] * b_re
    d = jnp.arange(tc + 1, dtype=F32)[:, None, None]
    pmag = jnp.exp(d * lam_re)
    pw_re, pw_im = pmag * jnp.cos(d * lam_im), pmag * jnp.sin(d * lam_im)
    z_re = pw_re[:tc, :, :, None] * bb_re - pw_im[:tc, :, :, None] * bb_im
    z_im = pw_re[:tc, :, :, None] * bb_im + pw_im[:tc, :, :, None] * bb_re
    kmat = (jnp.einsum('gop,dgpi->gdio', c_re, z_re, precision=hp)
            - jnp.einsum('gop,dgpi->gdio', c_im, z_im, precision=hp))
    ti = jnp.arange(tc)
    lag = ti[None, :] - ti[:, None]
    place = (lag[None] == ti[:, None, None]).astype(F32)
    m = jnp.einsum('dij,gdxy->gixjy', place, kmat, precision=hp).reshape(g, tc * c, tc * c)
    m = m + jnp.eye(tc * c, dtype=F32) * jnp.tile(d_skip.reshape(g, 1, c), (1, tc, 1)).reshape(g, 1, tc * c)
    w_re = jnp.transpose(z_re[::-1], (1, 0, 3, 2)).reshape(g, tc * c, p)
    w_im = jnp.transpose(z_im[::-1], (1, 0, 3, 2)).reshape(g, tc * c, p)
    mw = jnp.concatenate([m, w_re, w_im], axis=2).astype(BF16)
    q_re, q_im = pw_re[1:], pw_im[1:]
    v_re = c_re[None] * q_re[:, :, None, :] - c_im[None] * q_im[:, :, None, :]
    v_im = c_re[None] * q_im[:, :, None, :] + c_im[None] * q_re[:, :, None, :]
    v_re = jnp.transpose(v_re, (1, 3, 0, 2)).reshape(g, p, tc * c).astype(BF16)
    v_im = jnp.transpose(-v_im, (1, 3, 0, 2)).reshape(g, p, tc * c).astype(BF16)
    return mw, v_re, v_im, pw_re[tc].reshape(1, g * p), pw_im[tc].reshape(1, g * p)


def _s5(su, weights, bsz, lp):
    del lp
    mw, v_re, v_im, a_re, a_im = weights
    t = su.shape[0]
    c, tc, p = SSM_GROUP, SSM_CHUNK, SSM_STATE
    ng = LANES // c
    rows = t // tc
    return pl.pallas_call(
        functools.partial(_s5_kernel, bsz=bsz),
        grid=(SSM_WIDTH // LANES,),
        in_specs=[
            pl.BlockSpec((t, LANES), lambda i: (0, i)),
            pl.BlockSpec((ng, tc * c, tc * c + 2 * p), lambda i: (i, 0, 0)),
            pl.BlockSpec((ng, p, tc * c), lambda i: (i, 0, 0)),
            pl.BlockSpec((ng, p, tc * c), lambda i: (i, 0, 0)),
            pl.BlockSpec((1, ng * p), lambda i: (0, i)),
            pl.BlockSpec((1, ng * p), lambda i: (0, i)),
        ],
        out_specs=pl.BlockSpec((t, LANES), lambda i: (0, i)),
        out_shape=jax.ShapeDtypeStruct((t, SSM_WIDTH), F32),
        scratch_shapes=[pltpu.VMEM((tc * LANES, tc * LANES), BF16),
                        pltpu.VMEM((rows, tc * LANES), F32), pltpu.VMEM((rows, tc * LANES), BF16),
                        pltpu.VMEM((rows, ng * p), F32), pltpu.VMEM((rows, ng * p), F32),
                        pltpu.VMEM((rows, ng * p), F32), pltpu.VMEM((rows, ng * p), F32)],
        compiler_params=_cparams(("arbitrary",)),
        name="s5",
    )(su, mw, v_re, v_im, a_re, a_im)


def _swiglu_accumulate(x, w1_ref, w3_ref, w2_ref, o_ref):
    a = _dot(x, w1_ref[0].astype(BF16))
    b = _dot(x, w3_ref[0].astype(BF16))
    act = (a * jax.nn.sigmoid(a) * b).astype(BF16)
    o_ref[...] += _dot(act, w2_ref[0].astype(BF16))


def _ffn_kernel(be_ref, nu_ref, x_ref, w1_ref, w3_ref, w2_ref, o_ref):
    i = pl.program_id(0)
    f = pl.program_id(1)

    @pl.when(f == 0)
    def _():
        o_ref[...] = jnp.zeros(o_ref.shape, F32)

    @pl.when(i < nu_ref[0])
    def _():
        _swiglu_accumulate(x_ref[...].astype(BF16), w1_ref, w3_ref, w2_ref, o_ref)


def _ffn_gather_kernel(be_ref, nu_ref, tok_ref, h_ref, w1_ref, w3_ref, w2_ref, o_ref, x_buf, sem, *, nf):
    i = pl.program_id(0)
    f = pl.program_id(1)
    tm = o_ref.shape[0]
    nu = nu_ref[0]
    slot = i % 2

    def row_copy(buf, tok, r):
        return pltpu.make_async_copy(h_ref.at[pl.ds(tok, 1), :], x_buf.at[buf, pl.ds(r, 1), :],
                                     sem.at[buf])

    def start_block(blk, buf):
        def body(r8, carry):
            for k in range(GATHER_UNROLL):
                r = r8 * GATHER_UNROLL + k
                row_copy(buf, tok_ref[blk * tm + r], r).start()
            return carry
        lax.fori_loop(0, tm // GATHER_UNROLL, body, 0)

    def wait_block(buf):
        def body(r8, carry):
            for k in range(GATHER_UNROLL):
                row_copy(buf, 0, r8 * GATHER_UNROLL + k).wait()
            return carry
        lax.fori_loop(0, tm // GATHER_UNROLL, body, 0)

    rows_per_step = tm // nf
    has_next = i + 1 < pl.num_programs(0)

    @pl.when(f == 0)
    def _():
        o_ref[...] = jnp.zeros(o_ref.shape, F32)

        @pl.when(i == 0)
        def _():
            start_block(0, 0)

        @pl.when(i <= nu)
        def _():
            wait_block(slot)

    def issue_share():
        for k in range(rows_per_step):
            r = f * rows_per_step + k
            row_copy(1 - slot, tok_ref[(i + 1) * tm + r], r).start()

    @pl.when((i < nu) & has_next)
    def _():
        issue_share()
        _swiglu_accumulate(x_buf[slot].astype(BF16), w1_ref, w3_ref, w2_ref, o_ref)

    @pl.when((i < nu) & jnp.logical_not(has_next))
    def _():
        _swiglu_accumulate(x_buf[slot].astype(BF16), w1_ref, w3_ref, w2_ref, o_ref)


def _ffn(x, w1, w3, w2, blk_expert, n_used, tm, row_token=None, tf_prefs=(256, 128)):
    d = x.shape[1]
    n = x.shape[0] if row_token is None else row_token.shape[0]
    dff = w1.shape[2]
    tf = _pick(dff, tf_prefs)
    nf = dff // tf

    def live(i, nu):
        return jnp.minimum(i, nu[0] - 1)

    def fidx(i, f, nu):
        return jnp.where(i < nu[0], f, nf - 1)

    w_specs = [
        pl.BlockSpec((1, d, tf), lambda i, f, be, nu, *_: (be[live(i, nu)], 0, fidx(i, f, nu))),
        pl.BlockSpec((1, d, tf), lambda i, f, be, nu, *_: (be[live(i, nu)], 0, fidx(i, f, nu))),
        pl.BlockSpec((1, tf, d), lambda i, f, be, nu, *_: (be[live(i, nu)], fidx(i, f, nu), 0)),
    ]
    if row_token is None:
        kern, prefetch, scratch = _ffn_kernel, (blk_expert, n_used), []
        x_spec = pl.BlockSpec((tm, d), lambda i, f, be, nu: (live(i, nu), 0))
    else:
        assert tm % nf == 0
        kern, prefetch = functools.partial(_ffn_gather_kernel, nf=nf), (blk_expert, n_used, row_token)
        scratch = [pltpu.VMEM((2, tm, d), x.dtype), pltpu.SemaphoreType.DMA((2,))]
        x_spec = pl.BlockSpec(memory_space=pl.ANY)
    grid_spec = pltpu.PrefetchScalarGridSpec(
        num_scalar_prefetch=len(prefetch),
        grid=(n // tm, nf),
        in_specs=[x_spec] + w_specs,
        out_specs=pl.BlockSpec((tm, d), lambda i, f, *_: (i, 0)),
        scratch_shapes=scratch,
    )
    return pl.pallas_call(
        kern,
        grid_spec=grid_spec,
        out_shape=jax.ShapeDtypeStruct((n, d), F32),
        compiler_params=_cparams(("arbitrary", "arbitrary")),
        name="swiglu",
    )(*prefetch, x, w1, w3, w2)


def _router_kernel(h_ref, w_ref, idx_ref, gate_ref):
    hh, hl = _split_bf16(h_ref[...])
    wh, wl = _split_bf16(w_ref[...])
    logits = _dot_nt(wh, hh) + _dot_nt(wl, hh) + _dot_nt(wh, hl)
    e = lax.broadcasted_iota(I32, logits.shape, 0).astype(F32)
    m1 = jnp.max(logits, axis=0, keepdims=True)
    i1 = jnp.min(jnp.where(logits == m1, e, float(N_EXPERTS)), axis=0, keepdims=True)
    rest = jnp.where(e == i1, -jnp.inf, logits)
    m2 = jnp.max(rest, axis=0, keepdims=True)
    i2 = jnp.min(jnp.where(rest == m2, e, float(N_EXPERTS)), axis=0, keepdims=True)
    e2 = jnp.exp(m2 - m1)
    den = 1.0 + e2
    idx_ref[...] = jnp.concatenate([i1, i2], axis=0).astype(I32)
    gate_ref[...] = jnp.concatenate([1.0 / den, e2 / den], axis=0)


def _router(h, w_router_t):
    t, d = h.shape
    tm = _pick(t, (384, 256, 128))
    return pl.pallas_call(
        _router_kernel,
        grid=(t // tm,),
        in_specs=[pl.BlockSpec((tm, d), lambda i: (i, 0)), pl.BlockSpec((N_EXPERTS, d), lambda i: (0, 0))],
        out_specs=[pl.BlockSpec((TOP_K_EXPERTS, tm), lambda i: (0, i)),
                   pl.BlockSpec((TOP_K_EXPERTS, tm), lambda i: (0, i))],
        out_shape=[jax.ShapeDtypeStruct((TOP_K_EXPERTS, t), I32),
                   jax.ShapeDtypeStruct((TOP_K_EXPERTS, t), F32)],
        compiler_params=_cparams(("parallel",)),
        name="router",
    )(h, w_router_t)


def _moe(h, w_router, w1, w3, w2, e_base, ln_g, ln_b):
    t, d = h.shape
    tm = MOE_ROWS
    top_idx, gates = _router(h, w_router.T)
    n_assign = t * TOP_K_EXPERTS
    expert = top_idx.reshape(-1)
    onehot = (expert[:, None] == jnp.arange(N_EXPERTS)[None, :]).astype(I32)
    counts = jnp.sum(onehot, axis=0)
    padded = (counts + tm - 1) // tm * tm
    pad_end = jnp.cumsum(padded)
    dest = jnp.sum((jnp.cumsum(onehot, axis=0) - onehot + (pad_end - padded)[None, :]) * onehot, axis=1)
    n_blk = -(-(n_assign + N_EXPERTS * (tm - 1)) // tm)
    n_rows = n_blk * tm
    token = jnp.arange(n_assign, dtype=I32) % t
    row_token = jnp.zeros((n_rows,), I32).at[dest].set(token)
    blk_expert = jnp.minimum(
        jnp.sum((jnp.arange(n_blk)[:, None] * tm >= pad_end[None, :]).astype(I32), axis=1), N_EXPERTS - 1)
    n_used = (pad_end[-1:] // tm).astype(I32)
    ys = _ffn(h, w1, w3, w2, blk_expert + e_base, n_used, tm, row_token=row_token)
    return _moe_combine_ln(h, ys, dest.astype(I32), gates.T, ln_g, ln_b)


def _rope_tables(lp, n_heads, head_dim, off, rot_dim, scale=1.0):
    r = rot_dim // 2
    pos = jnp.arange(lp)
    inv = ROPE_THETA ** (-jnp.arange(0, rot_dim, 2, dtype=F32) / rot_dim)
    ang = pos.astype(F32)[:, None] * inv[None, :]
    cos, sin = jnp.cos(ang), jnp.sin(ang)
    c = jnp.ones((lp, head_dim), F32).at[:, off:off + r].set(cos).at[:, off + r:off + 2 * r].set(cos)
    sa = jnp.zeros((lp, head_dim), F32).at[:, off:off + r].set(-sin)
    sb = jnp.zeros((lp, head_dim), F32).at[:, off + r:off + 2 * r].set(sin)
    return tuple(jnp.tile(x * scale, (1, n_heads)) for x in (c, sa, sb)) + (r,)


def _pad_cols(w, n):
    return jnp.pad(w, ((0, 0), (0, n - w.shape[1])))


def kernel(x, meta, ln_in_g, ln_in_b, w_in, b_gate, mla_q_norm, mla_kv_norm, w_uq, w_ukv, ssm_a_re, ssm_a_im, ssm_log_dt, ssm_b_re, ssm_b_im, ssm_c_re, ssm_c_im, ssm_d, w_glu, w_branch_a, w_branch_b, w_branch_c, w_o, ln1_g, ln1_b, ffn_w1, ffn_w3, ffn_w2, w_router, moe_w1, moe_w3, moe_w2, ln2_g, ln2_b):
    bsz, seq, _ = x.shape
    n_tok = seq + N_META
    lp = -(-n_tok // BLOCK) * BLOCK
    t = bsz * lp
    n_sel = min(TOPK_MAX, seq // 4)
    meta_b = jnp.broadcast_to(meta[None].astype(x.dtype), (bsz, N_META, D_MODEL))
    pad = jnp.zeros((bsz, lp - n_tok, D_MODEL), x.dtype)
    h, hb = _layernorm(jnp.concatenate([meta_b, x, pad], axis=1).reshape(t, D_MODEL), ln_in_g, ln_in_b)

    a_scale = A_HEAD_DIM ** -0.5 * LOG2E
    rope_q = _rope_tables(lp, 1, A_HEAD_DIM, 0, A_ROT, a_scale)
    rope_k = _rope_tables(lp, 1, A_HEAD_DIM, 0, A_ROT)
    iq_tab = _rope_tables(lp, IDX_HEADS, IDX_DIM, 0, IDX_ROT)
    ik_tab = _rope_tables(lp, 1, LANES, 0, IDX_ROT)
    w_scale = (jnp.zeros((LANES,), F32).at[:IDX_DIM].set(1.0)
               .at[IDX_DIM:IDX_DIM + IDX_HEADS].set((IDX_HEADS * IDX_DIM) ** -0.5))
    rope_i = tuple(jnp.concatenate([a, b * w_scale[None, :], jnp.zeros((lp, LANES), F32)], axis=1)
                   for a, b in zip(iq_tab[:3], ik_tab[:3])) + (IDX_ROT // 2,)
    n_idx = IDX_HEADS * IDX_DIM + 2 * LANES
    rope_kr = _rope_tables(lp, 1, LANES, 0, MLA_ROPE)
    m_scale = (MLA_NOPE + MLA_ROPE) ** -0.5 * LOG2E
    rope_mq = _rope_tables(lp, 1, 2 * LANES, MLA_NOPE, MLA_ROPE, m_scale)

    offs = [0]
    for s in IN_SIZES:
        offs.append(offs[-1] + s)
    o_aq, o_ak, o_av, o_iq, o_ik, o_iw, o_dq, o_dkv, o_kr, o_su, o_gl, o_end = offs

    w_o_bf16 = w_o.astype(BF16)

    for layer in range(DEPTH):
        w_md = _pad_cols(w_in[layer, :, o_dq:o_su], MLA_Q_LORA + MLA_KV_LORA + LANES)
        w_su = w_in[layer, :, o_su:o_gl]

        q_a = _proj(hb, w_in, w_cols=(layer, o_aq, o_ak - o_aq), out_dtype=BF16, lp=lp, rope=rope_q,
                    name="proj_aq")
        k_a = _proj(hb, w_in, w_cols=(layer, o_ak, o_av - o_ak), out_dtype=BF16, lp=lp, rope=rope_k,
                    name="proj_ak")
        v_a = _proj(hb, w_in, w_cols=(layer, o_av, o_iq - o_av), out_dtype=BF16, lp=lp, name="proj_av")
        idx = _proj(h, w_in, w_cols=(layer, o_iq, n_idx), out_dtype=F32, lp=lp, rope=rope_i, hi_prec=True,
                    tm_prefs=(704, 384, 128), tn_prefs=(n_idx,), name="proj_idx")
        bias = _indexer(idx, bsz, lp, n_sel)
        out_a = _dsa_attention(q_a, k_a, v_a, bias, bsz, lp)

        dqkv, kr = _mla_down(hb, w_md, rope_kr, lp)
        wq = w_uq[layer].reshape(MLA_Q_LORA, MLA_HEADS, MLA_NOPE + MLA_ROPE)
        wq = jnp.pad(wq, ((0, 0), (0, 0), (0, 2 * LANES - MLA_NOPE - MLA_ROPE))).reshape(MLA_Q_LORA, -1)
        q_m = _proj(dqkv, wq, out_dtype=BF16, lp=lp, x_col=0, kdim=MLA_Q_LORA, rope=rope_mq,
                    rms_gain=mla_q_norm[layer], name="proj_mq")
        kv_m = _proj(dqkv, w_ukv, w_cols=(layer, 0, w_ukv.shape[2]), out_dtype=BF16, lp=lp, x_col=1,
                     kdim=MLA_KV_LORA, rms_gain=mla_kv_norm[layer], name="proj_mkv")
        out_b = _mla_attention(q_m, kv_m, kr, bsz, lp)

        su = _proj(hb, w_su, out_dtype=F32, lp=lp, name="proj_su")
        s5w = _s5_weights(ssm_a_re[layer], ssm_a_im[layer], ssm_log_dt[layer], ssm_b_re[layer],
                          ssm_b_im[layer], ssm_c_re[layer], ssm_c_im[layer], ssm_d[layer])
        y = _s5(su, s5w, bsz, lp)
        out_c = _glu(y, w_glu, layer, lp)

        w_gates = [w_in[layer, :, o_gl + k * D_MODEL:o_gl + (k + 1) * D_MODEL] for k in range(N_BRANCH)]
        merged = _merge(hb, out_a, out_b, out_c, w_gates, b_gate[layer], w_branch_a, w_branch_b,
                        w_branch_c, layer, lp)
        h, hb = _proj_ln(merged, w_o_bf16, layer, h, ln1_g[layer], ln1_b[layer])

        i = layer // 2
        if layer % 2 == 0:
            tm = _pick(t, (704, 384, 128))
            nblk = t // tm
            f = _ffn(hb, ffn_w1, ffn_w3, ffn_w2, jnp.full((nblk,), i, I32), jnp.full((1,), nblk, I32), tm,
                     tf_prefs=(512, 256, 128))
            h, hb = _layernorm(h, ln2_g[layer], ln2_b[layer], res=f)
        else:
            n_all = moe_w1.shape[0] * N_EXPERTS
            h, hb = _moe(h, w_router[i], moe_w1.reshape((n_all,) + moe_w1.shape[2:]),
                         moe_w3.reshape((n_all,) + moe_w3.shape[2:]),
                         moe_w2.reshape((n_all,) + moe_w2.shape[2:]), i * N_EXPERTS,
                         ln2_g[layer], ln2_b[layer])

    return h.reshape(bsz, lp, D_MODEL)[:, N_META:N_META + seq]
```
